```python
import jax, jax.numpy as jnp
from jax import lax
import numpy as np

D_MODEL = 1024
BATCH = 4
SEQ = 8192
DEPTH = 2

GRID_W = 64
CTX_LEN = 256
EPS = 1e-6
N_HEADS = 8
N_KV_HEADS = 2
HEAD_DIM = 64
ATTN_W = N_HEADS * HEAD_DIM
KV_W = N_KV_HEADS * HEAD_DIM
CONV_W = D_MODEL - ATTN_W
CONV_TAPS = 3
Q_BLOCK = 128
ROPE_THETA = 10000.0
AXIS_DIM = HEAD_DIM // 2
IN_SPLITS = (ATTN_W, ATTN_W + KV_W, ATTN_W + 2 * KV_W, ATTN_W + 2 * KV_W + CONV_W, ATTN_W + 2 * KV_W + 2 * CONV_W)
IN_W = ATTN_W + 2 * KV_W + 3 * CONV_W
CHUNK = 128
GM_W = D_MODEL
GM_GROUPS = 8
GM_GC = GM_W // GM_GROUPS
D_FF = 2816
N_EXPERTS = 8
TOP_K = 2
EXPERT_FF = 3584
N_EVEN = (DEPTH + 1) // 2
N_ODD = DEPTH // 2

kernel_name = "hybrid_dit_attn_conv_gmlp_moe"


def _rms_norm(x, g):
    xf = x.astype(jnp.float32)
    y = xf * lax.rsqrt(jnp.mean(xf * xf, axis=-1, keepdims=True) + EPS)
    return (y * g.astype(jnp.float32)).astype(x.dtype)


def _layer_norm(x, g, b):
    xf = x.astype(jnp.float32)
    mu = jnp.mean(xf, axis=-1, keepdims=True)
    var = jnp.mean(jnp.square(xf - mu), axis=-1, keepdims=True)
    y = (xf - mu) * lax.rsqrt(var + EPS)
    return (y * g.astype(jnp.float32) + b.astype(jnp.float32)).astype(x.dtype)


def _modulation(cond, w_mod, b_mod):
    return jnp.split(jax.nn.silu(cond) @ w_mod + b_mod, 6, axis=-1)


def _pre(x, g, shift, scale):
    return _rms_norm(x, g) * (1.0 + scale) + shift


def _axial_rope(n):
    rows = n // GRID_W
    r, col = jnp.meshgrid(jnp.arange(rows, dtype=jnp.float32), jnp.arange(GRID_W, dtype=jnp.float32), indexing="ij")
    inv = 1.0 / (ROPE_THETA ** (jnp.arange(0, AXIS_DIM, 2, dtype=jnp.float32) / AXIS_DIM))
    ang = jnp.concatenate([r.reshape(-1, 1) * inv, col.reshape(-1, 1) * inv], axis=-1)
    return jnp.cos(ang), jnp.sin(ang)


def _apply_rope(x, cos, sin):
    cos = cos[None, :, None, :].astype(x.dtype)
    sin = sin[None, :, None, :].astype(x.dtype)
    x1, x2 = jnp.split(x, 2, axis=-1)
    return jnp.concatenate([x1 * cos - x2 * sin, x2 * cos + x1 * sin], axis=-1)


def _block_attention(q, k, v):
    b, n, h, d = q.shape
    kvh = k.shape[2]
    g = h // kvh
    qb = q.reshape(b, n // Q_BLOCK, Q_BLOCK, kvh, g, d).transpose(1, 0, 2, 3, 4, 5)
    scale = d ** -0.5

    def one_block(qblk):
        s = jnp.einsum("bqkgd,bmkd->bkgqm", qblk, k).astype(jnp.float32) * scale
        p = jax.nn.softmax(s, axis=-1).astype(v.dtype)
        return jnp.einsum("bkgqm,bmkd->bqkgd", p, v)

    o = lax.map(one_block, qb)
    return o.transpose(1, 0, 2, 3, 4, 5).reshape(b, n, h * d)


def _short_conv(z, w_conv):
    zp = jnp.pad(z, ((0, 0), (1, 1), (0, 0)))
    return w_conv[0] * zp[:, :-2] + w_conv[1] * zp[:, 1:-1] + w_conv[2] * zp[:, 2:]


def _context_kv(hc, w_in, g_k):
    b, m, _ = hc.shape
    k, v = jnp.split(hc @ w_in[:, ATTN_W:ATTN_W + 2 * KV_W], 2, axis=-1)
    k = _rms_norm(k.reshape(b, m, N_KV_HEADS, HEAD_DIM), g_k)
    return k, v.reshape(b, m, N_KV_HEADS, HEAD_DIM)


def _attn_conv_mixer(h, w_in, g_q, g_k, w_conv, w_out, rope, ext_kv):
    b, n, _ = h.shape
    q, k, v, gate_b, gate_c, hv = jnp.split(h @ w_in, IN_SPLITS, axis=-1)
    q = _rms_norm(q.reshape(b, n, N_HEADS, HEAD_DIM), g_q)
    k = _rms_norm(k.reshape(b, n, N_KV_HEADS, HEAD_DIM), g_k)
    v = v.reshape(b, n, N_KV_HEADS, HEAD_DIM)
    if rope is not None:
        q = _apply_rope(q, *rope)
        k = jnp.concatenate([ext_kv[0], _apply_rope(k, *rope)], axis=1)
        v = jnp.concatenate([ext_kv[1], v], axis=1)
    attn = _block_attention(q, k, v)
    conv = gate_b * _short_conv(gate_c * hv, w_conv)
    return jnp.concatenate([attn, conv], axis=-1) @ w_out


def _chunk_gmlp(h, w_in, g_v, b_v, w_s, b_s, w_out):
    b, n, _ = h.shape
    u, v = jnp.split(jax.nn.gelu(h @ w_in), 2, axis=-1)
    v = _layer_norm(v, g_v, b_v).reshape(b, n // CHUNK, CHUNK, GM_GROUPS, GM_GC)
    s = jnp.einsum("gpq,bcqgd->bcpgd", w_s, v) + b_s.T[:, :, None]
    return (u * s.reshape(b, n, GM_W)) @ w_out


def _swiglu(h, w_gate, w_up, w_down):
    return (jax.nn.silu(h @ w_gate) * (h @ w_up)) @ w_down


def _moe_swiglu(h, w_router, b_router, w_gate, w_up, w_down):
    logits = (h @ w_router).astype(jnp.float32) + b_router.astype(jnp.float32)
    top_v, top_i = lax.top_k(logits, TOP_K)
    wts = jax.nn.softmax(top_v, axis=-1)
    gates = jnp.sum(jax.nn.one_hot(top_i, N_EXPERTS, dtype=jnp.float32) * wts[..., None], axis=-2).astype(h.dtype)
    out = jnp.zeros_like(h)
    for e in range(N_EXPERTS):
        out = out + gates[..., e:e + 1] * _swiglu(h, w_gate[e], w_up[e], w_down[e])
    return out


def setup_inputs(seed: int = 0) -> dict:
    key = jax.random.key(seed)
    ks = list(jax.random.split(key, 48))
    D = D_MODEL
    ne, no = N_EVEN, N_ODD

    def nrm(shape, scale):
        return jax.random.normal(ks.pop(), shape, jnp.float32) * scale

    def gain(shape):
        return 1.0 + nrm(shape, 0.1)

    return {
        "x": nrm((BATCH, SEQ, D), 1.0),
        "c": nrm((BATCH, D), 1.0),
        "ctx": nrm((BATCH, CTX_LEN, D), 1.0),
        "c_ctx": nrm((D,), 1.0),
        "e_w_mod": nrm((ne, D, 6 * D), 0.5 * D ** -0.5),
        "e_b_mod": nrm((ne, 6 * D), 0.02),
        "e_g_pre_mix": gain((ne, D)),
        "e_g_post_mix": gain((ne, D)),
        "e_w_in": nrm((ne, D, IN_W), D ** -0.5),
        "e_g_q": gain((ne, HEAD_DIM)),
        "e_g_k": gain((ne, HEAD_DIM)),
        "e_w_conv": nrm((ne, CONV_TAPS, CONV_W), 0.5),
        "e_w_out": nrm((ne, ATTN_W + CONV_W, D), (ATTN_W + CONV_W) ** -0.5),
        "e_g_pre_ffn": gain((ne, D)),
        "e_g_post_ffn": gain((ne, D)),
        "e_w_gate": nrm((ne, D, D_FF), D ** -0.5),
        "e_w_up": nrm((ne, D, D_FF), D ** -0.5),
        "e_w_down": nrm((ne, D_FF, D), D_FF ** -0.5),
        "o_w_mod": nrm((no, D, 6 * D), 0.5 * D ** -0.5),
        "o_b_mod": nrm((no, 6 * D), 0.02),
        "o_g_pre_mix": gain((no, D)),
        "o_g_post_mix": gain((no, D)),
        "o_w_in": nrm((no, D, 2 * GM_W), D ** -0.5),
        "o_g_v": gain((no, GM_W)),
        "o_b_v": nrm((no, GM_W), 0.02),
        "o_w_s": nrm((no, GM_GROUPS, CHUNK, CHUNK), 0.5 * CHUNK ** -0.5),
        "o_b_s": gain((no, GM_GROUPS, CHUNK)),
        "o_w_out": nrm((no, GM_W, D), GM_W ** -0.5),
        "o_g_pre_ffn": gain((no, D)),
        "o_g_post_ffn": gain((no, D)),
        "o_w_router": nrm((no, D, N_EXPERTS), D ** -0.5),
        "o_b_router": nrm((no, N_EXPERTS), 0.01),
        "o_w_gate": nrm((no, N_EXPERTS, D, EXPERT_FF), D ** -0.5),
        "o_w_up": nrm((no, N_EXPERTS, D, EXPERT_FF), D ** -0.5),
        "o_w_down": nrm((no, N_EXPERTS, EXPERT_FF, D), EXPERT_FF ** -0.5),
    }


def reference(x, c, ctx, c_ctx,
              e_w_mod, e_b_mod, e_g_pre_mix, e_g_post_mix, e_w_in, e_g_q, e_g_k, e_w_conv, e_w_out,
              e_g_pre_ffn, e_g_post_ffn, e_w_gate, e_w_up, e_w_down,
              o_w_mod, o_b_mod, o_g_pre_mix, o_g_post_mix, o_w_in, o_g_v, o_b_v, o_w_s, o_b_s, o_w_out,
              o_g_pre_ffn, o_g_post_ffn, o_w_router, o_b_router, o_w_gate, o_w_up, o_w_down):
    rope = _axial_rope(x.shape[1])
    for layer in range(DEPTH):
        i = layer // 2
        ctx_read_later = any(l % 2 == 0 for l in range(layer + 1, DEPTH))
        if layer % 2 == 0:
            mx = _modulation(c[:, None, :], e_w_mod[i], e_b_mod[i])
            mc = _modulation(c_ctx, e_w_mod[i], e_b_mod[i])
            hc = _pre(ctx, e_g_pre_mix[i], mc[0], mc[1])
            ctx_kv = _context_kv(hc, e_w_in[i], e_g_k[i])
            hx = _pre(x, e_g_pre_mix[i], mx[0], mx[1])
            y = _attn_conv_mixer(hx, e_w_in[i], e_g_q[i], e_g_k[i], e_w_conv[i], e_w_out[i], rope, ctx_kv)
            x = x + mx[2] * _rms_norm(y, e_g_post_mix[i])
            f = _swiglu(_pre(x, e_g_pre_ffn[i], mx[3], mx[4]), e_w_gate[i], e_w_up[i], e_w_down[i])
            x = x + mx[5] * _rms_norm(f, e_g_post_ffn[i])
            if ctx_read_later:
                yc = _attn_conv_mixer(hc, e_w_in[i], e_g_q[i], e_g_k[i], e_w_conv[i], e_w_out[i], None, None)
                ctx = ctx + mc[2] * _rms_norm(yc, e_g_post_mix[i])
                fc = _swiglu(_pre(ctx, e_g_pre_ffn[i], mc[3], mc[4]), e_w_gate[i], e_w_up[i], e_w_down[i])
                ctx = ctx + mc[5] * _rms_norm(fc, e_g_post_ffn[i])
        else:
            mx = _modulation(c[:, None, :], o_w_mod[i], o_b_mod[i])
            hx = _pre(x, o_g_pre_mix[i], mx[0], mx[1])
            y = _chunk_gmlp(hx, o_w_in[i], o_g_v[i], o_b_v[i], o_w_s[i], o_b_s[i], o_w_out[i])
            x = x + mx[2] * _rms_norm(y, o_g_post_mix[i])
            f = _moe_swiglu(_pre(x, o_g_pre_ffn[i], mx[3], mx[4]), o_w_router[i], o_b_router[i],
                            o_w_gate[i], o_w_up[i], o_w_down[i])
            x = x + mx[5] * _rms_norm(f, o_g_post_ffn[i])
            if ctx_read_later:
                mc = _modulation(c_ctx, o_w_mod[i], o_b_mod[i])
                hc = _pre(ctx, o_g_pre_mix[i], mc[0], mc[1])
                yc = _chunk_gmlp(hc, o_w_in[i], o_g_v[i], o_b_v[i], o_w_s[i], o_b_s[i], o_w_out[i])
                ctx = ctx + mc[2] * _rms_norm(yc, o_g_post_mix[i])
                fc = _moe_swiglu(_pre(ctx, o_g_pre_ffn[i], mc[3], mc[4]), o_w_router[i], o_b_router[i],
                                 o_w_gate[i], o_w_up[i], o_w_down[i])
                ctx = ctx + mc[5] * _rms_norm(fc, o_g_post_ffn[i])
    return x
```

```python
import functools
import math

import jax
import jax.numpy as jnp
from jax import lax
from jax.experimental import pallas as pl
from jax.experimental.pallas import tpu as pltpu

F32 = jnp.float32
BF16 = jnp.bfloat16

EPS = 1e-6
GRID_W = 64
N_HEADS = 8
N_KV_HEADS = 2
HEAD_DIM = 64
ATTN_W = N_HEADS * HEAD_DIM
KV_W = N_KV_HEADS * HEAD_DIM
ROPE_THETA = 10000.0
CHUNK = 128
GM_GROUPS = 8
N_EXPERTS = 8
LANES = 128
LOG2E = 1.4426950408889634
NEG_BIG = -1e30

VMEM_LIMIT_BYTES = 56 * 1024 * 1024

TOKEN_TILE = 512
ATTN_Q_TILE = 256
FF_CHUNK_DENSE = 1408
FF_CHUNK_EXPERT = 512
EXPERT_TILE = 1024
ROUTE_TILE = 256


def _cparams(sem):
    return pltpu.CompilerParams(dimension_semantics=sem, vmem_limit_bytes=VMEM_LIMIT_BYTES)


def _const_spec(shape):
    n = len(shape)
    return pl.BlockSpec(shape, lambda *_: (0,) * n, pipeline_mode=pl.Buffered(1))


def _rms(x, g):
    return x * lax.rsqrt(jnp.mean(x * x, axis=-1, keepdims=True) + EPS) * g


def _pre(x, g, shift, scale):
    return _rms(x, g) * (1.0 + scale) + shift


def _bdot(a, b):
    return jnp.dot(a, b, preferred_element_type=F32)


def _mod_kernel(c_ref, w_ref, b_ref, o_ref):
    a = jax.nn.silu(c_ref[...])
    o_ref[...] = jnp.dot(a, w_ref[...], precision=lax.Precision.HIGHEST,
                         preferred_element_type=F32) + b_ref[...]


def _modulation(cond8, w_mod, b_mod):
    d = cond8.shape[1]
    out = pl.pallas_call(
        _mod_kernel,
        grid=(6,),
        in_specs=[pl.BlockSpec((8, d), lambda j: (0, 0)),
                  pl.BlockSpec((d, d), lambda j: (0, j)),
                  pl.BlockSpec((1, d), lambda j: (0, j))],
        out_specs=pl.BlockSpec((8, d), lambda j: (0, j)),
        out_shape=jax.ShapeDtypeStruct((8, 6 * d), F32),
        compiler_params=_cparams(("arbitrary",)),
        name="modulation",
    )(cond8, w_mod, b_mod.reshape(1, 6 * d))
    return out.reshape(8, 6, d)


def _rope128(t, cos, sa, sb):
    return t * cos + pltpu.roll(t, 96, 1) * sa + pltpu.roll(t, 32, 1) * sb


def _inproj_kernel(x_ref, mod_ref, gpre_ref, w_ref, gq_ref, gk_ref, eq_ref, ek_ref,
                   cos_ref, sa_ref, sb_ref, qt_ref, k_ref, vt_ref, bg_ref, z_ref, *, q_scale):
    mod = mod_ref[...]
    h = _pre(x_ref[...], gpre_ref[...], mod[0:1], mod[1:2]).astype(BF16)
    cos, sa, sb = cos_ref[...], sa_ref[...], sb_ref[...]

    q = _bdot(h, w_ref[:, 0:ATTN_W])
    ms = _bdot((q * q).astype(BF16), eq_ref[...])
    qn = q * lax.rsqrt(ms + EPS) * gq_ref[...]
    qr = jnp.concatenate(
        [_rope128(qn[:, LANES * j:LANES * (j + 1)], cos, sa, sb) for j in range(ATTN_W // LANES)], axis=1)
    qt_ref[...] = (qr * q_scale).T.astype(BF16)

    k = _bdot(h, w_ref[:, ATTN_W:ATTN_W + KV_W])
    msk = _bdot((k * k).astype(BF16), ek_ref[...])
    kn = k * lax.rsqrt(msk + EPS) * gk_ref[...]
    k_ref[...] = _rope128(kn, cos, sa, sb).astype(BF16)

    v = _bdot(h, w_ref[:, ATTN_W + KV_W:ATTN_W + 2 * KV_W])
    vt_ref[...] = v.T.astype(BF16)

    o = ATTN_W + 2 * KV_W
    cw = (w_ref.shape[1] - o) // 3
    bg_ref[...] = _bdot(h, w_ref[:, o:o + cw]).astype(BF16)
    cg = _bdot(h, w_ref[:, o + cw:o + 2 * cw])
    hv = _bdot(h, w_ref[:, o + 2 * cw:o + 3 * cw])
    z_ref[...] = (cg * hv).astype(BF16)


def _inproj(x, mods, mod_row, g_pre, w_in, gq, gk, eq, ek, cos, sa, sb, tm):
    b, s, d = x.shape
    nt = s // tm
    cw = (w_in.shape[1] - ATTN_W - 2 * KV_W) // 3
    row = (lambda bi: bi) if mod_row is None else (lambda bi: mod_row)
    tok = lambda bi, i: (bi, i, 0)
    return pl.pallas_call(
        functools.partial(_inproj_kernel, q_scale=HEAD_DIM ** -0.5 * LOG2E),
        grid=(b, nt),
        in_specs=[pl.BlockSpec((None, tm, d), tok),
                  pl.BlockSpec((None, 6, d), lambda bi, i: (row(bi), 0, 0)),
                  _const_spec((1, d)),
                  _const_spec(w_in.shape),
                  _const_spec((1, ATTN_W)),
                  _const_spec((1, KV_W)),
                  _const_spec((ATTN_W, ATTN_W)),
                  _const_spec((KV_W, KV_W)),
                  pl.BlockSpec((tm, LANES), lambda bi, i: (i, 0)),
                  pl.BlockSpec((tm, LANES), lambda bi, i: (i, 0)),
                  pl.BlockSpec((tm, LANES), lambda bi, i: (i, 0))],
        out_specs=[pl.BlockSpec((None, ATTN_W, tm), lambda bi, i: (bi, 0, i)),
                   pl.BlockSpec((None, tm, KV_W), tok),
                   pl.BlockSpec((None, None, KV_W, tm), lambda bi, i: (bi, i, 0, 0)),
                   pl.BlockSpec((None, tm, cw), tok),
                   pl.BlockSpec((None, tm, cw), tok)],
        out_shape=[jax.ShapeDtypeStruct((b, ATTN_W, s), BF16),
                   jax.ShapeDtypeStruct((b, s, KV_W), BF16),
                   jax.ShapeDtypeStruct((b, nt, KV_W, tm), BF16),
                   jax.ShapeDtypeStruct((b, s, cw), BF16),
                   jax.ShapeDtypeStruct((b, s, cw), BF16)],
        compiler_params=_cparams(("parallel", "parallel")),
        name="inproj",
    )(x, mods, g_pre, w_in, gq, gk, eq, ek, cos, sa, sb)


def _attn_kernel(qt_ref, k_ref, vt_ref, kc_ref, vct_ref, o_ref, ot_ref):
    n_kb = k_ref.shape[0]
    group = N_HEADS // N_KV_HEADS

    def block(kb, vtb, qp, m, acc):
        s = _bdot(kb, qp)
        mb = jnp.max(s, axis=0, keepdims=True)
        m_new = mb if m is None else jnp.maximum(m, mb)
        p = jnp.exp2(s - m_new).astype(BF16)
        va = jnp.concatenate([vtb, jnp.ones((16, vtb.shape[1]), BF16)], axis=0)
        pv = _bdot(va, p)
        if m is None:
            return m_new, pv
        return m_new, acc * jnp.exp2(m - m_new) + pv

    for h in range(N_HEADS):
        g = h // group
        lo, hi = HEAD_DIM * g, HEAD_DIM * (g + 1)
        qh = qt_ref[HEAD_DIM * h:HEAD_DIM * (h + 1), :]
        zq = jnp.zeros_like(qh)
        qp = jnp.concatenate([qh, zq] if g == 0 else [zq, qh], axis=0)
        carry = block(kc_ref[0], vct_ref[0, lo:hi, :], qp, None, None)

        def body(i, c, qp=qp, lo=lo, hi=hi):
            return block(k_ref[i], vt_ref[i, lo:hi, :], qp, c[0], c[1])

        _, acc = lax.fori_loop(0, n_kb, body, carry)
        ot_ref[HEAD_DIM * h:HEAD_DIM * (h + 1), :] = acc[0:HEAD_DIM] / acc[HEAD_DIM:HEAD_DIM + 1]
    o_ref[...] = ot_ref[...].T.astype(BF16)


def _attention(qt, k4, vt4, kc4, vct4, tq):
    b, _, s = qt.shape
    _, n_kb, tk, _ = k4.shape
    ctx = kc4.shape[2]
    return pl.pallas_call(
        _attn_kernel,
        grid=(b, s // tq),
        in_specs=[pl.BlockSpec((None, ATTN_W, tq), lambda bi, i: (bi, 0, i)),
                  pl.BlockSpec((None, n_kb, tk, KV_W), lambda bi, i: (bi, 0, 0, 0)),
                  pl.BlockSpec((None, n_kb, KV_W, tk), lambda bi, i: (bi, 0, 0, 0)),
                  pl.BlockSpec((None, 1, ctx, KV_W), lambda bi, i: (bi, 0, 0, 0)),
                  pl.BlockSpec((None, 1, KV_W, ctx), lambda bi, i: (bi, 0, 0, 0))],
        out_specs=pl.BlockSpec((None, tq, ATTN_W), lambda bi, i: (bi, i, 0)),
        out_shape=jax.ShapeDtypeStruct((b, s, ATTN_W), BF16),
        scratch_shapes=[pltpu.VMEM((ATTN_W, tq), F32)],
        compiler_params=_cparams(("parallel", "parallel")),
        name="attention",
    )(qt, k4, vt4, kc4, vct4)


def _outproj_kernel(x_ref, mod_ref, attn_ref, bg_ref, z_ref, zp_ref, zn_ref, wc_ref, wo_ref, gpost_ref,
                    o_ref):
    i = pl.program_id(1)
    nt = pl.num_programs(1)
    tm = z_ref.shape[0]
    z = z_ref[...].astype(F32)
    halo = zp_ref.shape[0]
    zprev = zp_ref[...].astype(F32)[halo - 1:halo, :] * (i > 0).astype(F32)
    znext = zn_ref[...].astype(F32)[0:1, :] * (i < nt - 1).astype(F32)
    row = lax.broadcasted_iota(jnp.int32, z.shape, 0)
    zm1 = jnp.where(row == 0, zprev, pltpu.roll(z, 1, 0))
    zp1 = jnp.where(row == tm - 1, znext, pltpu.roll(z, tm - 1, 0))
    wc = wc_ref[...]
    conv = bg_ref[...].astype(F32) * (wc[0:1] * zm1 + wc[1:2] * z + wc[2:3] * zp1)
    y = _bdot(attn_ref[...], wo_ref[0:ATTN_W, :]) + _bdot(conv.astype(BF16), wo_ref[ATTN_W:, :])
    mod = mod_ref[...]
    o_ref[...] = x_ref[...] + mod[2:3] * _rms(y, gpost_ref[...])


def _outproj(x, mods, attn, bg, z, w_conv, w_out, g_post, tm):
    b, s, d = x.shape
    cw = z.shape[2]
    halo = 16
    r = tm // halo
    last = s // halo - 1
    tok = lambda bi, i: (bi, i, 0)
    return pl.pallas_call(
        _outproj_kernel,
        grid=(b, s // tm),
        in_specs=[pl.BlockSpec((None, tm, d), tok),
                  pl.BlockSpec((None, 6, d), lambda bi, i: (bi, 0, 0)),
                  pl.BlockSpec((None, tm, ATTN_W), tok),
                  pl.BlockSpec((None, tm, cw), tok),
                  pl.BlockSpec((None, tm, cw), tok),
                  pl.BlockSpec((None, halo, cw), lambda bi, i: (bi, jnp.maximum(i * r - 1, 0), 0)),
                  pl.BlockSpec((None, halo, cw), lambda bi, i: (bi, jnp.minimum((i + 1) * r, last), 0)),
                  _const_spec(w_conv.shape),
                  _const_spec(w_out.shape),
                  _const_spec((1, d))],
        out_specs=pl.BlockSpec((None, tm, d), tok),
        out_shape=jax.ShapeDtypeStruct((b, s, d), F32),
        compiler_params=_cparams(("parallel", "parallel")),
        name="outproj",
    )(x, mods, attn, bg, z, z, z, w_conv, w_out, g_post)


def _swiglu_kernel(x_ref, mod_ref, gpre_ref, wg_ref, wu_ref, wd_ref, gpost_ref, o_ref, *, chunk):
    mod = mod_ref[...]
    x = x_ref[...]
    h = _pre(x, gpre_ref[...], mod[3:4], mod[4:5]).astype(BF16)
    acc = None
    for c in range(wg_ref.shape[1] // chunk):
        sl = slice(c * chunk, (c + 1) * chunk)
        a = (jax.nn.silu(_bdot(h, wg_ref[:, sl])) * _bdot(h, wu_ref[:, sl])).astype(BF16)
        part = _bdot(a, wd_ref[sl, :])
        acc = part if acc is None else acc + part
    o_ref[...] = x + mod[5:6] * _rms(acc, gpost_ref[...])


def _swiglu(x, mods, g_pre, w_gate, w_up, w_down, g_post, tm):
    b, s, d = x.shape
    tok = lambda bi, i: (bi, i, 0)
    return pl.pallas_call(
        functools.partial(_swiglu_kernel, chunk=FF_CHUNK_DENSE),
        grid=(b, s // tm),
        in_specs=[pl.BlockSpec((None, tm, d), tok),
                  pl.BlockSpec((None, 6, d), lambda bi, i: (bi, 0, 0)),
                  _const_spec((1, d)),
                  _const_spec(w_gate.shape),
                  _const_spec(w_up.shape),
                  _const_spec(w_down.shape),
                  _const_spec((1, d))],
        out_specs=pl.BlockSpec((None, tm, d), tok),
        out_shape=jax.ShapeDtypeStruct((b, s, d), F32),
        compiler_params=_cparams(("parallel", "parallel")),
        name="swiglu",
    )(x, mods, g_pre, w_gate, w_up, w_down, g_post)


def _gmlp_kernel(x_ref, mod_ref, gpre_ref, win_ref, gv_ref, bv_ref, ws_ref, bs_ref, wout_ref, gpost_ref,
                 gffn_ref, wr_ref, br_ref, tri_ref,
                 x3_ref, h3_ref, route_ref, cnt_ref, run_ref):
    first = jnp.logical_and(pl.program_id(0) == 0, pl.program_id(1) == 0)

    @pl.when(first)
    def _():
        run_ref[...] = jnp.zeros_like(run_ref)

    mod = mod_ref[...]
    x = x_ref[...]
    tm, d = x.shape
    h = _pre(x, gpre_ref[...], mod[0:1], mod[1:2]).astype(BF16)
    u = jax.nn.gelu(_bdot(h, win_ref[:, 0:d]))
    v = jax.nn.gelu(_bdot(h, win_ref[:, d:2 * d]))
    mu = jnp.mean(v, axis=-1, keepdims=True)
    vc = v - mu
    var = jnp.mean(vc * vc, axis=-1, keepdims=True)
    vn = (vc * lax.rsqrt(var + EPS) * gv_ref[...] + bv_ref[...]).astype(BF16)

    n_chunks = tm // CHUNK
    gc = d // GM_GROUPS
    mixed = []
    for g in range(GM_GROUPS):
        rhs = jnp.concatenate([vn[CHUNK * c:CHUNK * (c + 1), gc * g:gc * (g + 1)] for c in range(n_chunks)],
                              axis=1)
        bias = bs_ref[g]
        mixed.append(_bdot(ws_ref[g], rhs) + jnp.concatenate([bias] * n_chunks, axis=1))
    s = jnp.concatenate(
        [jnp.concatenate([mixed[g][:, gc * c:gc * (c + 1)] for g in range(GM_GROUPS)], axis=1)
         for c in range(n_chunks)], axis=0)
    y = _bdot((u * s).astype(BF16), wout_ref[...])
    x3 = x + mod[2:3] * _rms(y, gpost_ref[...])
    x3_ref[...] = x3

    h3 = _pre(x3, gffn_ref[...], mod[3:4], mod[4:5])
    h3_ref[...] = h3
    h_hi = h3.astype(BF16)
    h_lo = (h3 - h_hi.astype(F32)).astype(BF16)
    logits = (_bdot(h_hi, wr_ref[0]) + _bdot(h_lo, wr_ref[0]) + _bdot(h_hi, wr_ref[1])
              + br_ref[...])
    lane = lax.broadcasted_iota(jnp.int32, logits.shape, 1)
    m1 = jnp.max(logits, axis=1, keepdims=True)
    i1 = jnp.min(jnp.where(logits == m1, lane, LANES), axis=1, keepdims=True)
    rest = jnp.where(lane == i1, 2.0 * NEG_BIG, logits)
    m2 = jnp.max(rest, axis=1, keepdims=True)
    i2 = jnp.min(jnp.where(rest == m2, lane, LANES), axis=1, keepdims=True)
    e21 = jnp.exp(m2 - m1)
    w1 = 1.0 / (1.0 + e21)
    w2 = e21 / (1.0 + e21)
    hot1 = lane == i1
    hot2 = lane == i2
    onehot = jnp.where(jnp.logical_or(hot1, hot2), 1.0, 0.0)
    before = _bdot(tri_ref[...], onehot.astype(BF16)) + run_ref[...]
    r1 = jnp.sum(jnp.where(hot1, before, 0.0), axis=1, keepdims=True)
    r2 = jnp.sum(jnp.where(hot2, before, 0.0), axis=1, keepdims=True)
    run = run_ref[...] + jnp.sum(onehot, axis=0, keepdims=True)
    run_ref[...] = run
    cnt_ref[...] = jnp.broadcast_to(run, cnt_ref.shape)
    fields = (i1.astype(F32), i2.astype(F32), r1, r2, w1, w2)
    route = jnp.zeros(logits.shape, F32)
    for j, f in enumerate(fields):
        route = jnp.where(lane == j, f, route)
    route_ref[...] = route


def _gmlp_router(x, mods, g_pre, w_in, g_v, b_v, w_s, b_s_b, w_out, g_post, g_ffn, w_r, b_r, tm):
    b, s, d = x.shape
    tok = lambda bi, i: (bi, i, 0)
    tri = jnp.tril(jnp.ones((tm, tm), F32), -1).astype(BF16)
    return pl.pallas_call(
        _gmlp_kernel,
        grid=(b, s // tm),
        in_specs=[pl.BlockSpec((None, tm, d), tok),
                  pl.BlockSpec((None, 6, d), lambda bi, i: (bi, 0, 0)),
                  _const_spec((1, d)),
                  _const_spec(w_in.shape),
                  _const_spec((1, d)),
                  _const_spec((1, d)),
                  _const_spec(w_s.shape),
                  _const_spec(b_s_b.shape),
                  _const_spec(w_out.shape),
                  _const_spec((1, d)),
                  _const_spec((1, d)),
                  _const_spec(w_r.shape),
                  _const_spec((1, LANES)),
                  _const_spec((tm, tm))],
        out_specs=[pl.BlockSpec((None, tm, d), tok),
                   pl.BlockSpec((None, tm, d), tok),
                   pl.BlockSpec((None, tm, LANES), tok),
                   pl.BlockSpec((8, LANES), lambda bi, i: (0, 0))],
        out_shape=[jax.ShapeDtypeStruct((b, s, d), F32),
                   jax.ShapeDtypeStruct((b, s, d), F32),
                   jax.ShapeDtypeStruct((b, s, LANES), F32),
                   jax.ShapeDtypeStruct((8, LANES), F32)],
        scratch_shapes=[pltpu.VMEM((1, LANES), F32)],
        compiler_params=_cparams(("arbitrary", "arbitrary")),
        name="gmlp_router",
    )(x, mods, g_pre, w_in, g_v, b_v, w_s, b_s_b, w_out, g_post, g_ffn, w_r, b_r, tri)


def _dispatch_kernel(slot_ref, h_ref, xs_in_ref, xs_ref, sem):
    del xs_in_ref
    tm = h_ref.shape[0]

    def row_copy(r, k):
        dst = slot_ref[0, 0, 2 * r + k]
        return pltpu.make_async_copy(h_ref.at[pl.ds(r, 1), :], xs_ref.at[pl.ds(dst, 1), :], sem)

    def issue(r, _):
        row_copy(r, 0).start()
        row_copy(r, 1).start()
        return 0

    lax.fori_loop(0, tm, issue, 0)

    def drain(r, _):
        row_copy(r, 0).wait()
        row_copy(r, 1).wait()
        return 0

    lax.fori_loop(0, tm, drain, 0)


def _dispatch(h, slots, n_slots, tm):
    n, d = h.shape
    nt = n // tm
    return pl.pallas_call(
        _dispatch_kernel,
        grid=(nt,),
        in_specs=[pl.BlockSpec((1, 1, 2 * tm), lambda i: (i, 0, 0), memory_space=pltpu.SMEM),
                  pl.BlockSpec((tm, d), lambda i: (i, 0)),
                  pl.BlockSpec(memory_space=pl.ANY)],
        out_specs=pl.BlockSpec(memory_space=pl.ANY),
        out_shape=jax.ShapeDtypeStruct((n_slots, d), F32),
        scratch_shapes=[pltpu.SemaphoreType.DMA(())],
        input_output_aliases={2: 0},
        compiler_params=_cparams(("arbitrary",)),
        name="dispatch",
    )(slots.reshape(nt, 1, 2 * tm), h, jnp.zeros((n_slots, d), F32))


def _expert_kernel(te_ref, na_ref, xs_ref, wg_ref, wu_ref, wd_ref, ys_ref, xb_ref):
    t = pl.program_id(0)
    c = pl.program_id(1)
    active = t < na_ref[0]

    @pl.when(jnp.logical_and(active, c == 0))
    def _():
        xb_ref[...] = xs_ref[...].astype(BF16)

    @pl.when(active)
    def _():
        xb = xb_ref[...]
        a = (jax.nn.silu(_bdot(xb, wg_ref[...])) * _bdot(xb, wu_ref[...])).astype(BF16)
        part = _bdot(a, wd_ref[...])

        @pl.when(c == 0)
        def _():
            ys_ref[...] = part

        @pl.when(c > 0)
        def _():
            ys_ref[...] += part

    @pl.when(jnp.logical_and(jnp.logical_not(active), c == 0))
    def _():
        ys_ref[...] = jnp.zeros_like(ys_ref)


def _experts(tile_expert, n_active, xs, w_gate, w_up, w_down, tm, chunk):
    n_slots, d = xs.shape
    ff = w_gate.shape[2]
    n_tiles = n_slots // tm

    def live(t, na):
        return jnp.minimum(t, na[0] - 1)

    grid_spec = pltpu.PrefetchScalarGridSpec(
        num_scalar_prefetch=2,
        grid=(n_tiles, ff // chunk),
        in_specs=[pl.BlockSpec((tm, d), lambda t, c, te, na: (live(t, na), 0)),
                  pl.BlockSpec((None, d, chunk), lambda t, c, te, na: (te[live(t, na)], 0, jnp.where(t < na[0], c, ff // chunk - 1))),
                  pl.BlockSpec((None, d, chunk), lambda t, c, te, na: (te[live(t, na)], 0, jnp.where(t < na[0], c, ff // chunk - 1))),
                  pl.BlockSpec((None, chunk, d), lambda t, c, te, na: (te[live(t, na)], jnp.where(t < na[0], c, ff // chunk - 1), 0))],
        out_specs=pl.BlockSpec((tm, d), lambda t, c, te, na: (t, 0)),
        scratch_shapes=[pltpu.VMEM((tm, d), BF16)],
    )
    return pl.pallas_call(
        _expert_kernel,
        grid_spec=grid_spec,
        out_shape=jax.ShapeDtypeStruct((n_slots, d), F32),
        compiler_params=_cparams(("arbitrary", "arbitrary")),
        name="experts",
    )(tile_expert, n_active, xs, w_gate, w_up, w_down)


def _combine_kernel(slot_ref, x_ref, mod_ref, route_ref, gpost_ref, ys_ref, o_ref, buf_ref, sem):
    tm = x_ref.shape[0]

    def row_copy(r, k):
        src = slot_ref[0, 0, 2 * r + k]
        return pltpu.make_async_copy(ys_ref.at[pl.ds(src, 1), :], buf_ref.at[k, pl.ds(r, 1), :], sem)

    def issue(r, _):
        row_copy(r, 0).start()
        row_copy(r, 1).start()
        return 0

    lax.fori_loop(0, tm, issue, 0)

    def drain(r, _):
        row_copy(r, 0).wait()
        row_copy(r, 1).wait()
        return 0

    lax.fori_loop(0, tm, drain, 0)

    route = route_ref[...]
    f = route[:, 4:5] * buf_ref[0] + route[:, 5:6] * buf_ref[1]
    mod = mod_ref[...]
    o_ref[...] = x_ref[...] + mod[5:6] * _rms(f, gpost_ref[...])


def _combine(x, mods, route, g_post, ys, slots, seq, tm):
    n, d = x.shape
    nt = n // tm
    per_batch = seq // tm
    return pl.pallas_call(
        _combine_kernel,
        grid=(nt,),
        in_specs=[pl.BlockSpec((1, 1, 2 * tm), lambda i: (i, 0, 0), memory_space=pltpu.SMEM),
                  pl.BlockSpec((tm, d), lambda i: (i, 0)),
                  pl.BlockSpec((None, 6, d), lambda i: (i // per_batch, 0, 0)),
                  pl.BlockSpec((tm, LANES), lambda i: (i, 0)),
                  _const_spec((1, d)),
                  pl.BlockSpec(memory_space=pl.ANY)],
        out_specs=pl.BlockSpec((tm, d), lambda i: (i, 0)),
        out_shape=jax.ShapeDtypeStruct((n, d), F32),
        scratch_shapes=[pltpu.VMEM((2, tm, d), F32), pltpu.SemaphoreType.DMA(())],
        compiler_params=_cparams(("arbitrary",)),
        name="combine",
    )(slots.reshape(nt, 1, 2 * tm), x, mods, route, g_post, ys)


def _rope_tables(n):
    axis_dim = HEAD_DIM // 2
    pos = jnp.arange(n, dtype=jnp.int32)
    r = (pos // GRID_W).astype(F32)[:, None]
    col = (pos % GRID_W).astype(F32)[:, None]
    inv = 1.0 / (ROPE_THETA ** (jnp.arange(0, axis_dim, 2, dtype=F32) / axis_dim))
    ang = jnp.concatenate([r * inv, col * inv], axis=-1)
    cos, sin = jnp.cos(ang), jnp.sin(ang)
    zero = jnp.zeros_like(sin)
    reps = LANES // HEAD_DIM
    return (jnp.tile(jnp.concatenate([cos, cos], -1), (1, reps)),
            jnp.tile(jnp.concatenate([-sin, zero], -1), (1, reps)),
            jnp.tile(jnp.concatenate([zero, sin], -1), (1, reps)))


def _head_mean_matrix(width):
    idx = jnp.arange(width) // HEAD_DIM
    return jnp.where(idx[:, None] == idx[None, :], 1.0 / HEAD_DIM, 0.0).astype(BF16)


def kernel(x, c, ctx, c_ctx, e_w_mod, e_b_mod, e_g_pre_mix, e_g_post_mix, e_w_in, e_g_q, e_g_k, e_w_conv, e_w_out, e_g_pre_ffn, e_g_post_ffn, e_w_gate, e_w_up, e_w_down, o_w_mod, o_b_mod, o_g_pre_mix, o_g_post_mix, o_w_in, o_g_v, o_b_v, o_w_s, o_b_s, o_w_out, o_g_pre_ffn, o_g_post_ffn, o_w_router, o_b_router, o_w_gate, o_w_up, o_w_down):
    b, s, d = x.shape
    n_ctx = ctx.shape[1]
    n = b * s
    tm = min(TOKEN_TILE, s)
    tq = min(ATTN_Q_TILE, s)
    assert b + 1 <= 8 and s % tm == 0 and s % tq == 0 and tm % CHUNK == 0 and n_ctx % 16 == 0
    assert e_w_mod.shape[0] == 1 and o_w_mod.shape[0] == 1
    row = lambda g: g.reshape(1, -1)

    cond8 = jnp.zeros((8, d), F32).at[:b].set(c).at[b].set(c_ctx)
    mods_e = _modulation(cond8, e_w_mod[0], e_b_mod[0])
    mods_o = _modulation(cond8, o_w_mod[0], o_b_mod[0])

    w_in = e_w_in[0].astype(BF16)
    gq = jnp.tile(e_g_q[0], N_HEADS).reshape(1, ATTN_W)
    gk = jnp.tile(e_g_k[0], N_KV_HEADS).reshape(1, KV_W)
    eq, ek = _head_mean_matrix(ATTN_W), _head_mean_matrix(KV_W)
    cos, sa, sb = _rope_tables(s)
    qt, k, vt4, bg, z = _inproj(x, mods_e, None, row(e_g_pre_mix[0]), w_in, gq, gk, eq, ek, cos, sa, sb, tm)
    ones = jnp.ones((n_ctx, LANES), F32)
    zeros = jnp.zeros((n_ctx, LANES), F32)
    _, kc, vct4, _, _ = _inproj(ctx, mods_e, b, row(e_g_pre_mix[0]), w_in, gq, gk, eq, ek, ones, zeros, zeros,
                                n_ctx)
    attn = _attention(qt, k.reshape(b, s // tm, tm, KV_W), vt4, kc.reshape(b, 1, n_ctx, KV_W), vct4, tq)
    x1 = _outproj(x, mods_e, attn, bg, z, e_w_conv[0], e_w_out[0].astype(BF16), row(e_g_post_mix[0]), tm)

    x2 = _swiglu(x1, mods_e, row(e_g_pre_ffn[0]), e_w_gate[0].astype(BF16), e_w_up[0].astype(BF16),
                 e_w_down[0].astype(BF16), row(e_g_post_ffn[0]), tm)

    b_s_b = jnp.broadcast_to(o_b_s[0][:, :, None], (GM_GROUPS, CHUNK, d // GM_GROUPS))
    w_r32 = jnp.zeros((d, LANES), F32).at[:, :N_EXPERTS].set(o_w_router[0])
    w_r_hi = w_r32.astype(BF16)
    w_r = jnp.stack([w_r_hi, (w_r32 - w_r_hi.astype(F32)).astype(BF16)])
    b_r = jnp.full((1, LANES), NEG_BIG, F32).at[0, :N_EXPERTS].set(o_b_router[0])
    x3, h3, route, counts = _gmlp_router(
        x2, mods_o, row(o_g_pre_mix[0]), o_w_in[0].astype(BF16), row(o_g_v[0]), row(o_b_v[0]),
        o_w_s[0].astype(BF16), b_s_b, o_w_out[0].astype(BF16), row(o_g_post_mix[0]), row(o_g_pre_ffn[0]),
        w_r, b_r, tm)

    te_rows = EXPERT_TILE
    n_tiles = -(-(2 * n + N_EXPERTS * (te_rows - 1)) // te_rows)
    n_slots = n_tiles * te_rows
    cnt = counts[0, :N_EXPERTS].astype(jnp.int32)
    tiles_per = (cnt + te_rows - 1) // te_rows
    tile_end = jnp.cumsum(tiles_per)
    base = (tile_end - tiles_per) * te_rows
    route2 = route.reshape(n, LANES)
    e12 = route2[:, 0:2].astype(jnp.int32)
    r12 = route2[:, 2:4].astype(jnp.int32)
    slots = (base[e12] + r12).reshape(-1)
    tile_expert = jnp.minimum(
        jnp.sum(jnp.arange(n_tiles, dtype=jnp.int32)[:, None] >= tile_end[None, :], axis=1),
        N_EXPERTS - 1).astype(jnp.int32)
    n_active = tile_end[-1:].astype(jnp.int32)

    rt = min(ROUTE_TILE, s)
    xs = _dispatch(h3.reshape(n, d), slots, n_slots, rt)
    ys = _experts(tile_expert, n_active, xs, o_w_gate[0].astype(BF16), o_w_up[0].astype(BF16),
                  o_w_down[0].astype(BF16), te_rows, FF_CHUNK_EXPERT)
    out = _combine(x3.reshape(n, d), mods_o, route2, row(o_g_post_ffn[0]), ys, slots, s, rt)
    return out.reshape(b, s, d)
```

```python
import functools
import math

import jax
import jax.numpy as jnp
from jax import lax
from jax.experimental import pallas as pl
from jax.experimental.pallas import tpu as pltpu

F32 = jnp.float32
BF16 = jnp.bfloat16

EPS = 1e-6
GRID_W = 64
N_HEADS = 8
N_KV_HEADS = 2
HEAD_DIM = 64
ATTN_W = N_HEADS * HEAD_DIM
KV_W = N_KV_HEADS * HEAD_DIM
ROPE_THETA = 10000.0
CHUNK = 128
GM_GROUPS = 8
N_EXPERTS = 8
LANES = 128
LOG2E = 1.4426950408889634
NEG_BIG = -1e30

VMEM_LIMIT_BYTES = 56 * 1024 * 1024

TOKEN_TILE = 512
ATTN_Q_TILE = 512
FF_CHUNK_DENSE = 1408
FF_CHUNK_EXPERT = 512
EXPERT_TILE = 1024
ROUTE_TILE = 256
DMA_UNROLL = 256


def _cparams(sem):
    return pltpu.CompilerParams(dimension_semantics=sem, vmem_limit_bytes=VMEM_LIMIT_BYTES)


def _const_spec(shape):
    n = len(shape)
    return pl.BlockSpec(shape, lambda *_: (0,) * n, pipeline_mode=pl.Buffered(1))


def _rms(x, g):
    return x * lax.rsqrt(jnp.mean(x * x, axis=-1, keepdims=True) + EPS) * g


def _pre(x, g, shift, scale):
    return _rms(x, g) * (1.0 + scale) + shift


def _bdot(a, b):
    return jnp.dot(a, b, preferred_element_type=F32)


def _mod_kernel(c_ref, w_ref, b_ref, o_ref):
    a = jax.nn.silu(c_ref[...])
    o_ref[...] = jnp.dot(a, w_ref[...], precision=lax.Precision.HIGHEST,
                         preferred_element_type=F32) + b_ref[...]


def _modulation(cond8, w_mod, b_mod):
    d = cond8.shape[1]
    out = pl.pallas_call(
        _mod_kernel,
        grid=(6,),
        in_specs=[pl.BlockSpec((8, d), lambda j: (0, 0)),
                  pl.BlockSpec((d, d), lambda j: (0, j)),
                  pl.BlockSpec((1, d), lambda j: (0, j))],
        out_specs=pl.BlockSpec((8, d), lambda j: (0, j)),
        out_shape=jax.ShapeDtypeStruct((8, 6 * d), F32),
        compiler_params=_cparams(("arbitrary",)),
        name="modulation",
    )(cond8, w_mod, b_mod.reshape(1, 6 * d))
    return out.reshape(8, 6, d)


def _rope128(t, cos, sa, sb):
    return t * cos + pltpu.roll(t, 96, 1) * sa + pltpu.roll(t, 32, 1) * sb


def _inproj_kernel(x_ref, mod_ref, gpre_ref, w_ref, gq_ref, gk_ref, eq_ref, ek_ref,
                   cos_ref, sa_ref, sb_ref, qt_ref, k_ref, vt_ref, bg_ref, z_ref, *, q_scale):
    mod = mod_ref[...]
    h = _pre(x_ref[...], gpre_ref[...], mod[0:1], mod[1:2]).astype(BF16)
    cos, sa, sb = cos_ref[...], sa_ref[...], sb_ref[...]

    q = _bdot(h, w_ref[:, 0:ATTN_W])
    ms = _bdot((q * q).astype(BF16), eq_ref[...])
    qn = q * lax.rsqrt(ms + EPS) * gq_ref[...]
    qr = jnp.concatenate(
        [_rope128(qn[:, LANES * j:LANES * (j + 1)], cos, sa, sb) for j in range(ATTN_W // LANES)], axis=1)
    qt_ref[...] = (qr * q_scale).T.astype(BF16)

    k = _bdot(h, w_ref[:, ATTN_W:ATTN_W + KV_W])
    msk = _bdot((k * k).astype(BF16), ek_ref[...])
    kn = k * lax.rsqrt(msk + EPS) * gk_ref[...]
    k_ref[...] = _rope128(kn, cos, sa, sb).astype(BF16)

    v = _bdot(h, w_ref[:, ATTN_W + KV_W:ATTN_W + 2 * KV_W])
    vt_ref[...] = v.T.astype(BF16)

    o = ATTN_W + 2 * KV_W
    cw = (w_ref.shape[1] - o) // 3
    bg_ref[...] = _bdot(h, w_ref[:, o:o + cw]).astype(BF16)
    cg = _bdot(h, w_ref[:, o + cw:o + 2 * cw])
    hv = _bdot(h, w_ref[:, o + 2 * cw:o + 3 * cw])
    z_ref[...] = (cg * hv).astype(BF16)


def _inproj(x, mods, mod_row, g_pre, w_in, gq, gk, eq, ek, cos, sa, sb, tm):
    b, s, d = x.shape
    nt = s // tm
    cw = (w_in.shape[1] - ATTN_W - 2 * KV_W) // 3
    row = (lambda bi: bi) if mod_row is None else (lambda bi: mod_row)
    tok = lambda bi, i: (bi, i, 0)
    return pl.pallas_call(
        functools.partial(_inproj_kernel, q_scale=HEAD_DIM ** -0.5 * LOG2E),
        grid=(b, nt),
        in_specs=[pl.BlockSpec((None, tm, d), tok),
                  pl.BlockSpec((None, 6, d), lambda bi, i: (row(bi), 0, 0)),
                  _const_spec((1, d)),
                  _const_spec(w_in.shape),
                  _const_spec((1, ATTN_W)),
                  _const_spec((1, KV_W)),
                  _const_spec((ATTN_W, ATTN_W)),
                  _const_spec((KV_W, KV_W)),
                  pl.BlockSpec((tm, LANES), lambda bi, i: (i, 0)),
                  pl.BlockSpec((tm, LANES), lambda bi, i: (i, 0)),
                  pl.BlockSpec((tm, LANES), lambda bi, i: (i, 0))],
        out_specs=[pl.BlockSpec((None, ATTN_W, tm), lambda bi, i: (bi, 0, i)),
                   pl.BlockSpec((None, tm, KV_W), tok),
                   pl.BlockSpec((None, None, KV_W, tm), lambda bi, i: (bi, i, 0, 0)),
                   pl.BlockSpec((None, tm, cw), tok),
                   pl.BlockSpec((None, tm, cw), tok)],
        out_shape=[jax.ShapeDtypeStruct((b, ATTN_W, s), BF16),
                   jax.ShapeDtypeStruct((b, s, KV_W), BF16),
                   jax.ShapeDtypeStruct((b, nt, KV_W, tm), BF16),
                   jax.ShapeDtypeStruct((b, s, cw), BF16),
                   jax.ShapeDtypeStruct((b, s, cw), BF16)],
        compiler_params=_cparams(("parallel", "parallel")),
        name="inproj",
    )(x, mods, g_pre, w_in, gq, gk, eq, ek, cos, sa, sb)


def _attn_kernel(qt_ref, k_ref, vt_ref, kc_ref, vct_ref, o_ref, qp_ref, m_ref, acc_ref, ot_ref):
    n_kb = k_ref.shape[0]
    group = N_HEADS // N_KV_HEADS

    for h in range(N_HEADS):
        qh = qt_ref[HEAD_DIM * h:HEAD_DIM * (h + 1), :]
        zq = jnp.zeros_like(qh)
        qp_ref[h] = jnp.concatenate([qh, zq] if h // group == 0 else [zq, qh], axis=0)

    def sweep(kb, vtb, first):
        ones = jnp.ones((16, vtb.shape[1]), BF16)
        s_next = _bdot(kb, qp_ref[0])
        for h in range(N_HEADS):
            g = h // group
            s = s_next
            if h + 1 < N_HEADS:
                s_next = _bdot(kb, qp_ref[h + 1])
            mb = jnp.max(s, axis=0, keepdims=True)
            m_old = None if first else m_ref[h:h + 1, :]
            m_new = mb if first else jnp.maximum(m_old, mb)
            p = jnp.exp2(s - m_new).astype(BF16)
            va = jnp.concatenate([vtb[HEAD_DIM * g:HEAD_DIM * (g + 1), :], ones], axis=0)
            pv = _bdot(va, p)
            acc_ref[h] = pv if first else acc_ref[h] * jnp.exp2(m_old - m_new) + pv
            m_ref[h:h + 1, :] = m_new

    sweep(kc_ref[0], vct_ref[0], True)

    def body(i, carry):
        sweep(k_ref[i], vt_ref[i], False)
        return carry

    lax.fori_loop(0, n_kb, body, 0)
    for h in range(N_HEADS):
        acc = acc_ref[h]
        ot_ref[HEAD_DIM * h:HEAD_DIM * (h + 1), :] = acc[0:HEAD_DIM] / acc[HEAD_DIM:HEAD_DIM + 1]
    o_ref[...] = ot_ref[...].T.astype(BF16)


def _attention(qt, k4, vt4, kc4, vct4, tq):
    b, _, s = qt.shape
    _, n_kb, tk, _ = k4.shape
    ctx = kc4.shape[2]
    return pl.pallas_call(
        _attn_kernel,
        grid=(b, s // tq),
        in_specs=[pl.BlockSpec((None, ATTN_W, tq), lambda bi, i: (bi, 0, i)),
                  pl.BlockSpec((None, n_kb, tk, KV_W), lambda bi, i: (bi, 0, 0, 0)),
                  pl.BlockSpec((None, n_kb, KV_W, tk), lambda bi, i: (bi, 0, 0, 0)),
                  pl.BlockSpec((None, 1, ctx, KV_W), lambda bi, i: (bi, 0, 0, 0)),
                  pl.BlockSpec((None, 1, KV_W, ctx), lambda bi, i: (bi, 0, 0, 0))],
        out_specs=pl.BlockSpec((None, tq, ATTN_W), lambda bi, i: (bi, i, 0)),
        out_shape=jax.ShapeDtypeStruct((b, s, ATTN_W), BF16),
        scratch_shapes=[pltpu.VMEM((N_HEADS, KV_W, tq), BF16),
                        pltpu.VMEM((N_HEADS, tq), F32),
                        pltpu.VMEM((N_HEADS, HEAD_DIM + 16, tq), F32),
                        pltpu.VMEM((ATTN_W, tq), F32)],
        compiler_params=_cparams(("parallel", "parallel")),
        name="attention",
    )(qt, k4, vt4, kc4, vct4)


def _outproj_kernel(x_ref, mod_ref, attn_ref, bg_ref, z_ref, zp_ref, zn_ref, wc_ref, wo_ref, gpost_ref,
                    o_ref):
    i = pl.program_id(1)
    nt = pl.num_programs(1)
    tm = z_ref.shape[0]
    z = z_ref[...].astype(F32)
    halo = zp_ref.shape[0]
    zprev = zp_ref[...].astype(F32)[halo - 1:halo, :] * (i > 0).astype(F32)
    znext = zn_ref[...].astype(F32)[0:1, :] * (i < nt - 1).astype(F32)
    row = lax.broadcasted_iota(jnp.int32, z.shape, 0)
    zm1 = jnp.where(row == 0, zprev, pltpu.roll(z, 1, 0))
    zp1 = jnp.where(row == tm - 1, znext, pltpu.roll(z, tm - 1, 0))
    wc = wc_ref[...]
    conv = bg_ref[...].astype(F32) * (wc[0:1] * zm1 + wc[1:2] * z + wc[2:3] * zp1)
    y = _bdot(attn_ref[...], wo_ref[0:ATTN_W, :]) + _bdot(conv.astype(BF16), wo_ref[ATTN_W:, :])
    mod = mod_ref[...]
    o_ref[...] = x_ref[...] + mod[2:3] * _rms(y, gpost_ref[...])


def _outproj(x, mods, attn, bg, z, w_conv, w_out, g_post, tm):
    b, s, d = x.shape
    cw = z.shape[2]
    halo = 16
    r = tm // halo
    last = s // halo - 1
    tok = lambda bi, i: (bi, i, 0)
    return pl.pallas_call(
        _outproj_kernel,
        grid=(b, s // tm),
        in_specs=[pl.BlockSpec((None, tm, d), tok),
                  pl.BlockSpec((None, 6, d), lambda bi, i: (bi, 0, 0)),
                  pl.BlockSpec((None, tm, ATTN_W), tok),
                  pl.BlockSpec((None, tm, cw), tok),
                  pl.BlockSpec((None, tm, cw), tok),
                  pl.BlockSpec((None, halo, cw), lambda bi, i: (bi, jnp.maximum(i * r - 1, 0), 0)),
                  pl.BlockSpec((None, halo, cw), lambda bi, i: (bi, jnp.minimum((i + 1) * r, last), 0)),
                  _const_spec(w_conv.shape),
                  _const_spec(w_out.shape),
                  _const_spec((1, d))],
        out_specs=pl.BlockSpec((None, tm, d), tok),
        out_shape=jax.ShapeDtypeStruct((b, s, d), F32),
        compiler_params=_cparams(("parallel", "parallel")),
        name="outproj",
    )(x, mods, attn, bg, z, z, z, w_conv, w_out, g_post)


def _swiglu_kernel(x_ref, mod_ref, gpre_ref, wg_ref, wu_ref, wd_ref, gpost_ref, o_ref, *, chunk):
    mod = mod_ref[...]
    x = x_ref[...]
    h = _pre(x, gpre_ref[...], mod[3:4], mod[4:5]).astype(BF16)
    acc = None
    for c in range(wg_ref.shape[1] // chunk):
        sl = slice(c * chunk, (c + 1) * chunk)
        a = (jax.nn.silu(_bdot(h, wg_ref[:, sl])) * _bdot(h, wu_ref[:, sl])).astype(BF16)
        part = _bdot(a, wd_ref[sl, :])
        acc = part if acc is None else acc + part
    o_ref[...] = x + mod[5:6] * _rms(acc, gpost_ref[...])


def _swiglu(x, mods, g_pre, w_gate, w_up, w_down, g_post, tm):
    b, s, d = x.shape
    tok = lambda bi, i: (bi, i, 0)
    return pl.pallas_call(
        functools.partial(_swiglu_kernel, chunk=FF_CHUNK_DENSE),
        grid=(b, s // tm),
        in_specs=[pl.BlockSpec((None, tm, d), tok),
                  pl.BlockSpec((None, 6, d), lambda bi, i: (bi, 0, 0)),
                  _const_spec((1, d)),
                  _const_spec(w_gate.shape),
                  _const_spec(w_up.shape),
                  _const_spec(w_down.shape),
                  _const_spec((1, d))],
        out_specs=pl.BlockSpec((None, tm, d), tok),
        out_shape=jax.ShapeDtypeStruct((b, s, d), F32),
        compiler_params=_cparams(("parallel", "parallel")),
        name="swiglu",
    )(x, mods, g_pre, w_gate, w_up, w_down, g_post)


def _gmlp_kernel(x_ref, mod_ref, gpre_ref, win_ref, gv_ref, bv_ref, ws_ref, bs_ref, wout_ref, gpost_ref,
                 gffn_ref, wr_ref, br_ref, tri_ref,
                 x3_ref, h3_ref, route_ref, cnt_ref, run_ref):
    first = jnp.logical_and(pl.program_id(0) == 0, pl.program_id(1) == 0)

    @pl.when(first)
    def _():
        run_ref[...] = jnp.zeros_like(run_ref)

    mod = mod_ref[...]
    x = x_ref[...]
    tm, d = x.shape
    h = _pre(x, gpre_ref[...], mod[0:1], mod[1:2]).astype(BF16)
    u = jax.nn.gelu(_bdot(h, win_ref[:, 0:d]))
    v = jax.nn.gelu(_bdot(h, win_ref[:, d:2 * d]))
    mu = jnp.mean(v, axis=-1, keepdims=True)
    vc = v - mu
    var = jnp.mean(vc * vc, axis=-1, keepdims=True)
    vn = (vc * lax.rsqrt(var + EPS) * gv_ref[...] + bv_ref[...]).astype(BF16)

    n_chunks = tm // CHUNK
    gc = d // GM_GROUPS
    mixed = []
    for g in range(GM_GROUPS):
        rhs = jnp.concatenate([vn[CHUNK * c:CHUNK * (c + 1), gc * g:gc * (g + 1)] for c in range(n_chunks)],
                              axis=1)
        bias = bs_ref[g]
        mixed.append(_bdot(ws_ref[g], rhs) + jnp.concatenate([bias] * n_chunks, axis=1))
    s = jnp.concatenate(
        [jnp.concatenate([mixed[g][:, gc * c:gc * (c + 1)] for g in range(GM_GROUPS)], axis=1)
         for c in range(n_chunks)], axis=0)
    y = _bdot((u * s).astype(BF16), wout_ref[...])
    x3 = x + mod[2:3] * _rms(y, gpost_ref[...])
    x3_ref[...] = x3

    h3 = _pre(x3, gffn_ref[...], mod[3:4], mod[4:5])
    h3_ref[...] = h3
    h_hi = h3.astype(BF16)
    h_lo = (h3 - h_hi.astype(F32)).astype(BF16)
    logits = (_bdot(h_hi, wr_ref[0]) + _bdot(h_lo, wr_ref[0]) + _bdot(h_hi, wr_ref[1])
              + br_ref[...])
    lane = lax.broadcasted_iota(jnp.int32, logits.shape, 1)
    m1 = jnp.max(logits, axis=1, keepdims=True)
    i1 = jnp.min(jnp.where(logits == m1, lane, LANES), axis=1, keepdims=True)
    rest = jnp.where(lane == i1, 2.0 * NEG_BIG, logits)
    m2 = jnp.max(rest, axis=1, keepdims=True)
    i2 = jnp.min(jnp.where(rest == m2, lane, LANES), axis=1, keepdims=True)
    e21 = jnp.exp(m2 - m1)
    w1 = 1.0 / (1.0 + e21)
    w2 = e21 / (1.0 + e21)
    hot1 = lane == i1
    hot2 = lane == i2
    onehot = jnp.where(jnp.logical_or(hot1, hot2), 1.0, 0.0)
    before = _bdot(tri_ref[...], onehot.astype(BF16)) + run_ref[...]
    r1 = jnp.sum(jnp.where(hot1, before, 0.0), axis=1, keepdims=True)
    r2 = jnp.sum(jnp.where(hot2, before, 0.0), axis=1, keepdims=True)
    run = run_ref[...] + jnp.sum(onehot, axis=0, keepdims=True)
    run_ref[...] = run
    cnt_ref[...] = jnp.broadcast_to(run, cnt_ref.shape)
    fields = (i1.astype(F32), i2.astype(F32), r1, r2, w1, w2)
    route = jnp.zeros(logits.shape, F32)
    for j, f in enumerate(fields):
        route = jnp.where(lane == j, f, route)
    route_ref[...] = route


def _gmlp_router(x, mods, g_pre, w_in, g_v, b_v, w_s, b_s_b, w_out, g_post, g_ffn, w_r, b_r, tm):
    b, s, d = x.shape
    tok = lambda bi, i: (bi, i, 0)
    tri = jnp.tril(jnp.ones((tm, tm), F32), -1).astype(BF16)
    return pl.pallas_call(
        _gmlp_kernel,
        grid=(b, s // tm),
        in_specs=[pl.BlockSpec((None, tm, d), tok),
                  pl.BlockSpec((None, 6, d), lambda bi, i: (bi, 0, 0)),
                  _const_spec((1, d)),
                  _const_spec(w_in.shape),
                  _const_spec((1, d)),
                  _const_spec((1, d)),
                  _const_spec(w_s.shape),
                  _const_spec(b_s_b.shape),
                  _const_spec(w_out.shape),
                  _const_spec((1, d)),
                  _const_spec((1, d)),
                  _const_spec(w_r.shape),
                  _const_spec((1, LANES)),
                  _const_spec((tm, tm))],
        out_specs=[pl.BlockSpec((None, tm, d), tok),
                   pl.BlockSpec((None, tm, d), tok),
                   pl.BlockSpec((None, tm, LANES), tok),
                   pl.BlockSpec((8, LANES), lambda bi, i: (0, 0))],
        out_shape=[jax.ShapeDtypeStruct((b, s, d), F32),
                   jax.ShapeDtypeStruct((b, s, d), F32),
                   jax.ShapeDtypeStruct((b, s, LANES), F32),
                   jax.ShapeDtypeStruct((8, LANES), F32)],
        scratch_shapes=[pltpu.VMEM((1, LANES), F32)],
        compiler_params=_cparams(("arbitrary", "arbitrary")),
        name="gmlp_router",
    )(x, mods, g_pre, w_in, g_v, b_v, w_s, b_s_b, w_out, g_post, g_ffn, w_r, b_r, tri)


def _dispatch_kernel(slot_ref, h_ref, xs_in_ref, xs_ref, sem):
    del xs_in_ref
    tm = h_ref.shape[0]

    def row_copy(r, k):
        dst = slot_ref[0, 0, 2 * r + k]
        return pltpu.make_async_copy(h_ref.at[pl.ds(r, 1), :], xs_ref.at[pl.ds(dst, 1), :], sem)

    def issue(j, _):
        for u in range(DMA_UNROLL):
            row_copy(j * DMA_UNROLL + u, 0).start()
            row_copy(j * DMA_UNROLL + u, 1).start()
        return 0

    lax.fori_loop(0, tm // DMA_UNROLL, issue, 0)
    for _ in range(2):
        pltpu.make_async_copy(h_ref, xs_ref.at[pl.ds(0, tm), :], sem).wait()


def _dispatch(h, slots, n_slots, tm):
    n, d = h.shape
    nt = n // tm
    return pl.pallas_call(
        _dispatch_kernel,
        grid=(nt,),
        in_specs=[pl.BlockSpec((1, 1, 2 * tm), lambda i: (i, 0, 0), memory_space=pltpu.SMEM),
                  pl.BlockSpec((tm, d), lambda i: (i, 0)),
                  pl.BlockSpec(memory_space=pl.ANY)],
        out_specs=pl.BlockSpec(memory_space=pl.ANY),
        out_shape=jax.ShapeDtypeStruct((n_slots, d), F32),
        scratch_shapes=[pltpu.SemaphoreType.DMA(())],
        input_output_aliases={2: 0},
        compiler_params=_cparams(("arbitrary",)),
        name="dispatch",
    )(slots.reshape(nt, 1, 2 * tm), h, jnp.zeros((n_slots, d), F32))


def _expert_kernel(te_ref, na_ref, xs_ref, wg_ref, wu_ref, wd_ref, ys_ref, xb_ref):
    t = pl.program_id(0)
    c = pl.program_id(1)
    active = t < na_ref[0]

    @pl.when(jnp.logical_and(active, c == 0))
    def _():
        xb_ref[...] = xs_ref[...].astype(BF16)

    @pl.when(active)
    def _():
        xb = xb_ref[...]
        a = (jax.nn.silu(_bdot(xb, wg_ref[...])) * _bdot(xb, wu_ref[...])).astype(BF16)
        part = _bdot(a, wd_ref[...])

        @pl.when(c == 0)
        def _():
            ys_ref[...] = part

        @pl.when(c > 0)
        def _():
            ys_ref[...] += part

    @pl.when(jnp.logical_and(jnp.logical_not(active), c == 0))
    def _():
        ys_ref[...] = jnp.zeros_like(ys_ref)


def _experts(tile_expert, n_active, xs, w_gate, w_up, w_down, tm, chunk):
    n_slots, d = xs.shape
    ff = w_gate.shape[2]
    n_tiles = n_slots // tm

    def live(t, na):
        return jnp.minimum(t, na[0] - 1)

    grid_spec = pltpu.PrefetchScalarGridSpec(
        num_scalar_prefetch=2,
        grid=(n_tiles, ff // chunk),
        in_specs=[pl.BlockSpec((tm, d), lambda t, c, te, na: (live(t, na), 0)),
                  pl.BlockSpec((None, d, chunk), lambda t, c, te, na: (te[live(t, na)], 0, jnp.where(t < na[0], c, ff // chunk - 1))),
                  pl.BlockSpec((None, d, chunk), lambda t, c, te, na: (te[live(t, na)], 0, jnp.where(t < na[0], c, ff // chunk - 1))),
                  pl.BlockSpec((None, chunk, d), lambda t, c, te, na: (te[live(t, na)], jnp.where(t < na[0], c, ff // chunk - 1), 0))],
        out_specs=pl.BlockSpec((tm, d), lambda t, c, te, na: (t, 0)),
        scratch_shapes=[pltpu.VMEM((tm, d), BF16)],
    )
    return pl.pallas_call(
        _expert_kernel,
        grid_spec=grid_spec,
        out_shape=jax.ShapeDtypeStruct((n_slots, d), F32),
        compiler_params=_cparams(("arbitrary", "arbitrary")),
        name="experts",
    )(tile_expert, n_active, xs, w_gate, w_up, w_down)


def _combine_kernel(slot_ref, x_ref, mod_ref, route_ref, gpost_ref, ys_ref, o_ref, buf_ref, sem):
    tm = x_ref.shape[0]

    def row_copy(r, k):
        src = slot_ref[0, 0, 2 * r + k]
        return pltpu.make_async_copy(ys_ref.at[pl.ds(src, 1), :], buf_ref.at[k, pl.ds(r, 1), :], sem)

    def issue(j, _):
        for u in range(DMA_UNROLL):
            row_copy(j * DMA_UNROLL + u, 0).start()
            row_copy(j * DMA_UNROLL + u, 1).start()
        return 0

    lax.fori_loop(0, tm // DMA_UNROLL, issue, 0)
    for k in range(2):
        pltpu.make_async_copy(ys_ref.at[pl.ds(0, tm), :], buf_ref.at[k], sem).wait()

    route = route_ref[...]
    f = route[:, 4:5] * buf_ref[0] + route[:, 5:6] * buf_ref[1]
    mod = mod_ref[...]
    o_ref[...] = x_ref[...] + mod[5:6] * _rms(f, gpost_ref[...])


def _combine(x, mods, route, g_post, ys, slots, seq, tm):
    n, d = x.shape
    nt = n // tm
    per_batch = seq // tm
    return pl.pallas_call(
        _combine_kernel,
        grid=(nt,),
        in_specs=[pl.BlockSpec((1, 1, 2 * tm), lambda i: (i, 0, 0), memory_space=pltpu.SMEM),
                  pl.BlockSpec((tm, d), lambda i: (i, 0)),
                  pl.BlockSpec((None, 6, d), lambda i: (i // per_batch, 0, 0)),
                  pl.BlockSpec((tm, LANES), lambda i: (i, 0)),
                  _const_spec((1, d)),
                  pl.BlockSpec(memory_space=pl.ANY)],
        out_specs=pl.BlockSpec((tm, d), lambda i: (i, 0)),
        out_shape=jax.ShapeDtypeStruct((n, d), F32),
        scratch_shapes=[pltpu.VMEM((2, tm, d), F32), pltpu.SemaphoreType.DMA(())],
        compiler_params=_cparams(("arbitrary",)),
        name="combine",
    )(slots.reshape(nt, 1, 2 * tm), x, mods, route, g_post, ys)


def _rope_tables(n):
    axis_dim = HEAD_DIM // 2
    pos = jnp.arange(n, dtype=jnp.int32)
    r = (pos // GRID_W).astype(F32)[:, None]
    col = (pos % GRID_W).astype(F32)[:, None]
    inv = 1.0 / (ROPE_THETA ** (jnp.arange(0, axis_dim, 2, dtype=F32) / axis_dim))
    ang = jnp.concatenate([r * inv, col * inv], axis=-1)
    cos, sin = jnp.cos(ang), jnp.sin(ang)
    zero = jnp.zeros_like(sin)
    reps = LANES // HEAD_DIM
    return (jnp.tile(jnp.concatenate([cos, cos], -1), (1, reps)),
            jnp.tile(jnp.concatenate([-sin, zero], -1), (1, reps)),
            jnp.tile(jnp.concatenate([zero, sin], -1), (1, reps)))


def _head_mean_matrix(width):
    idx = jnp.arange(width) // HEAD_DIM
    return jnp.where(idx[:, None] == idx[None, :], 1.0 / HEAD_DIM, 0.0).astype(BF16)


def kernel(x, c, ctx, c_ctx, e_w_mod, e_b_mod, e_g_pre_mix, e_g_post_mix, e_w_in, e_g_q, e_g_k, e_w_conv, e_w_out, e_g_pre_ffn, e_g_post_ffn, e_w_gate, e_w_up, e_w_down, o_w_mod, o_b_mod, o_g_pre_mix, o_g_post_mix, o_w_in, o_g_v, o_b_v, o_w_s, o_b_s, o_w_out, o_g_pre_ffn, o_g_post_ffn, o_w_router, o_b_router, o_w_gate, o_w_up, o_w_down):
    b, s, d = x.shape
    n_ctx = ctx.shape[1]
    n = b * s
    tm = min(TOKEN_TILE, s)
    tq = min(ATTN_Q_TILE, s)
    assert b + 1 <= 8 and s % tm == 0 and s % tq == 0 and tm % CHUNK == 0 and n_ctx % 16 == 0
    assert e_w_mod.shape[0] == 1 and o_w_mod.shape[0] == 1
    row = lambda g: g.reshape(1, -1)

    cond8 = jnp.zeros((8, d), F32).at[:b].set(c).at[b].set(c_ctx)
    mods_e = _modulation(cond8, e_w_mod[0], e_b_mod[0])
    mods_o = _modulation(cond8, o_w_mod[0], o_b_mod[0])

    w_in = e_w_in[0].astype(BF16)
    gq = jnp.tile(e_g_q[0], N_HEADS).reshape(1, ATTN_W)
    gk = jnp.tile(e_g_k[0], N_KV_HEADS).reshape(1, KV_W)
    eq, ek = _head_mean_matrix(ATTN_W), _head_mean_matrix(KV_W)
    cos, sa, sb = _rope_tables(s)
    qt, k, vt4, bg, z = _inproj(x, mods_e, None, row(e_g_pre_mix[0]), w_in, gq, gk, eq, ek, cos, sa, sb, tm)
    ones = jnp.ones((n_ctx, LANES), F32)
    zeros = jnp.zeros((n_ctx, LANES), F32)
    _, kc, vct4, _, _ = _inproj(ctx, mods_e, b, row(e_g_pre_mix[0]), w_in, gq, gk, eq, ek, ones, zeros, zeros,
                                n_ctx)
    attn = _attention(qt, k.reshape(b, s // tm, tm, KV_W), vt4, kc.reshape(b, 1, n_ctx, KV_W), vct4, tq)
    x1 = _outproj(x, mods_e, attn, bg, z, e_w_conv[0], e_w_out[0].astype(BF16), row(e_g_post_mix[0]), tm)

    x2 = _swiglu(x1, mods_e, row(e_g_pre_ffn[0]), e_w_gate[0].astype(BF16), e_w_up[0].astype(BF16),
                 e_w_down[0].astype(BF16), row(e_g_post_ffn[0]), tm)

    b_s_b = jnp.broadcast_to(o_b_s[0][:, :, None], (GM_GROUPS, CHUNK, d // GM_GROUPS))
    w_r32 = jnp.zeros((d, LANES), F32).at[:, :N_EXPERTS].set(o_w_router[0])
    w_r_hi = w_r32.astype(BF16)
    w_r = jnp.stack([w_r_hi, (w_r32 - w_r_hi.astype(F32)).astype(BF16)])
    b_r = jnp.full((1, LANES), NEG_BIG, F32).at[0, :N_EXPERTS].set(o_b_router[0])
    x3, h3, route, counts = _gmlp_router(
        x2, mods_o, row(o_g_pre_mix[0]), o_w_in[0].astype(BF16), row(o_g_v[0]), row(o_b_v[0]),
        o_w_s[0].astype(BF16), b_s_b, o_w_out[0].astype(BF16), row(o_g_post_mix[0]), row(o_g_pre_ffn[0]),
        w_r, b_r, tm)

    te_rows = EXPERT_TILE
    n_tiles = -(-(2 * n + N_EXPERTS * (te_rows - 1)) // te_rows)
    n_slots = n_tiles * te_rows
    cnt = counts[0, :N_EXPERTS].astype(jnp.int32)
    tiles_per = (cnt + te_rows - 1) // te_rows
    tile_end = jnp.cumsum(tiles_per)
    base = (tile_end - tiles_per) * te_rows
    route2 = route.reshape(n, LANES)
    e12 = route2[:, 0:2].astype(jnp.int32)
    r12 = route2[:, 2:4].astype(jnp.int32)
    slots = (base[e12] + r12).reshape(-1)
    tile_expert = jnp.minimum(
        jnp.sum(jnp.arange(n_tiles, dtype=jnp.int32)[:, None] >= tile_end[None, :], axis=1),
        N_EXPERTS - 1).astype(jnp.int32)
    n_active = tile_end[-1:].astype(jnp.int32)

    rt = min(ROUTE_TILE, s)
    xs = _dispatch(h3.reshape(n, d), slots, n_slots, rt)
    ys = _experts(tile_expert, n_active, xs, o_w_gate[0].astype(BF16), o_w_up[0].astype(BF16),
                  o_w_down[0].astype(BF16), te_rows, FF_CHUNK_EXPERT)
    out = _combine(x3.reshape(n, d), mods_o, route2, row(o_g_post_ffn[0]), ys, slots, s, rt)
    return out.reshape(b, s, d)
```

```python
import functools
import math

import jax
import jax.numpy as jnp
from jax import lax
from jax.experimental import pallas as pl
from jax.experimental.pallas import tpu as pltpu

F32 = jnp.float32
BF16 = jnp.bfloat16

EPS = 1e-6
GRID_W = 64
N_HEADS = 8
N_KV_HEADS = 2
HEAD_DIM = 64
ATTN_W = N_HEADS * HEAD_DIM
KV_W = N_KV_HEADS * HEAD_DIM
ROPE_THETA = 10000.0
CHUNK = 128
GM_GROUPS = 8
N_EXPERTS = 8
LANES = 128
LOG2E = 1.4426950408889634
NEG_BIG = -1e30

VMEM_LIMIT_BYTES = 56 * 1024 * 1024

TOKEN_TILE = 512
ATTN_Q_TILE = 512
ATTN_BLOCKS_PER_ITER = 2
ATTN_LOOKAHEAD = 2
FF_CHUNK_DENSE = 1408
FF_CHUNK_EXPERT = 1792
FF_SUB = 256
EXPERT_TILE = 1024
ROUTE_TILE = 256


def _cparams(sem):
    return pltpu.CompilerParams(dimension_semantics=sem, vmem_limit_bytes=VMEM_LIMIT_BYTES)


def _const_spec(shape):
    n = len(shape)
    return pl.BlockSpec(shape, lambda *_: (0,) * n, pipeline_mode=pl.Buffered(1))


def _rms(x, g):
    return x * lax.rsqrt(jnp.mean(x * x, axis=-1, keepdims=True) + EPS) * g


def _pre(x, g, shift, scale):
    return _rms(x, g) * (1.0 + scale) + shift


def _bdot(a, b):
    return jnp.dot(a, b, preferred_element_type=F32)


def _mod_kernel(c_ref, w_ref, b_ref, o_ref):
    a = jax.nn.silu(c_ref[...])
    o_ref[...] = jnp.dot(a, w_ref[...], precision=lax.Precision.HIGHEST,
                         preferred_element_type=F32) + b_ref[...]


def _modulation(cond8, w_mod, b_mod):
    d = cond8.shape[1]
    out = pl.pallas_call(
        _mod_kernel,
        grid=(6,),
        in_specs=[pl.BlockSpec((8, d), lambda j: (0, 0)),
                  pl.BlockSpec((d, d), lambda j: (0, j)),
                  pl.BlockSpec((1, d), lambda j: (0, j))],
        out_specs=pl.BlockSpec((8, d), lambda j: (0, j)),
        out_shape=jax.ShapeDtypeStruct((8, 6 * d), F32),
        compiler_params=_cparams(("arbitrary",)),
        name="modulation",
    )(cond8, w_mod, b_mod.reshape(1, 6 * d))
    return out.reshape(8, 6, d)


def _rope128(t, cos, sa, sb):
    return t * cos + pltpu.roll(t, 96, 1) * sa + pltpu.roll(t, 32, 1) * sb


def _inproj_kernel(x_ref, mod_ref, gpre_ref, w_ref, gq_ref, gk_ref, eq_ref, ek_ref,
                   cos_ref, sa_ref, sb_ref, qt_ref, k_ref, vt_ref, bg_ref, z_ref, *, q_scale):
    mod = mod_ref[...]
    h = _pre(x_ref[...], gpre_ref[...], mod[0:1], mod[1:2]).astype(BF16)
    cos, sa, sb = cos_ref[...], sa_ref[...], sb_ref[...]

    q = _bdot(h, w_ref[:, 0:ATTN_W])
    ms = _bdot((q * q).astype(BF16), eq_ref[...])
    qn = q * lax.rsqrt(ms + EPS) * gq_ref[...]
    qr = jnp.concatenate(
        [_rope128(qn[:, LANES * j:LANES * (j + 1)], cos, sa, sb) for j in range(ATTN_W // LANES)], axis=1)
    qt_ref[...] = (qr * q_scale).T.astype(BF16)

    k = _bdot(h, w_ref[:, ATTN_W:ATTN_W + KV_W])
    msk = _bdot((k * k).astype(BF16), ek_ref[...])
    kn = k * lax.rsqrt(msk + EPS) * gk_ref[...]
    k_ref[...] = _rope128(kn, cos, sa, sb).astype(BF16)

    v = _bdot(h, w_ref[:, ATTN_W + KV_W:ATTN_W + 2 * KV_W])
    vt_ref[...] = v.T.astype(BF16)

    o = ATTN_W + 2 * KV_W
    cw = (w_ref.shape[1] - o) // 3
    bg_ref[...] = _bdot(h, w_ref[:, o:o + cw]).astype(BF16)
    cg = _bdot(h, w_ref[:, o + cw:o + 2 * cw])
    hv = _bdot(h, w_ref[:, o + 2 * cw:o + 3 * cw])
    z_ref[...] = (cg * hv).astype(BF16)


def _inproj(x, mods, mod_row, g_pre, w_in, gq, gk, eq, ek, cos, sa, sb, tm):
    b, s, d = x.shape
    nt = s // tm
    cw = (w_in.shape[1] - ATTN_W - 2 * KV_W) // 3
    row = (lambda bi: bi) if mod_row is None else (lambda bi: mod_row)
    tok = lambda bi, i: (bi, i, 0)
    return pl.pallas_call(
        functools.partial(_inproj_kernel, q_scale=HEAD_DIM ** -0.5 * LOG2E),
        grid=(b, nt),
        in_specs=[pl.BlockSpec((None, tm, d), tok),
                  pl.BlockSpec((None, 6, d), lambda bi, i: (row(bi), 0, 0)),
                  _const_spec((1, d)),
                  _const_spec(w_in.shape),
                  _const_spec((1, ATTN_W)),
                  _const_spec((1, KV_W)),
                  _const_spec((ATTN_W, ATTN_W)),
                  _const_spec((KV_W, KV_W)),
                  pl.BlockSpec((tm, LANES), lambda bi, i: (i, 0)),
                  pl.BlockSpec((tm, LANES), lambda bi, i: (i, 0)),
                  pl.BlockSpec((tm, LANES), lambda bi, i: (i, 0))],
        out_specs=[pl.BlockSpec((None, ATTN_W, tm), lambda bi, i: (bi, 0, i)),
                   pl.BlockSpec((None, tm, KV_W), tok),
                   pl.BlockSpec((None, None, KV_W, tm), lambda bi, i: (bi, i, 0, 0)),
                   pl.BlockSpec((None, tm, cw), tok),
                   pl.BlockSpec((None, tm, cw), tok)],
        out_shape=[jax.ShapeDtypeStruct((b, ATTN_W, s), BF16),
                   jax.ShapeDtypeStruct((b, s, KV_W), BF16),
                   jax.ShapeDtypeStruct((b, nt, KV_W, tm), BF16),
                   jax.ShapeDtypeStruct((b, s, cw), BF16),
                   jax.ShapeDtypeStruct((b, s, cw), BF16)],
        compiler_params=_cparams(("parallel", "parallel")),
        name="inproj",
    )(x, mods, g_pre, w_in, gq, gk, eq, ek, cos, sa, sb)


def _attn_kernel(qt_ref, k_ref, vt_ref, kc_ref, vct_ref, o_ref, qp_ref, m_ref, acc_ref, ot_ref, s_ref):
    n_kb = k_ref.shape[0]
    group = N_HEADS // N_KV_HEADS

    for h in range(N_HEADS):
        qh = qt_ref[HEAD_DIM * h:HEAD_DIM * (h + 1), :]
        zq = jnp.zeros_like(qh)
        qp_ref[h] = jnp.concatenate([qh, zq] if h // group == 0 else [zq, qh], axis=0)

    look = ATTN_LOOKAHEAD

    def attend(s, vtb, h, first):
        g = h // group
        mb = jnp.max(s, axis=0, keepdims=True)
        m_old = None if first else m_ref[h:h + 1, :]
        m_new = mb if first else jnp.maximum(m_old, mb)
        p = jnp.exp2(s - m_new).astype(BF16)
        va = jnp.concatenate([vtb[HEAD_DIM * g:HEAD_DIM * (g + 1), :], jnp.ones((16, vtb.shape[1]), BF16)], axis=0)
        pv = _bdot(va, p)
        acc_ref[h] = pv if first else acc_ref[h] * jnp.exp2(m_old - m_new) + pv
        m_ref[h:h + 1, :] = m_new

    def run(pending, blocks, kb_after, first):
        n_steps = len(blocks) * N_HEADS
        for n in range(n_steps):
            s = pending.pop(0)
            ahead = n + look
            kb = blocks[ahead // N_HEADS][0] if ahead < n_steps else kb_after
            pending.append(_bdot(kb, qp_ref[ahead % N_HEADS]))
            attend(s, blocks[n // N_HEADS][1], n % N_HEADS, first)
        return pending

    kc = kc_ref[0]
    pending = run([_bdot(kc, qp_ref[h]) for h in range(look)], [(kc, vct_ref[0])], k_ref[0], True)
    for l in range(look):
        s_ref[l] = pending[l]

    def body(i, carry):
        first_blk = i * ATTN_BLOCKS_PER_ITER
        blocks = [(k_ref[first_blk + j], vt_ref[first_blk + j]) for j in range(ATTN_BLOCKS_PER_ITER)]
        kb_after = k_ref[jnp.minimum(first_blk + ATTN_BLOCKS_PER_ITER, n_kb - 1)]
        pending = run([s_ref[l] for l in range(look)], blocks, kb_after, False)
        for l in range(look):
            s_ref[l] = pending[l]
        return carry

    lax.fori_loop(0, n_kb // ATTN_BLOCKS_PER_ITER, body, 0)
    for h in range(N_HEADS):
        acc = acc_ref[h]
        ot_ref[HEAD_DIM * h:HEAD_DIM * (h + 1), :] = acc[0:HEAD_DIM] / acc[HEAD_DIM:HEAD_DIM + 1]
    o_ref[...] = ot_ref[...].T.astype(BF16)


def _attention(qt, k4, vt4, kc4, vct4, tq):
    b, _, s = qt.shape
    _, n_kb, tk, _ = k4.shape
    ctx = kc4.shape[2]
    return pl.pallas_call(
        _attn_kernel,
        grid=(b, s // tq),
        in_specs=[pl.BlockSpec((None, ATTN_W, tq), lambda bi, i: (bi, 0, i)),
                  pl.BlockSpec((None, n_kb, tk, KV_W), lambda bi, i: (bi, 0, 0, 0)),
                  pl.BlockSpec((None, n_kb, KV_W, tk), lambda bi, i: (bi, 0, 0, 0)),
                  pl.BlockSpec((None, 1, ctx, KV_W), lambda bi, i: (bi, 0, 0, 0)),
                  pl.BlockSpec((None, 1, KV_W, ctx), lambda bi, i: (bi, 0, 0, 0))],
        out_specs=pl.BlockSpec((None, tq, ATTN_W), lambda bi, i: (bi, i, 0)),
        out_shape=jax.ShapeDtypeStruct((b, s, ATTN_W), BF16),
        scratch_shapes=[pltpu.VMEM((N_HEADS, KV_W, tq), BF16),
                        pltpu.VMEM((N_HEADS, tq), F32),
                        pltpu.VMEM((N_HEADS, HEAD_DIM + 16, tq), F32),
                        pltpu.VMEM((ATTN_W, tq), F32),
                        pltpu.VMEM((ATTN_LOOKAHEAD, tk, tq), F32)],
        compiler_params=_cparams(("parallel", "parallel")),
        name="attention",
    )(qt, k4, vt4, kc4, vct4)


def _outproj_kernel(x_ref, mod_ref, attn_ref, bg_ref, z_ref, zp_ref, zn_ref, wc_ref, wo_ref, gpost_ref,
                    o_ref):
    i = pl.program_id(1)
    nt = pl.num_programs(1)
    tm = z_ref.shape[0]
    z = z_ref[...].astype(F32)
    halo = zp_ref.shape[0]
    zprev = zp_ref[...].astype(F32)[halo - 1:halo, :] * (i > 0).astype(F32)
    znext = zn_ref[...].astype(F32)[0:1, :] * (i < nt - 1).astype(F32)
    row = lax.broadcasted_iota(jnp.int32, z.shape, 0)
    zm1 = jnp.where(row == 0, zprev, pltpu.roll(z, 1, 0))
    zp1 = jnp.where(row == tm - 1, znext, pltpu.roll(z, tm - 1, 0))
    wc = wc_ref[...]
    conv = bg_ref[...].astype(F32) * (wc[0:1] * zm1 + wc[1:2] * z + wc[2:3] * zp1)
    y = _bdot(attn_ref[...], wo_ref[0:ATTN_W, :]) + _bdot(conv.astype(BF16), wo_ref[ATTN_W:, :])
    mod = mod_ref[...]
    o_ref[...] = x_ref[...] + mod[2:3] * _rms(y, gpost_ref[...])


def _outproj(x, mods, attn, bg, z, w_conv, w_out, g_post, tm):
    b, s, d = x.shape
    cw = z.shape[2]
    halo = 16
    r = tm // halo
    last = s // halo - 1
    tok = lambda bi, i: (bi, i, 0)
    return pl.pallas_call(
        _outproj_kernel,
        grid=(b, s // tm),
        in_specs=[pl.BlockSpec((None, tm, d), tok),
                  pl.BlockSpec((None, 6, d), lambda bi, i: (bi, 0, 0)),
                  pl.BlockSpec((None, tm, ATTN_W), tok),
                  pl.BlockSpec((None, tm, cw), tok),
                  pl.BlockSpec((None, tm, cw), tok),
                  pl.BlockSpec((None, halo, cw), lambda bi, i: (bi, jnp.maximum(i * r - 1, 0), 0)),
                  pl.BlockSpec((None, halo, cw), lambda bi, i: (bi, jnp.minimum((i + 1) * r, last), 0)),
                  _const_spec(w_conv.shape),
                  _const_spec(w_out.shape),
                  _const_spec((1, d))],
        out_specs=pl.BlockSpec((None, tm, d), tok),
        out_shape=jax.ShapeDtypeStruct((b, s, d), F32),
        compiler_params=_cparams(("parallel", "parallel")),
        name="outproj",
    )(x, mods, attn, bg, z, z, z, w_conv, w_out, g_post)


def _swiglu_kernel(x_ref, mod_ref, gpre_ref, wg_ref, wu_ref, wd_ref, gpost_ref, o_ref, *, chunk):
    mod = mod_ref[...]
    x = x_ref[...]
    h = _pre(x, gpre_ref[...], mod[3:4], mod[4:5]).astype(BF16)
    acc = None
    for c in range(wg_ref.shape[1] // chunk):
        sl = slice(c * chunk, (c + 1) * chunk)
        a = (jax.nn.silu(_bdot(h, wg_ref[:, sl])) * _bdot(h, wu_ref[:, sl])).astype(BF16)
        part = _bdot(a, wd_ref[sl, :])
        acc = part if acc is None else acc + part
    o_ref[...] = x + mod[5:6] * _rms(acc, gpost_ref[...])


def _swiglu(x, mods, g_pre, w_gate, w_up, w_down, g_post, tm):
    b, s, d = x.shape
    tok = lambda bi, i: (bi, i, 0)
    return pl.pallas_call(
        functools.partial(_swiglu_kernel, chunk=FF_CHUNK_DENSE),
        grid=(b, s // tm),
        in_specs=[pl.BlockSpec((None, tm, d), tok),
                  pl.BlockSpec((None, 6, d), lambda bi, i: (bi, 0, 0)),
                  _const_spec((1, d)),
                  _const_spec(w_gate.shape),
                  _const_spec(w_up.shape),
                  _const_spec(w_down.shape),
                  _const_spec((1, d))],
        out_specs=pl.BlockSpec((None, tm, d), tok),
        out_shape=jax.ShapeDtypeStruct((b, s, d), F32),
        compiler_params=_cparams(("parallel", "parallel")),
        name="swiglu",
    )(x, mods, g_pre, w_gate, w_up, w_down, g_post)


def _gmlp_kernel(x_ref, mod_ref, gpre_ref, win_ref, gv_ref, bv_ref, ws_ref, bs_ref, wout_ref, gpost_ref,
                 gffn_ref, wr_ref, br_ref, tri_ref,
                 x3_ref, h3_ref, route_ref, cnt_ref, run_ref):
    first = jnp.logical_and(pl.program_id(0) == 0, pl.program_id(1) == 0)

    @pl.when(first)
    def _():
        run_ref[...] = jnp.zeros_like(run_ref)

    mod = mod_ref[...]
    x = x_ref[...]
    tm, d = x.shape
    h = _pre(x, gpre_ref[...], mod[0:1], mod[1:2]).astype(BF16)
    u = jax.nn.gelu(_bdot(h, win_ref[:, 0:d]))
    v = jax.nn.gelu(_bdot(h, win_ref[:, d:2 * d]))
    mu = jnp.mean(v, axis=-1, keepdims=True)
    vc = v - mu
    var = jnp.mean(vc * vc, axis=-1, keepdims=True)
    vn = (vc * lax.rsqrt(var + EPS) * gv_ref[...] + bv_ref[...]).astype(BF16)

    n_chunks = tm // CHUNK
    gc = d // GM_GROUPS
    mixed = []
    for g in range(GM_GROUPS):
        rhs = jnp.concatenate([vn[CHUNK * c:CHUNK * (c + 1), gc * g:gc * (g + 1)] for c in range(n_chunks)],
                              axis=1)
        bias = bs_ref[g]
        mixed.append(_bdot(ws_ref[g], rhs) + jnp.concatenate([bias] * n_chunks, axis=1))
    s = jnp.concatenate(
        [jnp.concatenate([mixed[g][:, gc * c:gc * (c + 1)] for g in range(GM_GROUPS)], axis=1)
         for c in range(n_chunks)], axis=0)
    y = _bdot((u * s).astype(BF16), wout_ref[...])
    x3 = x + mod[2:3] * _rms(y, gpost_ref[...])
    x3_ref[...] = x3

    h3 = _pre(x3, gffn_ref[...], mod[3:4], mod[4:5])
    h3_ref[...] = h3
    h_hi = h3.astype(BF16)
    h_lo = (h3 - h_hi.astype(F32)).astype(BF16)
    logits = (_bdot(h_hi, wr_ref[0]) + _bdot(h_lo, wr_ref[0]) + _bdot(h_hi, wr_ref[1])
              + br_ref[...])
    lane = lax.broadcasted_iota(jnp.int32, logits.shape, 1)
    m1 = jnp.max(logits, axis=1, keepdims=True)
    i1 = jnp.min(jnp.where(logits == m1, lane, LANES), axis=1, keepdims=True)
    rest = jnp.where(lane == i1, 2.0 * NEG_BIG, logits)
    m2 = jnp.max(rest, axis=1, keepdims=True)
    i2 = jnp.min(jnp.where(rest == m2, lane, LANES), axis=1, keepdims=True)
    e21 = jnp.exp(m2 - m1)
    w1 = 1.0 / (1.0 + e21)
    w2 = e21 / (1.0 + e21)
    hot1 = lane == i1
    hot2 = lane == i2
    onehot = jnp.where(jnp.logical_or(hot1, hot2), 1.0, 0.0)
    before = _bdot(tri_ref[...], onehot.astype(BF16)) + run_ref[...]
    r1 = jnp.sum(jnp.where(hot1, before, 0.0), axis=1, keepdims=True)
    r2 = jnp.sum(jnp.where(hot2, before, 0.0), axis=1, keepdims=True)
    run = run_ref[...] + jnp.sum(onehot, axis=0, keepdims=True)
    run_ref[...] = run
    cnt_ref[...] = jnp.broadcast_to(run, cnt_ref.shape)
    fields = (i1.astype(F32), i2.astype(F32), r1, r2, w1, w2)
    route = jnp.zeros(logits.shape, F32)
    for j, f in enumerate(fields):
        route = jnp.where(lane == j, f, route)
    route_ref[...] = route


def _gmlp_router(x, mods, g_pre, w_in, g_v, b_v, w_s, b_s_b, w_out, g_post, g_ffn, w_r, b_r, tm):
    b, s, d = x.shape
    tok = lambda bi, i: (bi, i, 0)
    tri = jnp.tril(jnp.ones((tm, tm), F32), -1).astype(BF16)
    return pl.pallas_call(
        _gmlp_kernel,
        grid=(b, s // tm),
        in_specs=[pl.BlockSpec((None, tm, d), tok),
                  pl.BlockSpec((None, 6, d), lambda bi, i: (bi, 0, 0)),
                  _const_spec((1, d)),
                  _const_spec(w_in.shape),
                  _const_spec((1, d)),
                  _const_spec((1, d)),
                  _const_spec(w_s.shape),
                  _const_spec(b_s_b.shape),
                  _const_spec(w_out.shape),
                  _const_spec((1, d)),
                  _const_spec((1, d)),
                  _const_spec(w_r.shape),
                  _const_spec((1, LANES)),
                  _const_spec((tm, tm))],
        out_specs=[pl.BlockSpec((None, tm, d), tok),
                   pl.BlockSpec((None, tm, d), tok),
                   pl.BlockSpec((None, tm, LANES), tok),
                   pl.BlockSpec((8, LANES), lambda bi, i: (0, 0))],
        out_shape=[jax.ShapeDtypeStruct((b, s, d), F32),
                   jax.ShapeDtypeStruct((b, s, d), F32),
                   jax.ShapeDtypeStruct((b, s, LANES), F32),
                   jax.ShapeDtypeStruct((8, LANES), F32)],
        scratch_shapes=[pltpu.VMEM((1, LANES), F32)],
        compiler_params=_cparams(("arbitrary", "arbitrary")),
        name="gmlp_router",
    )(x, mods, g_pre, w_in, g_v, b_v, w_s, b_s_b, w_out, g_post, g_ffn, w_r, b_r, tri)


def _dispatch_kernel(slot_ref, h_ref, xs_in_ref, xs_ref, sem):
    del xs_in_ref
    tm = h_ref.shape[0]
    for r in range(tm):
        for k in range(2):
            dst = slot_ref[0, 0, 2 * r + k]
            pltpu.make_async_copy(h_ref.at[pl.ds(r, 1), :], xs_ref.at[pl.ds(dst, 1), :], sem).start()
    for _ in range(2):
        pltpu.make_async_copy(h_ref, xs_ref.at[pl.ds(0, tm), :], sem).wait()


def _dispatch(h, slots, n_slots, tm):
    n, d = h.shape
    nt = n // tm
    return pl.pallas_call(
        _dispatch_kernel,
        grid=(nt,),
        in_specs=[pl.BlockSpec((1, 1, 2 * tm), lambda i: (i, 0, 0), memory_space=pltpu.SMEM),
                  pl.BlockSpec((tm, d), lambda i: (i, 0)),
                  pl.BlockSpec(memory_space=pl.ANY)],
        out_specs=pl.BlockSpec(memory_space=pl.ANY),
        out_shape=jax.ShapeDtypeStruct((n_slots, d), F32),
        scratch_shapes=[pltpu.SemaphoreType.DMA(())],
        input_output_aliases={2: 0},
        compiler_params=_cparams(("arbitrary",)),
        name="dispatch",
    )(slots.reshape(nt, 1, 2 * tm), h, jnp.zeros((n_slots, d), F32))


def _expert_kernel(te_ref, na_ref, xs_ref, wgu_ref, wd_ref, ys_ref, xb_ref):
    t = pl.program_id(0)
    c = pl.program_id(1)
    active = t < na_ref[0]

    @pl.when(jnp.logical_and(active, c == 0))
    def _():
        xb_ref[...] = xs_ref[...].astype(BF16)

    @pl.when(active)
    def _():
        xb = xb_ref[...]
        acts = []
        for j in range(wd_ref.shape[0] // FF_SUB):
            gu = _bdot(xb, wgu_ref[:, 2 * FF_SUB * j:2 * FF_SUB * (j + 1)])
            acts.append((jax.nn.silu(gu[:, 0:FF_SUB]) * gu[:, FF_SUB:2 * FF_SUB]).astype(BF16))
        part = _bdot(jnp.concatenate(acts, axis=1), wd_ref[...])

        @pl.when(c == 0)
        def _():
            ys_ref[...] = part

        @pl.when(c > 0)
        def _():
            ys_ref[...] += part

    @pl.when(jnp.logical_and(jnp.logical_not(active), c == 0))
    def _():
        ys_ref[...] = jnp.zeros_like(ys_ref)


def _experts(tile_expert, n_active, xs, w_gu, w_down, tm, chunk):
    n_slots, d = xs.shape
    ff = w_down.shape[1]
    n_tiles = n_slots // tm
    last = ff // chunk - 1

    def live(t, na):
        return jnp.minimum(t, na[0] - 1)

    def chunk_of(t, c, na):
        return jnp.where(t < na[0], c, last)

    grid_spec = pltpu.PrefetchScalarGridSpec(
        num_scalar_prefetch=2,
        grid=(n_tiles, ff // chunk),
        in_specs=[pl.BlockSpec((tm, d), lambda t, c, te, na: (live(t, na), 0)),
                  pl.BlockSpec((None, d, 2 * chunk), lambda t, c, te, na: (te[live(t, na)], 0, chunk_of(t, c, na))),
                  pl.BlockSpec((None, chunk, d), lambda t, c, te, na: (te[live(t, na)], chunk_of(t, c, na), 0))],
        out_specs=pl.BlockSpec((tm, d), lambda t, c, te, na: (t, 0)),
        scratch_shapes=[pltpu.VMEM((tm, d), BF16)],
    )
    return pl.pallas_call(
        _expert_kernel,
        grid_spec=grid_spec,
        out_shape=jax.ShapeDtypeStruct((n_slots, d), F32),
        compiler_params=_cparams(("arbitrary", "arbitrary")),
        name="experts",
    )(tile_expert, n_active, xs, w_gu, w_down)


def _combine_kernel(slot_ref, x_ref, mod_ref, route_ref, gpost_ref, ys_ref, o_ref, buf_ref, sem):
    i = pl.program_id(0)
    nt = pl.num_programs(0) - 1
    tm = x_ref.shape[0]

    @pl.when(i < nt)
    def _():
        b = i % 2
        for r in range(tm):
            for k in range(2):
                src = slot_ref[0, 0, 2 * r + k]
                pltpu.make_async_copy(ys_ref.at[pl.ds(src, 1), :], buf_ref.at[b, k, pl.ds(r, 1), :],
                                      sem.at[b]).start()

    @pl.when(i > 0)
    def _():
        b = (i - 1) % 2
        for k in range(2):
            pltpu.make_async_copy(ys_ref.at[pl.ds(0, tm), :], buf_ref.at[b, k], sem.at[b]).wait()
        route = route_ref[...]
        f = route[:, 4:5] * buf_ref[b, 0] + route[:, 5:6] * buf_ref[b, 1]
        mod = mod_ref[...]
        o_ref[...] = x_ref[...] + mod[5:6] * _rms(f, gpost_ref[...])


def _combine(x, mods, route, g_post, ys, slots, seq, tm):
    n, d = x.shape
    nt = n // tm
    per_batch = seq // tm
    done = lambda i: jnp.maximum(i - 1, 0)
    return pl.pallas_call(
        _combine_kernel,
        grid=(nt + 1,),
        in_specs=[pl.BlockSpec((1, 1, 2 * tm), lambda i: (jnp.minimum(i, nt - 1), 0, 0), memory_space=pltpu.SMEM),
                  pl.BlockSpec((tm, d), lambda i: (done(i), 0)),
                  pl.BlockSpec((None, 6, d), lambda i: (done(i) // per_batch, 0, 0)),
                  pl.BlockSpec((tm, LANES), lambda i: (done(i), 0)),
                  _const_spec((1, d)),
                  pl.BlockSpec(memory_space=pl.ANY)],
        out_specs=pl.BlockSpec((tm, d), lambda i: (done(i), 0)),
        out_shape=jax.ShapeDtypeStruct((n, d), F32),
        scratch_shapes=[pltpu.VMEM((2, 2, tm, d), F32), pltpu.SemaphoreType.DMA((2,))],
        compiler_params=_cparams(("arbitrary",)),
        name="combine",
    )(slots.reshape(nt, 1, 2 * tm), x, mods, route, g_post, ys)


def _rope_tables(n):
    axis_dim = HEAD_DIM // 2
    pos = jnp.arange(n, dtype=jnp.int32)
    r = (pos // GRID_W).astype(F32)[:, None]
    col = (pos % GRID_W).astype(F32)[:, None]
    inv = 1.0 / (ROPE_THETA ** (jnp.arange(0, axis_dim, 2, dtype=F32) / axis_dim))
    ang = jnp.concatenate([r * inv, col * inv], axis=-1)
    cos, sin = jnp.cos(ang), jnp.sin(ang)
    zero = jnp.zeros_like(sin)
    reps = LANES // HEAD_DIM
    return (jnp.tile(jnp.concatenate([cos, cos], -1), (1, reps)),
            jnp.tile(jnp.concatenate([-sin, zero], -1), (1, reps)),
            jnp.tile(jnp.concatenate([zero, sin], -1), (1, reps)))


def _head_mean_matrix(width):
    idx = jnp.arange(width) // HEAD_DIM
    return jnp.where(idx[:, None] == idx[None, :], 1.0 / HEAD_DIM, 0.0).astype(BF16)


def kernel(x, c, ctx, c_ctx, e_w_mod, e_b_mod, e_g_pre_mix, e_g_post_mix, e_w_in, e_g_q, e_g_k, e_w_conv, e_w_out, e_g_pre_ffn, e_g_post_ffn, e_w_gate, e_w_up, e_w_down, o_w_mod, o_b_mod, o_g_pre_mix, o_g_post_mix, o_w_in, o_g_v, o_b_v, o_w_s, o_b_s, o_w_out, o_g_pre_ffn, o_g_post_ffn, o_w_router, o_b_router, o_w_gate, o_w_up, o_w_down):
    b, s, d = x.shape
    n_ctx = ctx.shape[1]
    n = b * s
    tm = min(TOKEN_TILE, s)
    tq = min(ATTN_Q_TILE, s)
    assert b + 1 <= 8 and s % tm == 0 and s % tq == 0 and tm % CHUNK == 0 and n_ctx % 16 == 0
    assert (s // tm) % ATTN_BLOCKS_PER_ITER == 0 and FF_CHUNK_EXPERT % FF_SUB == 0
    assert e_w_mod.shape[0] == 1 and o_w_mod.shape[0] == 1
    row = lambda g: g.reshape(1, -1)

    cond8 = jnp.zeros((8, d), F32).at[:b].set(c).at[b].set(c_ctx)
    mods_e = _modulation(cond8, e_w_mod[0], e_b_mod[0])
    mods_o = _modulation(cond8, o_w_mod[0], o_b_mod[0])

    w_in = e_w_in[0].astype(BF16)
    gq = jnp.tile(e_g_q[0], N_HEADS).reshape(1, ATTN_W)
    gk = jnp.tile(e_g_k[0], N_KV_HEADS).reshape(1, KV_W)
    eq, ek = _head_mean_matrix(ATTN_W), _head_mean_matrix(KV_W)
    cos, sa, sb = _rope_tables(s)
    qt, k, vt4, bg, z = _inproj(x, mods_e, None, row(e_g_pre_mix[0]), w_in, gq, gk, eq, ek, cos, sa, sb, tm)
    ones = jnp.ones((n_ctx, LANES), F32)
    zeros = jnp.zeros((n_ctx, LANES), F32)
    _, kc, vct4, _, _ = _inproj(ctx, mods_e, b, row(e_g_pre_mix[0]), w_in, gq, gk, eq, ek, ones, zeros, zeros,
                                n_ctx)
    attn = _attention(qt, k.reshape(b, s // tm, tm, KV_W), vt4, kc.reshape(b, 1, n_ctx, KV_W), vct4, tq)
    x1 = _outproj(x, mods_e, attn, bg, z, e_w_conv[0], e_w_out[0].astype(BF16), row(e_g_post_mix[0]), tm)

    x2 = _swiglu(x1, mods_e, row(e_g_pre_ffn[0]), e_w_gate[0].astype(BF16), e_w_up[0].astype(BF16),
                 e_w_down[0].astype(BF16), row(e_g_post_ffn[0]), tm)

    b_s_b = jnp.broadcast_to(o_b_s[0][:, :, None], (GM_GROUPS, CHUNK, d // GM_GROUPS))
    w_r32 = jnp.zeros((d, LANES), F32).at[:, :N_EXPERTS].set(o_w_router[0])
    w_r_hi = w_r32.astype(BF16)
    w_r = jnp.stack([w_r_hi, (w_r32 - w_r_hi.astype(F32)).astype(BF16)])
    b_r = jnp.full((1, LANES), NEG_BIG, F32).at[0, :N_EXPERTS].set(o_b_router[0])
    x3, h3, route, counts = _gmlp_router(
        x2, mods_o, row(o_g_pre_mix[0]), o_w_in[0].astype(BF16), row(o_g_v[0]), row(o_b_v[0]),
        o_w_s[0].astype(BF16), b_s_b, o_w_out[0].astype(BF16), row(o_g_post_mix[0]), row(o_g_pre_ffn[0]),
        w_r, b_r, tm)

    te_rows = EXPERT_TILE
    n_tiles = -(-(2 * n + N_EXPERTS * (te_rows - 1)) // te_rows)
    n_slots = n_tiles * te_rows
    cnt = counts[0, :N_EXPERTS].astype(jnp.int32)
    tiles_per = (cnt + te_rows - 1) // te_rows
    tile_end = jnp.cumsum(tiles_per)
    base = (tile_end - tiles_per) * te_rows
    route2 = route.reshape(n, LANES)
    e12 = route2[:, 0:2].astype(jnp.int32)
    r12 = route2[:, 2:4].astype(jnp.int32)
    slots = (base[e12] + r12).reshape(-1)
    tile_expert = jnp.minimum(
        jnp.sum(jnp.arange(n_tiles, dtype=jnp.int32)[:, None] >= tile_end[None, :], axis=1),
        N_EXPERTS - 1).astype(jnp.int32)
    n_active = tile_end[-1:].astype(jnp.int32)

    rt = min(ROUTE_TILE, s)
    xs = _dispatch(h3.reshape(n, d), slots, n_slots, rt)
    ne, _, ff = o_w_gate[0].shape
    w_gu = jnp.stack([o_w_gate[0].reshape(ne, d, ff // FF_SUB, FF_SUB),
                      o_w_up[0].reshape(ne, d, ff // FF_SUB, FF_SUB)], axis=3).astype(BF16).reshape(ne, d, 2 * ff)
    ys = _experts(tile_expert, n_active, xs, w_gu, o_w_down[0].astype(BF16), te_rows, FF_CHUNK_EXPERT)
    out = _combine(x3.reshape(n, d), mods_o, route2, row(o_g_post_ffn[0]), ys, slots, s, rt)
    return out.reshape(b, s, d)
```

```python
import functools
import math

import jax
import jax.numpy as jnp
from jax import lax
from jax.experimental import pallas as pl
from jax.experimental.pallas import tpu as pltpu

F32 = jnp.float32
BF16 = jnp.bfloat16

EPS = 1e-6
GRID_W = 64
N_HEADS = 8
N_KV_HEADS = 2
HEAD_DIM = 64
ATTN_W = N_HEADS * HEAD_DIM
KV_W = N_KV_HEADS * HEAD_DIM
ROPE_THETA = 10000.0
CHUNK = 128
GM_GROUPS = 8
N_EXPERTS = 8
LANES = 128
LOG2E = 1.4426950408889634
NEG_BIG = -1e30

VMEM_LIMIT_BYTES = 56 * 1024 * 1024

TOKEN_TILE = 512
ATTN_Q_TILE = 512
ATTN_BLOCKS_PER_ITER = 2
ATTN_LOOKAHEAD = 2
FF_CHUNK_DENSE = 1408
FF_CHUNK_EXPERT = 1792
FF_SUB = 256
EXPERT_TILE = 1024
ROUTE_TILE = 256


def _cparams(sem):
    return pltpu.CompilerParams(dimension_semantics=sem, vmem_limit_bytes=VMEM_LIMIT_BYTES)


def _const_spec(shape):
    n = len(shape)
    return pl.BlockSpec(shape, lambda *_: (0,) * n, pipeline_mode=pl.Buffered(1))


def _rms(x, g):
    return x * lax.rsqrt(jnp.mean(x * x, axis=-1, keepdims=True) + EPS) * g


def _pre(x, g, shift, scale):
    return _rms(x, g) * (1.0 + scale) + shift


def _bdot(a, b):
    return jnp.dot(a, b, preferred_element_type=F32)


def _mod_kernel(c_ref, w_ref, b_ref, o_ref):
    a = jax.nn.silu(c_ref[...])
    o_ref[...] = jnp.dot(a, w_ref[...], precision=lax.Precision.HIGHEST,
                         preferred_element_type=F32) + b_ref[...]


def _modulation(cond8, w_mod, b_mod):
    d = cond8.shape[1]
    out = pl.pallas_call(
        _mod_kernel,
        grid=(6,),
        in_specs=[pl.BlockSpec((8, d), lambda j: (0, 0)),
                  pl.BlockSpec((d, d), lambda j: (0, j)),
                  pl.BlockSpec((1, d), lambda j: (0, j))],
        out_specs=pl.BlockSpec((8, d), lambda j: (0, j)),
        out_shape=jax.ShapeDtypeStruct((8, 6 * d), F32),
        compiler_params=_cparams(("arbitrary",)),
        name="modulation",
    )(cond8, w_mod, b_mod.reshape(1, 6 * d))
    return out.reshape(8, 6, d)


def _rope128(t, cos, sa, sb):
    return t * cos + pltpu.roll(t, 96, 1) * sa + pltpu.roll(t, 32, 1) * sb


def _inproj_kernel(x_ref, mod_ref, gpre_ref, w_ref, gq_ref, gk_ref, eq_ref, ek_ref,
                   cos_ref, sa_ref, sb_ref, qt_ref, k_ref, vt_ref, bg_ref, z_ref, *, q_scale):
    mod = mod_ref[...]
    h = _pre(x_ref[...], gpre_ref[...], mod[0:1], mod[1:2]).astype(BF16)
    cos, sa, sb = cos_ref[...], sa_ref[...], sb_ref[...]

    q = _bdot(h, w_ref[:, 0:ATTN_W])
    ms = _bdot((q * q).astype(BF16), eq_ref[...])
    qn = q * lax.rsqrt(ms + EPS) * gq_ref[...]
    qr = jnp.concatenate(
        [_rope128(qn[:, LANES * j:LANES * (j + 1)], cos, sa, sb) for j in range(ATTN_W // LANES)], axis=1)
    qt_ref[...] = (qr * q_scale).T.astype(BF16)

    k = _bdot(h, w_ref[:, ATTN_W:ATTN_W + KV_W])
    msk = _bdot((k * k).astype(BF16), ek_ref[...])
    kn = k * lax.rsqrt(msk + EPS) * gk_ref[...]
    k_ref[...] = _rope128(kn, cos, sa, sb).astype(BF16)

    v = _bdot(h, w_ref[:, ATTN_W + KV_W:ATTN_W + 2 * KV_W])
    vt_ref[...] = v.T.astype(BF16)

    o = ATTN_W + 2 * KV_W
    cw = (w_ref.shape[1] - o) // 3
    bg_ref[...] = _bdot(h, w_ref[:, o:o + cw]).astype(BF16)
    cg = _bdot(h, w_ref[:, o + cw:o + 2 * cw])
    hv = _bdot(h, w_ref[:, o + 2 * cw:o + 3 * cw])
    z_ref[...] = (cg * hv).astype(BF16)


def _inproj(x, mods, mod_row, g_pre, w_in, gq, gk, eq, ek, cos, sa, sb, tm):
    b, s, d = x.shape
    nt = s // tm
    cw = (w_in.shape[1] - ATTN_W - 2 * KV_W) // 3
    row = (lambda bi: bi) if mod_row is None else (lambda bi: mod_row)
    tok = lambda bi, i: (bi, i, 0)
    return pl.pallas_call(
        functools.partial(_inproj_kernel, q_scale=HEAD_DIM ** -0.5 * LOG2E),
        grid=(b, nt),
        in_specs=[pl.BlockSpec((None, tm, d), tok),
                  pl.BlockSpec((None, 6, d), lambda bi, i: (row(bi), 0, 0)),
                  _const_spec((1, d)),
                  _const_spec(w_in.shape),
                  _const_spec((1, ATTN_W)),
                  _const_spec((1, KV_W)),
                  _const_spec((ATTN_W, ATTN_W)),
                  _const_spec((KV_W, KV_W)),
                  pl.BlockSpec((tm, LANES), lambda bi, i: (i, 0)),
                  pl.BlockSpec((tm, LANES), lambda bi, i: (i, 0)),
                  pl.BlockSpec((tm, LANES), lambda bi, i: (i, 0))],
        out_specs=[pl.BlockSpec((None, ATTN_W, tm), lambda bi, i: (bi, 0, i)),
                   pl.BlockSpec((None, tm, KV_W), tok),
                   pl.BlockSpec((None, None, KV_W, tm), lambda bi, i: (bi, i, 0, 0)),
                   pl.BlockSpec((None, tm, cw), tok),
                   pl.BlockSpec((None, tm, cw), tok)],
        out_shape=[jax.ShapeDtypeStruct((b, ATTN_W, s), BF16),
                   jax.ShapeDtypeStruct((b, s, KV_W), BF16),
                   jax.ShapeDtypeStruct((b, nt, KV_W, tm), BF16),
                   jax.ShapeDtypeStruct((b, s, cw), BF16),
                   jax.ShapeDtypeStruct((b, s, cw), BF16)],
        compiler_params=_cparams(("parallel", "parallel")),
        name="inproj",
    )(x, mods, g_pre, w_in, gq, gk, eq, ek, cos, sa, sb)


def _attn_kernel(qt_ref, k_ref, vt_ref, kc_ref, vct_ref, o_ref, qp_ref, m_ref, acc_ref, ot_ref, s_ref):
    n_kb = k_ref.shape[0]
    group = N_HEADS // N_KV_HEADS

    for h in range(N_HEADS):
        qh = qt_ref[HEAD_DIM * h:HEAD_DIM * (h + 1), :]
        zq = jnp.zeros_like(qh)
        qp_ref[h] = jnp.concatenate([qh, zq] if h // group == 0 else [zq, qh], axis=0)

    look = ATTN_LOOKAHEAD

    def attend(s, vtb, h, first):
        g = h // group
        mb = jnp.max(s, axis=0, keepdims=True)
        m_old = None if first else m_ref[h:h + 1, :]
        m_new = mb if first else jnp.maximum(m_old, mb)
        p = jnp.exp2(s - m_new).astype(BF16)
        va = jnp.concatenate([vtb[HEAD_DIM * g:HEAD_DIM * (g + 1), :], jnp.ones((16, vtb.shape[1]), BF16)], axis=0)
        pv = _bdot(va, p)
        acc_ref[h] = pv if first else acc_ref[h] * jnp.exp2(m_old - m_new) + pv
        m_ref[h:h + 1, :] = m_new

    def run(pending, blocks, kb_after, first):
        n_steps = len(blocks) * N_HEADS
        for n in range(n_steps):
            s = pending.pop(0)
            ahead = n + look
            kb = blocks[ahead // N_HEADS][0] if ahead < n_steps else kb_after
            pending.append(_bdot(kb, qp_ref[ahead % N_HEADS]))
            attend(s, blocks[n // N_HEADS][1], n % N_HEADS, first)
        return pending

    kc = kc_ref[0]
    pending = run([_bdot(kc, qp_ref[h]) for h in range(look)], [(kc, vct_ref[0])], k_ref[0], True)
    for l in range(look):
        s_ref[l] = pending[l]

    def body(i, carry):
        first_blk = i * ATTN_BLOCKS_PER_ITER
        blocks = [(k_ref[first_blk + j], vt_ref[first_blk + j]) for j in range(ATTN_BLOCKS_PER_ITER)]
        kb_after = k_ref[jnp.minimum(first_blk + ATTN_BLOCKS_PER_ITER, n_kb - 1)]
        pending = run([s_ref[l] for l in range(look)], blocks, kb_after, False)
        for l in range(look):
            s_ref[l] = pending[l]
        return carry

    lax.fori_loop(0, n_kb // ATTN_BLOCKS_PER_ITER, body, 0)
    for h in range(N_HEADS):
        acc = acc_ref[h]
        ot_ref[HEAD_DIM * h:HEAD_DIM * (h + 1), :] = acc[0:HEAD_DIM] / acc[HEAD_DIM:HEAD_DIM + 1]
    o_ref[...] = ot_ref[...].T.astype(BF16)


def _attention(qt, k4, vt4, kc4, vct4, tq):
    b, _, s = qt.shape
    _, n_kb, tk, _ = k4.shape
    ctx = kc4.shape[2]
    return pl.pallas_call(
        _attn_kernel,
        grid=(b, s // tq),
        in_specs=[pl.BlockSpec((None, ATTN_W, tq), lambda bi, i: (bi, 0, i)),
                  pl.BlockSpec((None, n_kb, tk, KV_W), lambda bi, i: (bi, 0, 0, 0)),
                  pl.BlockSpec((None, n_kb, KV_W, tk), lambda bi, i: (bi, 0, 0, 0)),
                  pl.BlockSpec((None, 1, ctx, KV_W), lambda bi, i: (bi, 0, 0, 0)),
                  pl.BlockSpec((None, 1, KV_W, ctx), lambda bi, i: (bi, 0, 0, 0))],
        out_specs=pl.BlockSpec((None, tq, ATTN_W), lambda bi, i: (bi, i, 0)),
        out_shape=jax.ShapeDtypeStruct((b, s, ATTN_W), BF16),
        scratch_shapes=[pltpu.VMEM((N_HEADS, KV_W, tq), BF16),
                        pltpu.VMEM((N_HEADS, tq), F32),
                        pltpu.VMEM((N_HEADS, HEAD_DIM + 16, tq), F32),
                        pltpu.VMEM((ATTN_W, tq), F32),
                        pltpu.VMEM((ATTN_LOOKAHEAD, tk, tq), F32)],
        compiler_params=_cparams(("parallel", "parallel")),
        name="attention",
    )(qt, k4, vt4, kc4, vct4)


def _outproj_kernel(x_ref, mod_ref, attn_ref, bg_ref, z_ref, zp_ref, zn_ref, wc_ref, wo_ref, gpost_ref,
                    o_ref):
    i = pl.program_id(1)
    nt = pl.num_programs(1)
    tm = z_ref.shape[0]
    z = z_ref[...].astype(F32)
    halo = zp_ref.shape[0]
    zprev = zp_ref[...].astype(F32)[halo - 1:halo, :] * (i > 0).astype(F32)
    znext = zn_ref[...].astype(F32)[0:1, :] * (i < nt - 1).astype(F32)
    row = lax.broadcasted_iota(jnp.int32, z.shape, 0)
    zm1 = jnp.where(row == 0, zprev, pltpu.roll(z, 1, 0))
    zp1 = jnp.where(row == tm - 1, znext, pltpu.roll(z, tm - 1, 0))
    wc = wc_ref[...]
    conv = bg_ref[...].astype(F32) * (wc[0:1] * zm1 + wc[1:2] * z + wc[2:3] * zp1)
    y = _bdot(attn_ref[...], wo_ref[0:ATTN_W, :]) + _bdot(conv.astype(BF16), wo_ref[ATTN_W:, :])
    mod = mod_ref[...]
    o_ref[...] = x_ref[...] + mod[2:3] * _rms(y, gpost_ref[...])


def _outproj(x, mods, attn, bg, z, w_conv, w_out, g_post, tm):
    b, s, d = x.shape
    cw = z.shape[2]
    halo = 16
    r = tm // halo
    last = s // halo - 1
    tok = lambda bi, i: (bi, i, 0)
    return pl.pallas_call(
        _outproj_kernel,
        grid=(b, s // tm),
        in_specs=[pl.BlockSpec((None, tm, d), tok),
                  pl.BlockSpec((None, 6, d), lambda bi, i: (bi, 0, 0)),
                  pl.BlockSpec((None, tm, ATTN_W), tok),
                  pl.BlockSpec((None, tm, cw), tok),
                  pl.BlockSpec((None, tm, cw), tok),
                  pl.BlockSpec((None, halo, cw), lambda bi, i: (bi, jnp.maximum(i * r - 1, 0), 0)),
                  pl.BlockSpec((None, halo, cw), lambda bi, i: (bi, jnp.minimum((i + 1) * r, last), 0)),
                  _const_spec(w_conv.shape),
                  _const_spec(w_out.shape),
                  _const_spec((1, d))],
        out_specs=pl.BlockSpec((None, tm, d), tok),
        out_shape=jax.ShapeDtypeStruct((b, s, d), F32),
        compiler_params=_cparams(("parallel", "parallel")),
        name="outproj",
    )(x, mods, attn, bg, z, z, z, w_conv, w_out, g_post)


def _swiglu_kernel(x_ref, mod_ref, gpre_ref, wg_ref, wu_ref, wd_ref, gpost_ref, o_ref, *, chunk):
    mod = mod_ref[...]
    x = x_ref[...]
    h = _pre(x, gpre_ref[...], mod[3:4], mod[4:5]).astype(BF16)
    acc = None
    for c in range(wg_ref.shape[1] // chunk):
        sl = slice(c * chunk, (c + 1) * chunk)
        a = (jax.nn.silu(_bdot(h, wg_ref[:, sl])) * _bdot(h, wu_ref[:, sl])).astype(BF16)
        part = _bdot(a, wd_ref[sl, :])
        acc = part if acc is None else acc + part
    o_ref[...] = x + mod[5:6] * _rms(acc, gpost_ref[...])


def _swiglu(x, mods, g_pre, w_gate, w_up, w_down, g_post, tm):
    b, s, d = x.shape
    tok = lambda bi, i: (bi, i, 0)
    return pl.pallas_call(
        functools.partial(_swiglu_kernel, chunk=FF_CHUNK_DENSE),
        grid=(b, s // tm),
        in_specs=[pl.BlockSpec((None, tm, d), tok),
                  pl.BlockSpec((None, 6, d), lambda bi, i: (bi, 0, 0)),
                  _const_spec((1, d)),
                  _const_spec(w_gate.shape),
                  _const_spec(w_up.shape),
                  _const_spec(w_down.shape),
                  _const_spec((1, d))],
        out_specs=pl.BlockSpec((None, tm, d), tok),
        out_shape=jax.ShapeDtypeStruct((b, s, d), F32),
        compiler_params=_cparams(("parallel", "parallel")),
        name="swiglu",
    )(x, mods, g_pre, w_gate, w_up, w_down, g_post)


def _gmlp_kernel(x_ref, mod_ref, gpre_ref, win_ref, gv_ref, bv_ref, ws_ref, bs_ref, wout_ref, gpost_ref,
                 gffn_ref, wr_ref, br_ref, tri_ref,
                 x3_ref, h3_ref, route_ref, cnt_ref, run_ref):
    first = jnp.logical_and(pl.program_id(0) == 0, pl.program_id(1) == 0)

    @pl.when(first)
    def _():
        run_ref[...] = jnp.zeros_like(run_ref)

    mod = mod_ref[...]
    x = x_ref[...]
    tm, d = x.shape
    h = _pre(x, gpre_ref[...], mod[0:1], mod[1:2]).astype(BF16)
    u = jax.nn.gelu(_bdot(h, win_ref[:, 0:d]))
    v = jax.nn.gelu(_bdot(h, win_ref[:, d:2 * d]))
    mu = jnp.mean(v, axis=-1, keepdims=True)
    vc = v - mu
    var = jnp.mean(vc * vc, axis=-1, keepdims=True)
    vn = (vc * lax.rsqrt(var + EPS) * gv_ref[...] + bv_ref[...]).astype(BF16)

    n_chunks = tm // CHUNK
    gc = d // GM_GROUPS
    mixed = []
    for g in range(GM_GROUPS):
        rhs = jnp.concatenate([vn[CHUNK * c:CHUNK * (c + 1), gc * g:gc * (g + 1)] for c in range(n_chunks)],
                              axis=1)
        bias = bs_ref[g]
        mixed.append(_bdot(ws_ref[g], rhs) + jnp.concatenate([bias] * n_chunks, axis=1))
    s = jnp.concatenate(
        [jnp.concatenate([mixed[g][:, gc * c:gc * (c + 1)] for g in range(GM_GROUPS)], axis=1)
         for c in range(n_chunks)], axis=0)
    y = _bdot((u * s).astype(BF16), wout_ref[...])
    x3 = x + mod[2:3] * _rms(y, gpost_ref[...])
    x3_ref[...] = x3

    h3 = _pre(x3, gffn_ref[...], mod[3:4], mod[4:5])
    h3_ref[...] = h3
    h_hi = h3.astype(BF16)
    h_lo = (h3 - h_hi.astype(F32)).astype(BF16)
    logits = (_bdot(h_hi, wr_ref[0]) + _bdot(h_lo, wr_ref[0]) + _bdot(h_hi, wr_ref[1])
              + br_ref[...])
    lane = lax.broadcasted_iota(jnp.int32, logits.shape, 1)
    m1 = jnp.max(logits, axis=1, keepdims=True)
    i1 = jnp.min(jnp.where(logits == m1, lane, LANES), axis=1, keepdims=True)
    rest = jnp.where(lane == i1, 2.0 * NEG_BIG, logits)
    m2 = jnp.max(rest, axis=1, keepdims=True)
    i2 = jnp.min(jnp.where(rest == m2, lane, LANES), axis=1, keepdims=True)
    e21 = jnp.exp(m2 - m1)
    w1 = 1.0 / (1.0 + e21)
    w2 = e21 / (1.0 + e21)
    hot1 = lane == i1
    hot2 = lane == i2
    onehot = jnp.where(jnp.logical_or(hot1, hot2), 1.0, 0.0)
    before = _bdot(tri_ref[...], onehot.astype(BF16)) + run_ref[...]
    r1 = jnp.sum(jnp.where(hot1, before, 0.0), axis=1, keepdims=True)
    r2 = jnp.sum(jnp.where(hot2, before, 0.0), axis=1, keepdims=True)
    run = run_ref[...] + jnp.sum(onehot, axis=0, keepdims=True)
    run_ref[...] = run
    cnt_ref[...] = jnp.broadcast_to(run, cnt_ref.shape)
    fields = (i1.astype(F32), i2.astype(F32), r1, r2, w1, w2)
    route = jnp.zeros(logits.shape, F32)
    for j, f in enumerate(fields):
        route = jnp.where(lane == j, f, route)
    route_ref[...] = route


def _gmlp_router(x, mods, g_pre, w_in, g_v, b_v, w_s, b_s_b, w_out, g_post, g_ffn, w_r, b_r, tm):
    b, s, d = x.shape
    tok = lambda bi, i: (bi, i, 0)
    tri = jnp.tril(jnp.ones((tm, tm), F32), -1).astype(BF16)
    return pl.pallas_call(
        _gmlp_kernel,
        grid=(b, s // tm),
        in_specs=[pl.BlockSpec((None, tm, d), tok),
                  pl.BlockSpec((None, 6, d), lambda bi, i: (bi, 0, 0)),
                  _const_spec((1, d)),
                  _const_spec(w_in.shape),
                  _const_spec((1, d)),
                  _const_spec((1, d)),
                  _const_spec(w_s.shape),
                  _const_spec(b_s_b.shape),
                  _const_spec(w_out.shape),
                  _const_spec((1, d)),
                  _const_spec((1, d)),
                  _const_spec(w_r.shape),
                  _const_spec((1, LANES)),
                  _const_spec((tm, tm))],
        out_specs=[pl.BlockSpec((None, tm, d), tok),
                   pl.BlockSpec((None, tm, d), tok),
                   pl.BlockSpec((None, tm, LANES), tok),
                   pl.BlockSpec((8, LANES), lambda bi, i: (0, 0))],
        out_shape=[jax.ShapeDtypeStruct((b, s, d), F32),
                   jax.ShapeDtypeStruct((b, s, d), F32),
                   jax.ShapeDtypeStruct((b, s, LANES), F32),
                   jax.ShapeDtypeStruct((8, LANES), F32)],
        scratch_shapes=[pltpu.VMEM((1, LANES), F32)],
        compiler_params=_cparams(("arbitrary", "arbitrary")),
        name="gmlp_router",
    )(x, mods, g_pre, w_in, g_v, b_v, w_s, b_s_b, w_out, g_post, g_ffn, w_r, b_r, tri)


def _dispatch_kernel(slot_ref, h_ref, xs_in_ref, xs_ref, sem):
    del xs_in_ref
    tm = h_ref.shape[0]
    for r in range(tm):
        for k in range(2):
            dst = slot_ref[0, 0, 2 * r + k]
            pltpu.make_async_copy(h_ref.at[pl.ds(r, 1), :], xs_ref.at[pl.ds(dst, 1), :], sem).start()
    for _ in range(2):
        pltpu.make_async_copy(h_ref, xs_ref.at[pl.ds(0, tm), :], sem).wait()


def _dispatch(h, slots, n_slots, tm):
    n, d = h.shape
    nt = n // tm
    return pl.pallas_call(
        _dispatch_kernel,
        grid=(nt,),
        in_specs=[pl.BlockSpec((1, 1, 2 * tm), lambda i: (i, 0, 0), memory_space=pltpu.SMEM),
                  pl.BlockSpec((tm, d), lambda i: (i, 0)),
                  pl.BlockSpec(memory_space=pl.ANY)],
        out_specs=pl.BlockSpec(memory_space=pl.ANY),
        out_shape=jax.ShapeDtypeStruct((n_slots, d), F32),
        scratch_shapes=[pltpu.SemaphoreType.DMA(())],
        input_output_aliases={2: 0},
        compiler_params=_cparams(("arbitrary",)),
        name="dispatch",
    )(slots.reshape(nt, 1, 2 * tm), h, jnp.zeros((n_slots, d), F32))


def _expert_kernel(te_ref, na_ref, xs_ref, wgu_ref, wd_ref, ys_ref, xb_ref):
    t = pl.program_id(0)
    c = pl.program_id(1)
    active = t < na_ref[0]

    @pl.when(jnp.logical_and(active, c == 0))
    def _():
        xb_ref[...] = xs_ref[...].astype(BF16)

    @pl.when(active)
    def _():
        xb = xb_ref[...]
        acts = []
        for j in range(wd_ref.shape[0] // FF_SUB):
            gu = _bdot(xb, wgu_ref[:, 2 * FF_SUB * j:2 * FF_SUB * (j + 1)])
            acts.append((jax.nn.silu(gu[:, 0:FF_SUB]) * gu[:, FF_SUB:2 * FF_SUB]).astype(BF16))
        part = _bdot(jnp.concatenate(acts, axis=1), wd_ref[...])

        @pl.when(c == 0)
        def _():
            ys_ref[...] = part

        @pl.when(c > 0)
        def _():
            ys_ref[...] += part

    @pl.when(jnp.logical_and(jnp.logical_not(active), c == 0))
    def _():
        ys_ref[...] = jnp.zeros_like(ys_ref)


def _gate_up_kernel(wg_ref, wu_ref, o_ref):
    for j in range(wg_ref.shape[1] // FF_SUB):
        src = slice(FF_SUB * j, FF_SUB * (j + 1))
        o_ref[:, 2 * FF_SUB * j:2 * FF_SUB * j + FF_SUB] = wg_ref[:, src].astype(BF16)
        o_ref[:, 2 * FF_SUB * j + FF_SUB:2 * FF_SUB * (j + 1)] = wu_ref[:, src].astype(BF16)


def _interleave_gate_up(w_gate, w_up):
    ne, d, ff = w_gate.shape
    cols = 2 * FF_SUB
    spec = pl.BlockSpec((None, d, cols), lambda e, j: (e, 0, j))
    return pl.pallas_call(
        _gate_up_kernel,
        grid=(ne, ff // cols),
        in_specs=[spec, spec],
        out_specs=pl.BlockSpec((None, d, 2 * cols), lambda e, j: (e, 0, j)),
        out_shape=jax.ShapeDtypeStruct((ne, d, 2 * ff), BF16),
        compiler_params=_cparams(("parallel", "parallel")),
        name="gate_up_weights",
    )(w_gate, w_up)


def _experts(tile_expert, n_active, xs, w_gu, w_down, tm, chunk):
    n_slots, d = xs.shape
    ff = w_down.shape[1]
    n_tiles = n_slots // tm
    last = ff // chunk - 1

    def live(t, na):
        return jnp.minimum(t, na[0] - 1)

    def chunk_of(t, c, na):
        return jnp.where(t < na[0], c, last)

    grid_spec = pltpu.PrefetchScalarGridSpec(
        num_scalar_prefetch=2,
        grid=(n_tiles, ff // chunk),
        in_specs=[pl.BlockSpec((tm, d), lambda t, c, te, na: (live(t, na), 0)),
                  pl.BlockSpec((None, d, 2 * chunk), lambda t, c, te, na: (te[live(t, na)], 0, chunk_of(t, c, na))),
                  pl.BlockSpec((None, chunk, d), lambda t, c, te, na: (te[live(t, na)], chunk_of(t, c, na), 0))],
        out_specs=pl.BlockSpec((tm, d), lambda t, c, te, na: (t, 0)),
        scratch_shapes=[pltpu.VMEM((tm, d), BF16)],
    )
    return pl.pallas_call(
        _expert_kernel,
        grid_spec=grid_spec,
        out_shape=jax.ShapeDtypeStruct((n_slots, d), F32),
        compiler_params=_cparams(("arbitrary", "arbitrary")),
        name="experts",
    )(tile_expert, n_active, xs, w_gu, w_down)


def _combine_kernel(slot_ref, x_ref, mod_ref, route_ref, gpost_ref, ys_ref, o_ref, buf_ref, sem):
    i = pl.program_id(0)
    nt = pl.num_programs(0) - 1
    tm = x_ref.shape[0]

    @pl.when(i < nt)
    def _():
        b = i % 2
        for r in range(tm):
            for k in range(2):
                src = slot_ref[0, 0, 2 * r + k]
                pltpu.make_async_copy(ys_ref.at[pl.ds(src, 1), :], buf_ref.at[b, k, pl.ds(r, 1), :],
                                      sem.at[b]).start()

    @pl.when(i > 0)
    def _():
        b = (i - 1) % 2
        for k in range(2):
            pltpu.make_async_copy(ys_ref.at[pl.ds(0, tm), :], buf_ref.at[b, k], sem.at[b]).wait()
        route = route_ref[...]
        f = route[:, 4:5] * buf_ref[b, 0] + route[:, 5:6] * buf_ref[b, 1]
        mod = mod_ref[...]
        o_ref[...] = x_ref[...] + mod[5:6] * _rms(f, gpost_ref[...])


def _combine(x, mods, route, g_post, ys, slots, seq, tm):
    n, d = x.shape
    nt = n // tm
    per_batch = seq // tm
    done = lambda i: jnp.maximum(i - 1, 0)
    return pl.pallas_call(
        _combine_kernel,
        grid=(nt + 1,),
        in_specs=[pl.BlockSpec((1, 1, 2 * tm), lambda i: (jnp.minimum(i, nt - 1), 0, 0), memory_space=pltpu.SMEM),
                  pl.BlockSpec((tm, d), lambda i: (done(i), 0)),
                  pl.BlockSpec((None, 6, d), lambda i: (done(i) // per_batch, 0, 0)),
                  pl.BlockSpec((tm, LANES), lambda i: (done(i), 0)),
                  _const_spec((1, d)),
                  pl.BlockSpec(memory_space=pl.ANY)],
        out_specs=pl.BlockSpec((tm, d), lambda i: (done(i), 0)),
        out_shape=jax.ShapeDtypeStruct((n, d), F32),
        scratch_shapes=[pltpu.VMEM((2, 2, tm, d), F32), pltpu.SemaphoreType.DMA((2,))],
        compiler_params=_cparams(("arbitrary",)),
        name="combine",
    )(slots.reshape(nt, 1, 2 * tm), x, mods, route, g_post, ys)


def _rope_tables(n):
    axis_dim = HEAD_DIM // 2
    pos = jnp.arange(n, dtype=jnp.int32)
    r = (pos // GRID_W).astype(F32)[:, None]
    col = (pos % GRID_W).astype(F32)[:, None]
    inv = 1.0 / (ROPE_THETA ** (jnp.arange(0, axis_dim, 2, dtype=F32) / axis_dim))
    ang = jnp.concatenate([r * inv, col * inv], axis=-1)
    cos, sin = jnp.cos(ang), jnp.sin(ang)
    zero = jnp.zeros_like(sin)
    reps = LANES // HEAD_DIM
    return (jnp.tile(jnp.concatenate([cos, cos], -1), (1, reps)),
            jnp.tile(jnp.concatenate([-sin, zero], -1), (1, reps)),
            jnp.tile(jnp.concatenate([zero, sin], -1), (1, reps)))


def _head_mean_matrix(width):
    idx = jnp.arange(width) // HEAD_DIM
    return jnp.where(idx[:, None] == idx[None, :], 1.0 / HEAD_DIM, 0.0).astype(BF16)


def kernel(x, c, ctx, c_ctx, e_w_mod, e_b_mod, e_g_pre_mix, e_g_post_mix, e_w_in, e_g_q, e_g_k, e_w_conv, e_w_out, e_g_pre_ffn, e_g_post_ffn, e_w_gate, e_w_up, e_w_down, o_w_mod, o_b_mod, o_g_pre_mix, o_g_post_mix, o_w_in, o_g_v, o_b_v, o_w_s, o_b_s, o_w_out, o_g_pre_ffn, o_g_post_ffn, o_w_router, o_b_router, o_w_gate, o_w_up, o_w_down):
    b, s, d = x.shape
    n_ctx = ctx.shape[1]
    n = b * s
    tm = min(TOKEN_TILE, s)
    tq = min(ATTN_Q_TILE, s)
    assert b + 1 <= 8 and s % tm == 0 and s % tq == 0 and tm % CHUNK == 0 and n_ctx % 16 == 0
    assert (s // tm) % ATTN_BLOCKS_PER_ITER == 0 and FF_CHUNK_EXPERT % FF_SUB == 0
    assert e_w_mod.shape[0] == 1 and o_w_mod.shape[0] == 1
    row = lambda g: g.reshape(1, -1)

    cond8 = jnp.zeros((8, d), F32).at[:b].set(c).at[b].set(c_ctx)
    mods_e = _modulation(cond8, e_w_mod[0], e_b_mod[0])
    mods_o = _modulation(cond8, o_w_mod[0], o_b_mod[0])

    w_in = e_w_in[0].astype(BF16)
    gq = jnp.tile(e_g_q[0], N_HEADS).reshape(1, ATTN_W)
    gk = jnp.tile(e_g_k[0], N_KV_HEADS).reshape(1, KV_W)
    eq, ek = _head_mean_matrix(ATTN_W), _head_mean_matrix(KV_W)
    cos, sa, sb = _rope_tables(s)
    qt, k, vt4, bg, z = _inproj(x, mods_e, None, row(e_g_pre_mix[0]), w_in, gq, gk, eq, ek, cos, sa, sb, tm)
    ones = jnp.ones((n_ctx, LANES), F32)
    zeros = jnp.zeros((n_ctx, LANES), F32)
    _, kc, vct4, _, _ = _inproj(ctx, mods_e, b, row(e_g_pre_mix[0]), w_in, gq, gk, eq, ek, ones, zeros, zeros,
                                n_ctx)
    attn = _attention(qt, k.reshape(b, s // tm, tm, KV_W), vt4, kc.reshape(b, 1, n_ctx, KV_W), vct4, tq)
    x1 = _outproj(x, mods_e, attn, bg, z, e_w_conv[0], e_w_out[0].astype(BF16), row(e_g_post_mix[0]), tm)

    x2 = _swiglu(x1, mods_e, row(e_g_pre_ffn[0]), e_w_gate[0].astype(BF16), e_w_up[0].astype(BF16),
                 e_w_down[0].astype(BF16), row(e_g_post_ffn[0]), tm)

    b_s_b = jnp.broadcast_to(o_b_s[0][:, :, None], (GM_GROUPS, CHUNK, d // GM_GROUPS))
    w_r32 = jnp.zeros((d, LANES), F32).at[:, :N_EXPERTS].set(o_w_router[0])
    w_r_hi = w_r32.astype(BF16)
    w_r = jnp.stack([w_r_hi, (w_r32 - w_r_hi.astype(F32)).astype(BF16)])
    b_r = jnp.full((1, LANES), NEG_BIG, F32).at[0, :N_EXPERTS].set(o_b_router[0])
    x3, h3, route, counts = _gmlp_router(
        x2, mods_o, row(o_g_pre_mix[0]), o_w_in[0].astype(BF16), row(o_g_v[0]), row(o_b_v[0]),
        o_w_s[0].astype(BF16), b_s_b, o_w_out[0].astype(BF16), row(o_g_post_mix[0]), row(o_g_pre_ffn[0]),
        w_r, b_r, tm)

    te_rows = EXPERT_TILE
    n_tiles = -(-(2 * n + N_EXPERTS * (te_rows - 1)) // te_rows)
    n_slots = n_tiles * te_rows
    cnt = counts[0, :N_EXPERTS].astype(jnp.int32)
    tiles_per = (cnt + te_rows - 1) // te_rows
    tile_end = jnp.cumsum(tiles_per)
    base = (tile_end - tiles_per) * te_rows
    route2 = route.reshape(n, LANES)
    e12 = route2[:, 0:2].astype(jnp.int32)
    r12 = route2[:, 2:4].astype(jnp.int32)
    slots = (base[e12] + r12).reshape(-1)
    tile_expert = jnp.minimum(
        jnp.sum(jnp.arange(n_tiles, dtype=jnp.int32)[:, None] >= tile_end[None, :], axis=1),
        N_EXPERTS - 1).astype(jnp.int32)
    n_active = tile_end[-1:].astype(jnp.int32)

    rt = min(ROUTE_TILE, s)
    xs = _dispatch(h3.reshape(n, d), slots, n_slots, rt)
    w_gu = _interleave_gate_up(o_w_gate[0], o_w_up[0])
    ys = _experts(tile_expert, n_active, xs, w_gu, o_w_down[0].astype(BF16), te_rows, FF_CHUNK_EXPERT)
    out = _combine(x3.reshape(n, d), mods_o, route2, row(o_g_post_ffn[0]), ys, slots, s, rt)
    return out.reshape(b, s, d)
```

```python
import functools
import math

import jax
import jax.numpy as jnp
from jax import lax
from jax.experimental import pallas as pl
from jax.experimental.pallas import tpu as pltpu

F32 = jnp.float32
BF16 = jnp.bfloat16

EPS = 1e-6
GRID_W = 64
N_HEADS = 8
N_KV_HEADS = 2
HEAD_DIM = 64
ATTN_W = N_HEADS * HEAD_DIM
KV_W = N_KV_HEADS * HEAD_DIM
ROPE_THETA = 10000.0
CHUNK = 128
GM_GROUPS = 8
N_EXPERTS = 8
LANES = 128
LOG2E = 1.4426950408889634
NEG_BIG = -1e30

VMEM_LIMIT_BYTES = 56 * 1024 * 1024

TOKEN_TILE = 512
ATTN_Q_TILE = 512
ATTN_BLOCKS_PER_ITER = 2
ATTN_LOOKAHEAD = 2
ATTN_HEADROOM = 40.0
ATTN_BOUND_MARGIN = 1.05
ATTN_MIN_ROW_SUM = 2.0 ** -20
ATTN_MAX_ROW_SUM = 2.0 ** 100
FF_CHUNK_DENSE = 1408
FF_CHUNK_EXPERT = 1792
FF_SUB = 256
EXPERT_TILE = 1024
ROUTE_TILE = 256


def _cparams(sem):
    return pltpu.CompilerParams(dimension_semantics=sem, vmem_limit_bytes=VMEM_LIMIT_BYTES)


def _const_spec(shape):
    n = len(shape)
    return pl.BlockSpec(shape, lambda *_: (0,) * n, pipeline_mode=pl.Buffered(1))


def _rms(x, g):
    return x * lax.rsqrt(jnp.mean(x * x, axis=-1, keepdims=True) + EPS) * g


def _pre(x, g, shift, scale):
    return _rms(x, g) * (1.0 + scale) + shift


def _bdot(a, b):
    return jnp.dot(a, b, preferred_element_type=F32)


def _mod_kernel(c_ref, w_ref, b_ref, o_ref):
    a = jax.nn.silu(c_ref[...])
    o_ref[...] = jnp.dot(a, w_ref[...], precision=lax.Precision.HIGHEST,
                         preferred_element_type=F32) + b_ref[...]


def _modulation(cond8, w_mod, b_mod):
    d = cond8.shape[1]
    out = pl.pallas_call(
        _mod_kernel,
        grid=(6,),
        in_specs=[pl.BlockSpec((8, d), lambda j: (0, 0)),
                  pl.BlockSpec((d, d), lambda j: (0, j)),
                  pl.BlockSpec((1, d), lambda j: (0, j))],
        out_specs=pl.BlockSpec((8, d), lambda j: (0, j)),
        out_shape=jax.ShapeDtypeStruct((8, 6 * d), F32),
        compiler_params=_cparams(("arbitrary",)),
        name="modulation",
    )(cond8, w_mod, b_mod.reshape(1, 6 * d))
    return out.reshape(8, 6, d)


def _rope128(t, cos, sa, sb):
    return t * cos + pltpu.roll(t, 96, 1) * sa + pltpu.roll(t, 32, 1) * sb


def _inproj_kernel(x_ref, mod_ref, gpre_ref, w_ref, gq_ref, gk_ref, eq_ref, ek_ref,
                   cos_ref, sa_ref, sb_ref, qt_ref, k_ref, vt_ref, bg_ref, z_ref, *, q_scale):
    mod = mod_ref[...]
    h = _pre(x_ref[...], gpre_ref[...], mod[0:1], mod[1:2]).astype(BF16)
    cos, sa, sb = cos_ref[...], sa_ref[...], sb_ref[...]

    q = _bdot(h, w_ref[:, 0:ATTN_W])
    ms = _bdot((q * q).astype(BF16), eq_ref[...])
    qn = q * lax.rsqrt(ms + EPS) * gq_ref[...]
    qr = jnp.concatenate(
        [_rope128(qn[:, LANES * j:LANES * (j + 1)], cos, sa, sb) for j in range(ATTN_W // LANES)], axis=1)
    qt_ref[...] = (qr * q_scale).T.astype(BF16)

    k = _bdot(h, w_ref[:, ATTN_W:ATTN_W + KV_W])
    msk = _bdot((k * k).astype(BF16), ek_ref[...])
    kn = k * lax.rsqrt(msk + EPS) * gk_ref[...]
    k_ref[...] = _rope128(kn, cos, sa, sb).astype(BF16)

    v = _bdot(h, w_ref[:, ATTN_W + KV_W:ATTN_W + 2 * KV_W])
    vt_ref[...] = v.T.astype(BF16)

    o = ATTN_W + 2 * KV_W
    cw = (w_ref.shape[1] - o) // 3
    bg_ref[...] = _bdot(h, w_ref[:, o:o + cw]).astype(BF16)
    cg = _bdot(h, w_ref[:, o + cw:o + 2 * cw])
    hv = _bdot(h, w_ref[:, o + 2 * cw:o + 3 * cw])
    z_ref[...] = (cg * hv).astype(BF16)


def _inproj(x, mods, mod_row, g_pre, w_in, gq, gk, eq, ek, cos, sa, sb, tm):
    b, s, d = x.shape
    nt = s // tm
    cw = (w_in.shape[1] - ATTN_W - 2 * KV_W) // 3
    row = (lambda bi: bi) if mod_row is None else (lambda bi: mod_row)
    tok = lambda bi, i: (bi, i, 0)
    return pl.pallas_call(
        functools.partial(_inproj_kernel, q_scale=HEAD_DIM ** -0.5 * LOG2E),
        grid=(b, nt),
        in_specs=[pl.BlockSpec((None, tm, d), tok),
                  pl.BlockSpec((None, 6, d), lambda bi, i: (row(bi), 0, 0)),
                  _const_spec((1, d)),
                  _const_spec(w_in.shape),
                  _const_spec((1, ATTN_W)),
                  _const_spec((1, KV_W)),
                  _const_spec((ATTN_W, ATTN_W)),
                  _const_spec((KV_W, KV_W)),
                  pl.BlockSpec((tm, LANES), lambda bi, i: (i, 0)),
                  pl.BlockSpec((tm, LANES), lambda bi, i: (i, 0)),
                  pl.BlockSpec((tm, LANES), lambda bi, i: (i, 0))],
        out_specs=[pl.BlockSpec((None, ATTN_W, tm), lambda bi, i: (bi, 0, i)),
                   pl.BlockSpec((None, tm, KV_W), tok),
                   pl.BlockSpec((None, None, KV_W, tm), lambda bi, i: (bi, i, 0, 0)),
                   pl.BlockSpec((None, tm, cw), tok),
                   pl.BlockSpec((None, tm, cw), tok)],
        out_shape=[jax.ShapeDtypeStruct((b, ATTN_W, s), BF16),
                   jax.ShapeDtypeStruct((b, s, KV_W), BF16),
                   jax.ShapeDtypeStruct((b, nt, KV_W, tm), BF16),
                   jax.ShapeDtypeStruct((b, s, cw), BF16),
                   jax.ShapeDtypeStruct((b, s, cw), BF16)],
        compiler_params=_cparams(("parallel", "parallel")),
        name="inproj",
    )(x, mods, g_pre, w_in, gq, gk, eq, ek, cos, sa, sb)


def _attn_kernel(bound_ref, qt_ref, k_ref, vt_ref, kc_ref, vct_ref, o_ref, qp_ref, m_ref, acc_ref, ot_ref,
                 s_ref):
    n_kb = k_ref.shape[0]
    group = N_HEADS // N_KV_HEADS

    for h in range(N_HEADS):
        qh = qt_ref[HEAD_DIM * h:HEAD_DIM * (h + 1), :]
        zq = jnp.zeros_like(qh)
        qp_ref[h] = jnp.concatenate([qh, zq] if h // group == 0 else [zq, qh], axis=0)

    look = ATTN_LOOKAHEAD

    def values_aug(vtb, h):
        g = h // group
        return jnp.concatenate([vtb[HEAD_DIM * g:HEAD_DIM * (g + 1), :], jnp.ones((16, vtb.shape[1]), BF16)],
                               axis=0)

    def attend_exact(s, vtb, h, first):
        mb = jnp.max(s, axis=0, keepdims=True)
        m_old = None if first else m_ref[h:h + 1, :]
        m_new = mb if first else jnp.maximum(m_old, mb)
        p = jnp.exp2(s - m_new).astype(BF16)
        pv = _bdot(values_aug(vtb, h), p)
        acc_ref[h] = pv if first else acc_ref[h] * jnp.exp2(m_old - m_new) + pv
        m_ref[h:h + 1, :] = m_new

    def attend_stream(s, vtb, h, first):
        del first
        m_cur = m_ref[h:h + 1, :]
        p = jnp.exp2(s - m_cur).astype(BF16)
        pv = _bdot(values_aug(vtb, h), p)
        m_new = jnp.maximum(m_cur, jnp.max(s, axis=0, keepdims=True))
        acc_ref[h] = (acc_ref[h] + pv) * jnp.exp2(m_cur - m_new)
        m_ref[h:h + 1, :] = m_new

    def run(pending, blocks, kb_after, attend, first):
        n_steps = len(blocks) * N_HEADS
        for n in range(n_steps):
            s = pending.pop(0)
            ahead = n + look
            kb = blocks[ahead // N_HEADS][0] if ahead < n_steps else kb_after
            pending.append(_bdot(kb, qp_ref[ahead % N_HEADS]))
            attend(s, blocks[n // N_HEADS][1], n % N_HEADS, first)
        return pending

    def all_keys(attend):
        kc = kc_ref[0]
        pending = run([_bdot(kc, qp_ref[h]) for h in range(look)], [(kc, vct_ref[0])], k_ref[0], attend, True)
        for l in range(look):
            s_ref[l] = pending[l]

        def body(i, carry):
            first_blk = i * ATTN_BLOCKS_PER_ITER
            blocks = [(k_ref[first_blk + j], vt_ref[first_blk + j]) for j in range(ATTN_BLOCKS_PER_ITER)]
            kb_after = k_ref[jnp.minimum(first_blk + ATTN_BLOCKS_PER_ITER, n_kb - 1)]
            pending = run([s_ref[l] for l in range(look)], blocks, kb_after, attend, False)
            for l in range(look):
                s_ref[l] = pending[l]
            return carry

        lax.fori_loop(0, n_kb // ATTN_BLOCKS_PER_ITER, body, 0)

    m_ref[...] = jnp.full(m_ref.shape, bound_ref[0] - ATTN_HEADROOM, F32)
    acc_ref[...] = jnp.zeros(acc_ref.shape, F32)
    all_keys(attend_stream)
    sums = jnp.concatenate([acc_ref[h, HEAD_DIM:HEAD_DIM + 1, :] for h in range(N_HEADS)], axis=0)
    sound = jnp.logical_and(sums >= ATTN_MIN_ROW_SUM, sums <= ATTN_MAX_ROW_SUM)
    n_unsound = jnp.sum(jnp.where(sound, 0.0, 1.0))

    @pl.when(n_unsound > 0.0)
    def _():
        all_keys(attend_exact)

    for h in range(N_HEADS):
        acc = acc_ref[h]
        ot_ref[HEAD_DIM * h:HEAD_DIM * (h + 1), :] = acc[0:HEAD_DIM] / acc[HEAD_DIM:HEAD_DIM + 1]
    o_ref[...] = ot_ref[...].T.astype(BF16)


def _attention(score_bound, qt, k4, vt4, kc4, vct4, tq):
    b, _, s = qt.shape
    _, n_kb, tk, _ = k4.shape
    ctx = kc4.shape[2]
    return pl.pallas_call(
        _attn_kernel,
        grid=(b, s // tq),
        in_specs=[pl.BlockSpec(memory_space=pltpu.SMEM),
                  pl.BlockSpec((None, ATTN_W, tq), lambda bi, i: (bi, 0, i)),
                  pl.BlockSpec((None, n_kb, tk, KV_W), lambda bi, i: (bi, 0, 0, 0)),
                  pl.BlockSpec((None, n_kb, KV_W, tk), lambda bi, i: (bi, 0, 0, 0)),
                  pl.BlockSpec((None, 1, ctx, KV_W), lambda bi, i: (bi, 0, 0, 0)),
                  pl.BlockSpec((None, 1, KV_W, ctx), lambda bi, i: (bi, 0, 0, 0))],
        out_specs=pl.BlockSpec((None, tq, ATTN_W), lambda bi, i: (bi, i, 0)),
        out_shape=jax.ShapeDtypeStruct((b, s, ATTN_W), BF16),
        scratch_shapes=[pltpu.VMEM((N_HEADS, KV_W, tq), BF16),
                        pltpu.VMEM((N_HEADS, tq), F32),
                        pltpu.VMEM((N_HEADS, HEAD_DIM + 16, tq), F32),
                        pltpu.VMEM((ATTN_W, tq), F32),
                        pltpu.VMEM((ATTN_LOOKAHEAD, tk, tq), F32)],
        compiler_params=_cparams(("parallel", "parallel")),
        name="attention",
    )(score_bound, qt, k4, vt4, kc4, vct4)


def _outproj_kernel(x_ref, mod_ref, attn_ref, bg_ref, z_ref, zp_ref, zn_ref, wc_ref, wo_ref, gpost_ref,
                    o_ref):
    i = pl.program_id(1)
    nt = pl.num_programs(1)
    tm = z_ref.shape[0]
    z = z_ref[...].astype(F32)
    halo = zp_ref.shape[0]
    zprev = zp_ref[...].astype(F32)[halo - 1:halo, :] * (i > 0).astype(F32)
    znext = zn_ref[...].astype(F32)[0:1, :] * (i < nt - 1).astype(F32)
    row = lax.broadcasted_iota(jnp.int32, z.shape, 0)
    zm1 = jnp.where(row == 0, zprev, pltpu.roll(z, 1, 0))
    zp1 = jnp.where(row == tm - 1, znext, pltpu.roll(z, tm - 1, 0))
    wc = wc_ref[...]
    conv = bg_ref[...].astype(F32) * (wc[0:1] * zm1 + wc[1:2] * z + wc[2:3] * zp1)
    y = _bdot(attn_ref[...], wo_ref[0:ATTN_W, :]) + _bdot(conv.astype(BF16), wo_ref[ATTN_W:, :])
    mod = mod_ref[...]
    o_ref[...] = x_ref[...] + mod[2:3] * _rms(y, gpost_ref[...])


def _outproj(x, mods, attn, bg, z, w_conv, w_out, g_post, tm):
    b, s, d = x.shape
    cw = z.shape[2]
    halo = 16
    r = tm // halo
    last = s // halo - 1
    tok = lambda bi, i: (bi, i, 0)
    return pl.pallas_call(
        _outproj_kernel,
        grid=(b, s // tm),
        in_specs=[pl.BlockSpec((None, tm, d), tok),
                  pl.BlockSpec((None, 6, d), lambda bi, i: (bi, 0, 0)),
                  pl.BlockSpec((None, tm, ATTN_W), tok),
                  pl.BlockSpec((None, tm, cw), tok),
                  pl.BlockSpec((None, tm, cw), tok),
                  pl.BlockSpec((None, halo, cw), lambda bi, i: (bi, jnp.maximum(i * r - 1, 0), 0)),
                  pl.BlockSpec((None, halo, cw), lambda bi, i: (bi, jnp.minimum((i + 1) * r, last), 0)),
                  _const_spec(w_conv.shape),
                  _const_spec(w_out.shape),
                  _const_spec((1, d))],
        out_specs=pl.BlockSpec((None, tm, d), tok),
        out_shape=jax.ShapeDtypeStruct((b, s, d), F32),
        compiler_params=_cparams(("parallel", "parallel")),
        name="outproj",
    )(x, mods, attn, bg, z, z, z, w_conv, w_out, g_post)


def _swiglu_kernel(x_ref, mod_ref, gpre_ref, wg_ref, wu_ref, wd_ref, gpost_ref, o_ref, *, chunk):
    mod = mod_ref[...]
    x = x_ref[...]
    h = _pre(x, gpre_ref[...], mod[3:4], mod[4:5]).astype(BF16)
    acc = None
    for c in range(wg_ref.shape[1] // chunk):
        sl = slice(c * chunk, (c + 1) * chunk)
        a = (jax.nn.silu(_bdot(h, wg_ref[:, sl])) * _bdot(h, wu_ref[:, sl])).astype(BF16)
        part = _bdot(a, wd_ref[sl, :])
        acc = part if acc is None else acc + part
    o_ref[...] = x + mod[5:6] * _rms(acc, gpost_ref[...])


def _swiglu(x, mods, g_pre, w_gate, w_up, w_down, g_post, tm):
    b, s, d = x.shape
    tok = lambda bi, i: (bi, i, 0)
    return pl.pallas_call(
        functools.partial(_swiglu_kernel, chunk=FF_CHUNK_DENSE),
        grid=(b, s // tm),
        in_specs=[pl.BlockSpec((None, tm, d), tok),
                  pl.BlockSpec((None, 6, d), lambda bi, i: (bi, 0, 0)),
                  _const_spec((1, d)),
                  _const_spec(w_gate.shape),
                  _const_spec(w_up.shape),
                  _const_spec(w_down.shape),
                  _const_spec((1, d))],
        out_specs=pl.BlockSpec((None, tm, d), tok),
        out_shape=jax.ShapeDtypeStruct((b, s, d), F32),
        compiler_params=_cparams(("parallel", "parallel")),
        name="swiglu",
    )(x, mods, g_pre, w_gate, w_up, w_down, g_post)


def _gmlp_kernel(x_ref, mod_ref, gpre_ref, win_ref, gv_ref, bv_ref, ws_ref, bs_ref, wout_ref, gpost_ref,
                 gffn_ref, wr_ref, br_ref, tri_ref,
                 x3_ref, h3_ref, route_ref, cnt_ref, run_ref):
    first = jnp.logical_and(pl.program_id(0) == 0, pl.program_id(1) == 0)

    @pl.when(first)
    def _():
        run_ref[...] = jnp.zeros_like(run_ref)

    mod = mod_ref[...]
    x = x_ref[...]
    tm, d = x.shape
    h = _pre(x, gpre_ref[...], mod[0:1], mod[1:2]).astype(BF16)
    u = jax.nn.gelu(_bdot(h, win_ref[:, 0:d]))
    v = jax.nn.gelu(_bdot(h, win_ref[:, d:2 * d]))
    mu = jnp.mean(v, axis=-1, keepdims=True)
    vc = v - mu
    var = jnp.mean(vc * vc, axis=-1, keepdims=True)
    vn = (vc * lax.rsqrt(var + EPS) * gv_ref[...] + bv_ref[...]).astype(BF16)

    n_chunks = tm // CHUNK
    gc = d // GM_GROUPS
    mixed = []
    for g in range(GM_GROUPS):
        rhs = jnp.concatenate([vn[CHUNK * c:CHUNK * (c + 1), gc * g:gc * (g + 1)] for c in range(n_chunks)],
                              axis=1)
        bias = bs_ref[g]
        mixed.append(_bdot(ws_ref[g], rhs) + jnp.concatenate([bias] * n_chunks, axis=1))
    s = jnp.concatenate(
        [jnp.concatenate([mixed[g][:, gc * c:gc * (c + 1)] for g in range(GM_GROUPS)], axis=1)
         for c in range(n_chunks)], axis=0)
    y = _bdot((u * s).astype(BF16), wout_ref[...])
    x3 = x + mod[2:3] * _rms(y, gpost_ref[...])
    x3_ref[...] = x3

    h3 = _pre(x3, gffn_ref[...], mod[3:4], mod[4:5])
    h3_ref[...] = h3
    h_hi = h3.astype(BF16)
    h_lo = (h3 - h_hi.astype(F32)).astype(BF16)
    logits = (_bdot(h_hi, wr_ref[0]) + _bdot(h_lo, wr_ref[0]) + _bdot(h_hi, wr_ref[1])
              + br_ref[...])
    lane = lax.broadcasted_iota(jnp.int32, logits.shape, 1)
    m1 = jnp.max(logits, axis=1, keepdims=True)
    i1 = jnp.min(jnp.where(logits == m1, lane, LANES), axis=1, keepdims=True)
    rest = jnp.where(lane == i1, 2.0 * NEG_BIG, logits)
    m2 = jnp.max(rest, axis=1, keepdims=True)
    i2 = jnp.min(jnp.where(rest == m2, lane, LANES), axis=1, keepdims=True)
    e21 = jnp.exp(m2 - m1)
    w1 = 1.0 / (1.0 + e21)
    w2 = e21 / (1.0 + e21)
    hot1 = lane == i1
    hot2 = lane == i2
    onehot = jnp.where(jnp.logical_or(hot1, hot2), 1.0, 0.0)
    before = _bdot(tri_ref[...], onehot.astype(BF16)) + run_ref[...]
    r1 = jnp.sum(jnp.where(hot1, before, 0.0), axis=1, keepdims=True)
    r2 = jnp.sum(jnp.where(hot2, before, 0.0), axis=1, keepdims=True)
    run = run_ref[...] + jnp.sum(onehot, axis=0, keepdims=True)
    run_ref[...] = run
    cnt_ref[...] = jnp.broadcast_to(run, cnt_ref.shape)
    fields = (i1.astype(F32), i2.astype(F32), r1, r2, w1, w2)
    route = jnp.zeros(logits.shape, F32)
    for j, f in enumerate(fields):
        route = jnp.where(lane == j, f, route)
    route_ref[...] = route


def _gmlp_router(x, mods, g_pre, w_in, g_v, b_v, w_s, b_s_b, w_out, g_post, g_ffn, w_r, b_r, tm):
    b, s, d = x.shape
    tok = lambda bi, i: (bi, i, 0)
    tri = jnp.tril(jnp.ones((tm, tm), F32), -1).astype(BF16)
    return pl.pallas_call(
        _gmlp_kernel,
        grid=(b, s // tm),
        in_specs=[pl.BlockSpec((None, tm, d), tok),
                  pl.BlockSpec((None, 6, d), lambda bi, i: (bi, 0, 0)),
                  _const_spec((1, d)),
                  _const_spec(w_in.shape),
                  _const_spec((1, d)),
                  _const_spec((1, d)),
                  _const_spec(w_s.shape),
                  _const_spec(b_s_b.shape),
                  _const_spec(w_out.shape),
                  _const_spec((1, d)),
                  _const_spec((1, d)),
                  _const_spec(w_r.shape),
                  _const_spec((1, LANES)),
                  _const_spec((tm, tm))],
        out_specs=[pl.BlockSpec((None, tm, d), tok),
                   pl.BlockSpec((None, tm, d), tok),
                   pl.BlockSpec((None, tm, LANES), tok),
                   pl.BlockSpec((8, LANES), lambda bi, i: (0, 0))],
        out_shape=[jax.ShapeDtypeStruct((b, s, d), F32),
                   jax.ShapeDtypeStruct((b, s, d), F32),
                   jax.ShapeDtypeStruct((b, s, LANES), F32),
                   jax.ShapeDtypeStruct((8, LANES), F32)],
        scratch_shapes=[pltpu.VMEM((1, LANES), F32)],
        compiler_params=_cparams(("arbitrary", "arbitrary")),
        name="gmlp_router",
    )(x, mods, g_pre, w_in, g_v, b_v, w_s, b_s_b, w_out, g_post, g_ffn, w_r, b_r, tri)


def _dispatch_kernel(slot_ref, h_ref, xs_in_ref, xs_ref, sem):
    del xs_in_ref
    tm = h_ref.shape[0]
    for r in range(tm):
        for k in range(2):
            dst = slot_ref[0, 0, 2 * r + k]
            pltpu.make_async_copy(h_ref.at[pl.ds(r, 1), :], xs_ref.at[pl.ds(dst, 1), :], sem).start()
    for _ in range(2):
        pltpu.make_async_copy(h_ref, xs_ref.at[pl.ds(0, tm), :], sem).wait()


def _dispatch(h, slots, n_slots, tm):
    n, d = h.shape
    nt = n // tm
    return pl.pallas_call(
        _dispatch_kernel,
        grid=(nt,),
        in_specs=[pl.BlockSpec((1, 1, 2 * tm), lambda i: (i, 0, 0), memory_space=pltpu.SMEM),
                  pl.BlockSpec((tm, d), lambda i: (i, 0)),
                  pl.BlockSpec(memory_space=pl.ANY)],
        out_specs=pl.BlockSpec(memory_space=pl.ANY),
        out_shape=jax.ShapeDtypeStruct((n_slots, d), F32),
        scratch_shapes=[pltpu.SemaphoreType.DMA(())],
        input_output_aliases={2: 0},
        compiler_params=_cparams(("arbitrary",)),
        name="dispatch",
    )(slots.reshape(nt, 1, 2 * tm), h, jnp.zeros((n_slots, d), F32))


def _expert_kernel(te_ref, na_ref, xs_ref, wgu_ref, wd_ref, ys_ref, xb_ref):
    t = pl.program_id(0)
    c = pl.program_id(1)
    active = t < na_ref[0]

    @pl.when(jnp.logical_and(active, c == 0))
    def _():
        xb_ref[...] = xs_ref[...].astype(BF16)

    @pl.when(active)
    def _():
        xb = xb_ref[...]
        acts = []
        for j in range(wd_ref.shape[0] // FF_SUB):
            gu = _bdot(xb, wgu_ref[:, 2 * FF_SUB * j:2 * FF_SUB * (j + 1)])
            acts.append((jax.nn.silu(gu[:, 0:FF_SUB]) * gu[:, FF_SUB:2 * FF_SUB]).astype(BF16))
        part = _bdot(jnp.concatenate(acts, axis=1), wd_ref[...])

        @pl.when(c == 0)
        def _():
            ys_ref[...] = part

        @pl.when(c > 0)
        def _():
            ys_ref[...] += part

    @pl.when(jnp.logical_and(jnp.logical_not(active), c == 0))
    def _():
        ys_ref[...] = jnp.zeros_like(ys_ref)


def _gate_up_kernel(wg_ref, wu_ref, o_ref):
    for j in range(wg_ref.shape[1] // FF_SUB):
        src = slice(FF_SUB * j, FF_SUB * (j + 1))
        o_ref[:, 2 * FF_SUB * j:2 * FF_SUB * j + FF_SUB] = wg_ref[:, src].astype(BF16)
        o_ref[:, 2 * FF_SUB * j + FF_SUB:2 * FF_SUB * (j + 1)] = wu_ref[:, src].astype(BF16)


def _interleave_gate_up(w_gate, w_up):
    ne, d, ff = w_gate.shape
    cols = 2 * FF_SUB
    spec = pl.BlockSpec((None, d, cols), lambda e, j: (e, 0, j))
    return pl.pallas_call(
        _gate_up_kernel,
        grid=(ne, ff // cols),
        in_specs=[spec, spec],
        out_specs=pl.BlockSpec((None, d, 2 * cols), lambda e, j: (e, 0, j)),
        out_shape=jax.ShapeDtypeStruct((ne, d, 2 * ff), BF16),
        compiler_params=_cparams(("parallel", "parallel")),
        name="gate_up_weights",
    )(w_gate, w_up)


def _experts(tile_expert, n_active, xs, w_gu, w_down, tm, chunk):
    n_slots, d = xs.shape
    ff = w_down.shape[1]
    n_tiles = n_slots // tm
    last = ff // chunk - 1

    def live(t, na):
        return jnp.minimum(t, na[0] - 1)

    def chunk_of(t, c, na):
        return jnp.where(t < na[0], c, last)

    grid_spec = pltpu.PrefetchScalarGridSpec(
        num_scalar_prefetch=2,
        grid=(n_tiles, ff // chunk),
        in_specs=[pl.BlockSpec((tm, d), lambda t, c, te, na: (live(t, na), 0)),
                  pl.BlockSpec((None, d, 2 * chunk), lambda t, c, te, na: (te[live(t, na)], 0, chunk_of(t, c, na))),
                  pl.BlockSpec((None, chunk, d), lambda t, c, te, na: (te[live(t, na)], chunk_of(t, c, na), 0))],
        out_specs=pl.BlockSpec((tm, d), lambda t, c, te, na: (t, 0)),
        scratch_shapes=[pltpu.VMEM((tm, d), BF16)],
    )
    return pl.pallas_call(
        _expert_kernel,
        grid_spec=grid_spec,
        out_shape=jax.ShapeDtypeStruct((n_slots, d), F32),
        compiler_params=_cparams(("arbitrary", "arbitrary")),
        name="experts",
    )(tile_expert, n_active, xs, w_gu, w_down)


def _combine_kernel(slot_ref, x_ref, mod_ref, route_ref, gpost_ref, ys_ref, o_ref, buf_ref, sem):
    i = pl.program_id(0)
    nt = pl.num_programs(0) - 1
    tm = x_ref.shape[0]

    @pl.when(i < nt)
    def _():
        b = i % 2
        for r in range(tm):
            for k in range(2):
                src = slot_ref[0, 0, 2 * r + k]
                pltpu.make_async_copy(ys_ref.at[pl.ds(src, 1), :], buf_ref.at[b, k, pl.ds(r, 1), :],
                                      sem.at[b]).start()

    @pl.when(i > 0)
    def _():
        b = (i - 1) % 2
        for k in range(2):
            pltpu.make_async_copy(ys_ref.at[pl.ds(0, tm), :], buf_ref.at[b, k], sem.at[b]).wait()
        route = route_ref[...]
        f = route[:, 4:5] * buf_ref[b, 0] + route[:, 5:6] * buf_ref[b, 1]
        mod = mod_ref[...]
        o_ref[...] = x_ref[...] + mod[5:6] * _rms(f, gpost_ref[...])


def _combine(x, mods, route, g_post, ys, slots, seq, tm):
    n, d = x.shape
    nt = n // tm
    per_batch = seq // tm
    done = lambda i: jnp.maximum(i - 1, 0)
    return pl.pallas_call(
        _combine_kernel,
        grid=(nt + 1,),
        in_specs=[pl.BlockSpec((1, 1, 2 * tm), lambda i: (jnp.minimum(i, nt - 1), 0, 0), memory_space=pltpu.SMEM),
                  pl.BlockSpec((tm, d), lambda i: (done(i), 0)),
                  pl.BlockSpec((None, 6, d), lambda i: (done(i) // per_batch, 0, 0)),
                  pl.BlockSpec((tm, LANES), lambda i: (done(i), 0)),
                  _const_spec((1, d)),
                  pl.BlockSpec(memory_space=pl.ANY)],
        out_specs=pl.BlockSpec((tm, d), lambda i: (done(i), 0)),
        out_shape=jax.ShapeDtypeStruct((n, d), F32),
        scratch_shapes=[pltpu.VMEM((2, 2, tm, d), F32), pltpu.SemaphoreType.DMA((2,))],
        compiler_params=_cparams(("arbitrary",)),
        name="combine",
    )(slots.reshape(nt, 1, 2 * tm), x, mods, route, g_post, ys)


def _rope_tables(n):
    axis_dim = HEAD_DIM // 2
    pos = jnp.arange(n, dtype=jnp.int32)
    r = (pos // GRID_W).astype(F32)[:, None]
    col = (pos % GRID_W).astype(F32)[:, None]
    inv = 1.0 / (ROPE_THETA ** (jnp.arange(0, axis_dim, 2, dtype=F32) / axis_dim))
    ang = jnp.concatenate([r * inv, col * inv], axis=-1)
    cos, sin = jnp.cos(ang), jnp.sin(ang)
    zero = jnp.zeros_like(sin)
    reps = LANES // HEAD_DIM
    return (jnp.tile(jnp.concatenate([cos, cos], -1), (1, reps)),
            jnp.tile(jnp.concatenate([-sin, zero], -1), (1, reps)),
            jnp.tile(jnp.concatenate([zero, sin], -1), (1, reps)))


def _head_mean_matrix(width):
    idx = jnp.arange(width) // HEAD_DIM
    return jnp.where(idx[:, None] == idx[None, :], 1.0 / HEAD_DIM, 0.0).astype(BF16)


def kernel(x, c, ctx, c_ctx, e_w_mod, e_b_mod, e_g_pre_mix, e_g_post_mix, e_w_in, e_g_q, e_g_k, e_w_conv, e_w_out, e_g_pre_ffn, e_g_post_ffn, e_w_gate, e_w_up, e_w_down, o_w_mod, o_b_mod, o_g_pre_mix, o_g_post_mix, o_w_in, o_g_v, o_b_v, o_w_s, o_b_s, o_w_out, o_g_pre_ffn, o_g_post_ffn, o_w_router, o_b_router, o_w_gate, o_w_up, o_w_down):
    b, s, d = x.shape
    n_ctx = ctx.shape[1]
    n = b * s
    tm = min(TOKEN_TILE, s)
    tq = min(ATTN_Q_TILE, s)
    assert b + 1 <= 8 and s % tm == 0 and s % tq == 0 and tm % CHUNK == 0 and n_ctx % 16 == 0
    assert (s // tm) % ATTN_BLOCKS_PER_ITER == 0 and FF_CHUNK_EXPERT % FF_SUB == 0
    assert e_w_mod.shape[0] == 1 and o_w_mod.shape[0] == 1
    row = lambda g: g.reshape(1, -1)

    cond8 = jnp.zeros((8, d), F32).at[:b].set(c).at[b].set(c_ctx)
    mods_e = _modulation(cond8, e_w_mod[0], e_b_mod[0])
    mods_o = _modulation(cond8, o_w_mod[0], o_b_mod[0])

    w_in = e_w_in[0].astype(BF16)
    gq = jnp.tile(e_g_q[0], N_HEADS).reshape(1, ATTN_W)
    gk = jnp.tile(e_g_k[0], N_KV_HEADS).reshape(1, KV_W)
    eq, ek = _head_mean_matrix(ATTN_W), _head_mean_matrix(KV_W)
    cos, sa, sb = _rope_tables(s)
    qt, k, vt4, bg, z = _inproj(x, mods_e, None, row(e_g_pre_mix[0]), w_in, gq, gk, eq, ek, cos, sa, sb, tm)
    ones = jnp.ones((n_ctx, LANES), F32)
    zeros = jnp.zeros((n_ctx, LANES), F32)
    _, kc, vct4, _, _ = _inproj(ctx, mods_e, b, row(e_g_pre_mix[0]), w_in, gq, gk, eq, ek, ones, zeros, zeros,
                                n_ctx)
    score_bound = (ATTN_BOUND_MARGIN * HEAD_DIM ** 0.5 * LOG2E * jnp.max(jnp.abs(e_g_q[0]))
                   * jnp.max(jnp.abs(e_g_k[0]))).reshape(1).astype(F32)
    attn = _attention(score_bound, qt, k.reshape(b, s // tm, tm, KV_W), vt4, kc.reshape(b, 1, n_ctx, KV_W), vct4, tq)
    x1 = _outproj(x, mods_e, attn, bg, z, e_w_conv[0], e_w_out[0].astype(BF16), row(e_g_post_mix[0]), tm)

    x2 = _swiglu(x1, mods_e, row(e_g_pre_ffn[0]), e_w_gate[0].astype(BF16), e_w_up[0].astype(BF16),
                 e_w_down[0].astype(BF16), row(e_g_post_ffn[0]), tm)

    b_s_b = jnp.broadcast_to(o_b_s[0][:, :, None], (GM_GROUPS, CHUNK, d // GM_GROUPS))
    w_r32 = jnp.zeros((d, LANES), F32).at[:, :N_EXPERTS].set(o_w_router[0])
    w_r_hi = w_r32.astype(BF16)
    w_r = jnp.stack([w_r_hi, (w_r32 - w_r_hi.astype(F32)).astype(BF16)])
    b_r = jnp.full((1, LANES), NEG_BIG, F32).at[0, :N_EXPERTS].set(o_b_router[0])
    x3, h3, route, counts = _gmlp_router(
        x2, mods_o, row(o_g_pre_mix[0]), o_w_in[0].astype(BF16), row(o_g_v[0]), row(o_b_v[0]),
        o_w_s[0].astype(BF16), b_s_b, o_w_out[0].astype(BF16), row(o_g_post_mix[0]), row(o_g_pre_ffn[0]),
        w_r, b_r, tm)

    te_rows = EXPERT_TILE
    n_tiles = -(-(2 * n + N_EXPERTS * (te_rows - 1)) // te_rows)
    n_slots = n_tiles * te_rows
    cnt = counts[0, :N_EXPERTS].astype(jnp.int32)
    tiles_per = (cnt + te_rows - 1) // te_rows
    tile_end = jnp.cumsum(tiles_per)
    base = (tile_end - tiles_per) * te_rows
    route2 = route.reshape(n, LANES)
    e12 = route2[:, 0:2].astype(jnp.int32)
    r12 = route2[:, 2:4].astype(jnp.int32)
    slots = (base[e12] + r12).reshape(-1)
    tile_expert = jnp.minimum(
        jnp.sum(jnp.arange(n_tiles, dtype=jnp.int32)[:, None] >= tile_end[None, :], axis=1),
        N_EXPERTS - 1).astype(jnp.int32)
    n_active = tile_end[-1:].astype(jnp.int32)

    rt = min(ROUTE_TILE, s)
    xs = _dispatch(h3.reshape(n, d), slots, n_slots, rt)
    w_gu = _interleave_gate_up(o_w_gate[0], o_w_up[0])
    ys = _experts(tile_expert, n_active, xs, w_gu, o_w_down[0].astype(BF16), te_rows, FF_CHUNK_EXPERT)
    out = _combine(x3.reshape(n, d), mods_o, route2, row(o_g_post_ffn[0]), ys, slots, s, rt)
    return out.reshape(b, s, d)
```

```python
import functools
import math

import jax
import jax.numpy as jnp
from jax import lax
from jax.experimental import pallas as pl
from jax.experimental.pallas import tpu as pltpu

F32 = jnp.float32
BF16 = jnp.bfloat16

EPS = 1e-6
GRID_W = 64
N_HEADS = 8
N_KV_HEADS = 2
HEAD_DIM = 64
ATTN_W = N_HEADS * HEAD_DIM
KV_W = N_KV_HEADS * HEAD_DIM
ROPE_THETA = 10000.0
CHUNK = 128
GM_GROUPS = 8
N_EXPERTS = 8
LANES = 128
LOG2E = 1.4426950408889634
NEG_BIG = -1e30

VMEM_LIMIT_BYTES = 56 * 1024 * 1024

TOKEN_TILE = 512
ATTN_Q_TILE = 512
ATTN_BLOCKS_PER_ITER = 2
ATTN_LOOKAHEAD = 2
ATTN_HEADROOM = 40.0
ATTN_BOUND_MARGIN = 1.05
ATTN_MIN_ROW_SUM = 2.0 ** -20
ATTN_MAX_ROW_SUM = 2.0 ** 100
FF_CHUNK_DENSE = 1408
FF_CHUNK_EXPERT = 1792
FF_SUB = 256
EXPERT_TILE = 1024
ROUTE_TILE = 256


def _cparams(sem):
    return pltpu.CompilerParams(dimension_semantics=sem, vmem_limit_bytes=VMEM_LIMIT_BYTES)


def _const_spec(shape):
    n = len(shape)
    return pl.BlockSpec(shape, lambda *_: (0,) * n, pipeline_mode=pl.Buffered(1))


def _rms(x, g):
    return x * lax.rsqrt(jnp.mean(x * x, axis=-1, keepdims=True) + EPS) * g


def _pre(x, g, shift, scale):
    return _rms(x, g) * (1.0 + scale) + shift


def _bdot(a, b):
    return jnp.dot(a, b, preferred_element_type=F32)


def _mod_kernel(c_ref, w_ref, b_ref, o_ref):
    a = jax.nn.silu(c_ref[...])
    o_ref[...] = jnp.dot(a, w_ref[...], precision=lax.Precision.HIGHEST,
                         preferred_element_type=F32) + b_ref[...]


def _modulation(cond8, w_mod, b_mod):
    d = cond8.shape[1]
    out = pl.pallas_call(
        _mod_kernel,
        grid=(6,),
        in_specs=[pl.BlockSpec((8, d), lambda j: (0, 0)),
                  pl.BlockSpec((d, d), lambda j: (0, j)),
                  pl.BlockSpec((1, d), lambda j: (0, j))],
        out_specs=pl.BlockSpec((8, d), lambda j: (0, j)),
        out_shape=jax.ShapeDtypeStruct((8, 6 * d), F32),
        compiler_params=_cparams(("arbitrary",)),
        name="modulation",
    )(cond8, w_mod, b_mod.reshape(1, 6 * d))
    return out.reshape(8, 6, d)


def _rope128(t, cos, sa, sb):
    return t * cos + pltpu.roll(t, 96, 1) * sa + pltpu.roll(t, 32, 1) * sb


def _inproj_kernel(x_ref, mod_ref, gpre_ref, w_ref, gq_ref, gk_ref, eq_ref, ek_ref,
                   cos_ref, sa_ref, sb_ref, qt_ref, k_ref, vt_ref, bg_ref, z_ref, *, q_scale):
    mod = mod_ref[...]
    h = _pre(x_ref[...], gpre_ref[...], mod[0:1], mod[1:2]).astype(BF16)
    cos, sa, sb = cos_ref[...], sa_ref[...], sb_ref[...]

    q = _bdot(h, w_ref[:, 0:ATTN_W])
    ms = _bdot((q * q).astype(BF16), eq_ref[...])
    qn = q * lax.rsqrt(ms + EPS) * gq_ref[...]
    qr = jnp.concatenate(
        [_rope128(qn[:, LANES * j:LANES * (j + 1)], cos, sa, sb) for j in range(ATTN_W // LANES)], axis=1)
    qt_ref[...] = (qr * q_scale).T.astype(BF16)

    k = _bdot(h, w_ref[:, ATTN_W:ATTN_W + KV_W])
    msk = _bdot((k * k).astype(BF16), ek_ref[...])
    kn = k * lax.rsqrt(msk + EPS) * gk_ref[...]
    k_ref[...] = _rope128(kn, cos, sa, sb).astype(BF16)

    v = _bdot(h, w_ref[:, ATTN_W + KV_W:ATTN_W + 2 * KV_W])
    vt_ref[...] = v.T.astype(BF16)

    o = ATTN_W + 2 * KV_W
    cw = (w_ref.shape[1] - o) // 3
    bg_ref[...] = _bdot(h, w_ref[:, o:o + cw]).astype(BF16)
    cg = _bdot(h, w_ref[:, o + cw:o + 2 * cw])
    hv = _bdot(h, w_ref[:, o + 2 * cw:o + 3 * cw])
    z_ref[...] = (cg * hv).astype(BF16)


def _inproj(x, mods, mod_row, g_pre, w_in, gq, gk, eq, ek, cos, sa, sb, tm):
    b, s, d = x.shape
    nt = s // tm
    cw = (w_in.shape[1] - ATTN_W - 2 * KV_W) // 3
    row = (lambda bi: bi) if mod_row is None else (lambda bi: mod_row)
    tok = lambda bi, i: (bi, i, 0)
    return pl.pallas_call(
        functools.partial(_inproj_kernel, q_scale=HEAD_DIM ** -0.5 * LOG2E),
        grid=(b, nt),
        in_specs=[pl.BlockSpec((None, tm, d), tok),
                  pl.BlockSpec((None, 6, d), lambda bi, i: (row(bi), 0, 0)),
                  _const_spec((1, d)),
                  _const_spec(w_in.shape),
                  _const_spec((1, ATTN_W)),
                  _const_spec((1, KV_W)),
                  _const_spec((ATTN_W, ATTN_W)),
                  _const_spec((KV_W, KV_W)),
                  pl.BlockSpec((tm, LANES), lambda bi, i: (i, 0)),
                  pl.BlockSpec((tm, LANES), lambda bi, i: (i, 0)),
                  pl.BlockSpec((tm, LANES), lambda bi, i: (i, 0))],
        out_specs=[pl.BlockSpec((None, ATTN_W, tm), lambda bi, i: (bi, 0, i)),
                   pl.BlockSpec((None, tm, KV_W), tok),
                   pl.BlockSpec((None, None, KV_W, tm), lambda bi, i: (bi, i, 0, 0)),
                   pl.BlockSpec((None, tm, cw), tok),
                   pl.BlockSpec((None, tm, cw), tok)],
        out_shape=[jax.ShapeDtypeStruct((b, ATTN_W, s), BF16),
                   jax.ShapeDtypeStruct((b, s, KV_W), BF16),
                   jax.ShapeDtypeStruct((b, nt, KV_W, tm), BF16),
                   jax.ShapeDtypeStruct((b, s, cw), BF16),
                   jax.ShapeDtypeStruct((b, s, cw), BF16)],
        compiler_params=_cparams(("parallel", "parallel")),
        name="inproj",
    )(x, mods, g_pre, w_in, gq, gk, eq, ek, cos, sa, sb)


def _attn_kernel(bound_ref, qt_ref, k_ref, vt_ref, kc_ref, vct_ref, o_ref, qp_ref, m_ref, acc_ref, ot_ref,
                 s_ref):
    n_kb = k_ref.shape[0]
    group = N_HEADS // N_KV_HEADS

    for h in range(N_HEADS):
        qh = qt_ref[HEAD_DIM * h:HEAD_DIM * (h + 1), :]
        zq = jnp.zeros_like(qh)
        qp_ref[h] = jnp.concatenate([qh, zq] if h // group == 0 else [zq, qh], axis=0)

    look = ATTN_LOOKAHEAD

    def values_aug(vtb, h):
        g = h // group
        return jnp.concatenate([vtb[HEAD_DIM * g:HEAD_DIM * (g + 1), :], jnp.ones((16, vtb.shape[1]), BF16)],
                               axis=0)

    def attend_exact(s, vtb, h, first):
        mb = jnp.max(s, axis=0, keepdims=True)
        m_old = None if first else m_ref[h:h + 1, :]
        m_new = mb if first else jnp.maximum(m_old, mb)
        p = jnp.exp2(s - m_new).astype(BF16)
        pv = _bdot(values_aug(vtb, h), p)
        acc_ref[h] = pv if first else acc_ref[h] * jnp.exp2(m_old - m_new) + pv
        m_ref[h:h + 1, :] = m_new

    def attend_stream(s, vtb, h, first):
        del first
        m_cur = m_ref[h:h + 1, :]
        p = jnp.exp2((s - m_cur).astype(BF16))
        pv = _bdot(values_aug(vtb, h), p)
        m_new = jnp.maximum(m_cur, jnp.max(s, axis=0, keepdims=True))
        acc_ref[h] = (acc_ref[h] + pv) * jnp.exp2(m_cur - m_new)
        m_ref[h:h + 1, :] = m_new

    def run(pending, blocks, kb_after, attend, first):
        n_steps = len(blocks) * N_HEADS
        for n in range(n_steps):
            s = pending.pop(0)
            ahead = n + look
            kb = blocks[ahead // N_HEADS][0] if ahead < n_steps else kb_after
            pending.append(_bdot(kb, qp_ref[ahead % N_HEADS]))
            attend(s, blocks[n // N_HEADS][1], n % N_HEADS, first)
        return pending

    def all_keys(attend, floor):
        kc = kc_ref[0]
        pending = run([_bdot(kc, qp_ref[h]) for h in range(look)], [(kc, vct_ref[0])], k_ref[0], attend_exact,
                      True)
        for l in range(look):
            s_ref[l] = pending[l]
        if floor is not None:
            for h in range(N_HEADS):
                m_cur = m_ref[h:h + 1, :]
                m_new = jnp.maximum(m_cur, floor)
                acc_ref[h] = acc_ref[h] * jnp.exp2(m_cur - m_new)
                m_ref[h:h + 1, :] = m_new

        def body(i, carry):
            first_blk = i * ATTN_BLOCKS_PER_ITER
            blocks = [(k_ref[first_blk + j], vt_ref[first_blk + j]) for j in range(ATTN_BLOCKS_PER_ITER)]
            kb_after = k_ref[jnp.minimum(first_blk + ATTN_BLOCKS_PER_ITER, n_kb - 1)]
            pending = run([s_ref[l] for l in range(look)], blocks, kb_after, attend, False)
            for l in range(look):
                s_ref[l] = pending[l]
            return carry

        lax.fori_loop(0, n_kb // ATTN_BLOCKS_PER_ITER, body, 0)

    all_keys(attend_stream, bound_ref[0] - ATTN_HEADROOM)
    sums = jnp.concatenate([acc_ref[h, HEAD_DIM:HEAD_DIM + 1, :] for h in range(N_HEADS)], axis=0)
    sound = jnp.logical_and(sums >= ATTN_MIN_ROW_SUM, sums <= ATTN_MAX_ROW_SUM)
    n_unsound = jnp.sum(jnp.where(sound, 0.0, 1.0))

    @pl.when(n_unsound > 0.0)
    def _():
        all_keys(attend_exact, None)

    for h in range(N_HEADS):
        acc = acc_ref[h]
        ot_ref[HEAD_DIM * h:HEAD_DIM * (h + 1), :] = acc[0:HEAD_DIM] / acc[HEAD_DIM:HEAD_DIM + 1]
    o_ref[...] = ot_ref[...].T.astype(BF16)


def _attention(score_bound, qt, k4, vt4, kc4, vct4, tq):
    b, _, s = qt.shape
    _, n_kb, tk, _ = k4.shape
    ctx = kc4.shape[2]
    return pl.pallas_call(
        _attn_kernel,
        grid=(b, s // tq),
        in_specs=[pl.BlockSpec(memory_space=pltpu.SMEM),
                  pl.BlockSpec((None, ATTN_W, tq), lambda bi, i: (bi, 0, i)),
                  pl.BlockSpec((None, n_kb, tk, KV_W), lambda bi, i: (bi, 0, 0, 0)),
                  pl.BlockSpec((None, n_kb, KV_W, tk), lambda bi, i: (bi, 0, 0, 0)),
                  pl.BlockSpec((None, 1, ctx, KV_W), lambda bi, i: (bi, 0, 0, 0)),
                  pl.BlockSpec((None, 1, KV_W, ctx), lambda bi, i: (bi, 0, 0, 0))],
        out_specs=pl.BlockSpec((None, tq, ATTN_W), lambda bi, i: (bi, i, 0)),
        out_shape=jax.ShapeDtypeStruct((b, s, ATTN_W), BF16),
        scratch_shapes=[pltpu.VMEM((N_HEADS, KV_W, tq), BF16),
                        pltpu.VMEM((N_HEADS, tq), F32),
                        pltpu.VMEM((N_HEADS, HEAD_DIM + 16, tq), F32),
                        pltpu.VMEM((ATTN_W, tq), F32),
                        pltpu.VMEM((ATTN_LOOKAHEAD, tk, tq), F32)],
        compiler_params=_cparams(("parallel", "parallel")),
        name="attention",
    )(score_bound, qt, k4, vt4, kc4, vct4)


def _outproj_kernel(x_ref, mod_ref, attn_ref, bg_ref, z_ref, zp_ref, zn_ref, wc_ref, wo_ref, gpost_ref,
                    o_ref):
    i = pl.program_id(1)
    nt = pl.num_programs(1)
    tm = z_ref.shape[0]
    z = z_ref[...].astype(F32)
    halo = zp_ref.shape[0]
    zprev = zp_ref[...].astype(F32)[halo - 1:halo, :] * (i > 0).astype(F32)
    znext = zn_ref[...].astype(F32)[0:1, :] * (i < nt - 1).astype(F32)
    row = lax.broadcasted_iota(jnp.int32, z.shape, 0)
    zm1 = jnp.where(row == 0, zprev, pltpu.roll(z, 1, 0))
    zp1 = jnp.where(row == tm - 1, znext, pltpu.roll(z, tm - 1, 0))
    wc = wc_ref[...]
    conv = bg_ref[...].astype(F32) * (wc[0:1] * zm1 + wc[1:2] * z + wc[2:3] * zp1)
    y = _bdot(attn_ref[...], wo_ref[0:ATTN_W, :]) + _bdot(conv.astype(BF16), wo_ref[ATTN_W:, :])
    mod = mod_ref[...]
    o_ref[...] = x_ref[...] + mod[2:3] * _rms(y, gpost_ref[...])


def _outproj(x, mods, attn, bg, z, w_conv, w_out, g_post, tm):
    b, s, d = x.shape
    cw = z.shape[2]
    halo = 16
    r = tm // halo
    last = s // halo - 1
    tok = lambda bi, i: (bi, i, 0)
    return pl.pallas_call(
        _outproj_kernel,
        grid=(b, s // tm),
        in_specs=[pl.BlockSpec((None, tm, d), tok),
                  pl.BlockSpec((None, 6, d), lambda bi, i: (bi, 0, 0)),
                  pl.BlockSpec((None, tm, ATTN_W), tok),
                  pl.BlockSpec((None, tm, cw), tok),
                  pl.BlockSpec((None, tm, cw), tok),
                  pl.BlockSpec((None, halo, cw), lambda bi, i: (bi, jnp.maximum(i * r - 1, 0), 0)),
                  pl.BlockSpec((None, halo, cw), lambda bi, i: (bi, jnp.minimum((i + 1) * r, last), 0)),
                  _const_spec(w_conv.shape),
                  _const_spec(w_out.shape),
                  _const_spec((1, d))],
        out_specs=pl.BlockSpec((None, tm, d), tok),
        out_shape=jax.ShapeDtypeStruct((b, s, d), F32),
        compiler_params=_cparams(("parallel", "parallel")),
        name="outproj",
    )(x, mods, attn, bg, z, z, z, w_conv, w_out, g_post)


def _swiglu_kernel(x_ref, mod_ref, gpre_ref, wg_ref, wu_ref, wd_ref, gpost_ref, o_ref, *, chunk):
    mod = mod_ref[...]
    x = x_ref[...]
    h = _pre(x, gpre_ref[...], mod[3:4], mod[4:5]).astype(BF16)
    acc = None
    for c in range(wg_ref.shape[1] // chunk):
        sl = slice(c * chunk, (c + 1) * chunk)
        a = (jax.nn.silu(_bdot(h, wg_ref[:, sl])) * _bdot(h, wu_ref[:, sl])).astype(BF16)
        part = _bdot(a, wd_ref[sl, :])
        acc = part if acc is None else acc + part
    o_ref[...] = x + mod[5:6] * _rms(acc, gpost_ref[...])


def _swiglu(x, mods, g_pre, w_gate, w_up, w_down, g_post, tm):
    b, s, d = x.shape
    tok = lambda bi, i: (bi, i, 0)
    return pl.pallas_call(
        functools.partial(_swiglu_kernel, chunk=FF_CHUNK_DENSE),
        grid=(b, s // tm),
        in_specs=[pl.BlockSpec((None, tm, d), tok),
                  pl.BlockSpec((None, 6, d), lambda bi, i: (bi, 0, 0)),
                  _const_spec((1, d)),
                  _const_spec(w_gate.shape),
                  _const_spec(w_up.shape),
                  _const_spec(w_down.shape),
                  _const_spec((1, d))],
        out_specs=pl.BlockSpec((None, tm, d), tok),
        out_shape=jax.ShapeDtypeStruct((b, s, d), F32),
        compiler_params=_cparams(("parallel", "parallel")),
        name="swiglu",
    )(x, mods, g_pre, w_gate, w_up, w_down, g_post)


def _gmlp_kernel(x_ref, mod_ref, gpre_ref, win_ref, gv_ref, bv_ref, ws_ref, bs_ref, wout_ref, gpost_ref,
                 gffn_ref, wr_ref, br_ref, tri_ref,
                 x3_ref, h3_ref, route_ref, cnt_ref, run_ref):
    first = jnp.logical_and(pl.program_id(0) == 0, pl.program_id(1) == 0)

    @pl.when(first)
    def _():
        run_ref[...] = jnp.zeros_like(run_ref)

    mod = mod_ref[...]
    x = x_ref[...]
    tm, d = x.shape
    h = _pre(x, gpre_ref[...], mod[0:1], mod[1:2]).astype(BF16)
    u = jax.nn.gelu(_bdot(h, win_ref[:, 0:d]))
    v = jax.nn.gelu(_bdot(h, win_ref[:, d:2 * d]))
    mu = jnp.mean(v, axis=-1, keepdims=True)
    vc = v - mu
    var = jnp.mean(vc * vc, axis=-1, keepdims=True)
    vn = (vc * lax.rsqrt(var + EPS) * gv_ref[...] + bv_ref[...]).astype(BF16)

    n_chunks = tm // CHUNK
    gc = d // GM_GROUPS
    mixed = []
    for g in range(GM_GROUPS):
        rhs = jnp.concatenate([vn[CHUNK * c:CHUNK * (c + 1), gc * g:gc * (g + 1)] for c in range(n_chunks)],
                              axis=1)
        bias = bs_ref[g]
        mixed.append(_bdot(ws_ref[g], rhs) + jnp.concatenate([bias] * n_chunks, axis=1))
    s = jnp.concatenate(
        [jnp.concatenate([mixed[g][:, gc * c:gc * (c + 1)] for g in range(GM_GROUPS)], axis=1)
         for c in range(n_chunks)], axis=0)
    y = _bdot((u * s).astype(BF16), wout_ref[...])
    x3 = x + mod[2:3] * _rms(y, gpost_ref[...])
    x3_ref[...] = x3

    h3 = _pre(x3, gffn_ref[...], mod[3:4], mod[4:5])
    h3_ref[...] = h3
    h_hi = h3.astype(BF16)
    h_lo = (h3 - h_hi.astype(F32)).astype(BF16)
    logits = (_bdot(h_hi, wr_ref[0]) + _bdot(h_lo, wr_ref[0]) + _bdot(h_hi, wr_ref[1])
              + br_ref[...])
    lane = lax.broadcasted_iota(jnp.int32, logits.shape, 1)
    m1 = jnp.max(logits, axis=1, keepdims=True)
    i1 = jnp.min(jnp.where(logits == m1, lane, LANES), axis=1, keepdims=True)
    rest = jnp.where(lane == i1, 2.0 * NEG_BIG, logits)
    m2 = jnp.max(rest, axis=1, keepdims=True)
    i2 = jnp.min(jnp.where(rest == m2, lane, LANES), axis=1, keepdims=True)
    e21 = jnp.exp(m2 - m1)
    w1 = 1.0 / (1.0 + e21)
    w2 = e21 / (1.0 + e21)
    hot1 = lane == i1
    hot2 = lane == i2
    onehot = jnp.where(jnp.logical_or(hot1, hot2), 1.0, 0.0)
    before = _bdot(tri_ref[...], onehot.astype(BF16)) + run_ref[...]
    r1 = jnp.sum(jnp.where(hot1, before, 0.0), axis=1, keepdims=True)
    r2 = jnp.sum(jnp.where(hot2, before, 0.0), axis=1, keepdims=True)
    run = run_ref[...] + jnp.sum(onehot, axis=0, keepdims=True)
    run_ref[...] = run
    cnt_ref[...] = jnp.broadcast_to(run, cnt_ref.shape)
    fields = (i1.astype(F32), i2.astype(F32), r1, r2, w1, w2)
    route = jnp.zeros(logits.shape, F32)
    for j, f in enumerate(fields):
        route = jnp.where(lane == j, f, route)
    route_ref[...] = route


def _gmlp_router(x, mods, g_pre, w_in, g_v, b_v, w_s, b_s_b, w_out, g_post, g_ffn, w_r, b_r, tm):
    b, s, d = x.shape
    tok = lambda bi, i: (bi, i, 0)
    tri = jnp.tril(jnp.ones((tm, tm), F32), -1).astype(BF16)
    return pl.pallas_call(
        _gmlp_kernel,
        grid=(b, s // tm),
        in_specs=[pl.BlockSpec((None, tm, d), tok),
                  pl.BlockSpec((None, 6, d), lambda bi, i: (bi, 0, 0)),
                  _const_spec((1, d)),
                  _const_spec(w_in.shape),
                  _const_spec((1, d)),
                  _const_spec((1, d)),
                  _const_spec(w_s.shape),
                  _const_spec(b_s_b.shape),
                  _const_spec(w_out.shape),
                  _const_spec((1, d)),
                  _const_spec((1, d)),
                  _const_spec(w_r.shape),
                  _const_spec((1, LANES)),
                  _const_spec((tm, tm))],
        out_specs=[pl.BlockSpec((None, tm, d), tok),
                   pl.BlockSpec((None, tm, d), tok),
                   pl.BlockSpec((None, tm, LANES), tok),
                   pl.BlockSpec((8, LANES), lambda bi, i: (0, 0))],
        out_shape=[jax.ShapeDtypeStruct((b, s, d), F32),
                   jax.ShapeDtypeStruct((b, s, d), F32),
                   jax.ShapeDtypeStruct((b, s, LANES), F32),
                   jax.ShapeDtypeStruct((8, LANES), F32)],
        scratch_shapes=[pltpu.VMEM((1, LANES), F32)],
        compiler_params=_cparams(("arbitrary", "arbitrary")),
        name="gmlp_router",
    )(x, mods, g_pre, w_in, g_v, b_v, w_s, b_s_b, w_out, g_post, g_ffn, w_r, b_r, tri)


def _dispatch_kernel(slot_ref, h_ref, xs_in_ref, xs_ref, sem):
    del xs_in_ref
    tm = h_ref.shape[0]
    for r in range(tm):
        for k in range(2):
            dst = slot_ref[0, 0, 2 * r + k]
            pltpu.make_async_copy(h_ref.at[pl.ds(r, 1), :], xs_ref.at[pl.ds(dst, 1), :], sem).start()
    for _ in range(2):
        pltpu.make_async_copy(h_ref, xs_ref.at[pl.ds(0, tm), :], sem).wait()


def _dispatch(h, slots, n_slots, tm):
    n, d = h.shape
    nt = n // tm
    return pl.pallas_call(
        _dispatch_kernel,
        grid=(nt,),
        in_specs=[pl.BlockSpec((1, 1, 2 * tm), lambda i: (i, 0, 0), memory_space=pltpu.SMEM),
                  pl.BlockSpec((tm, d), lambda i: (i, 0)),
                  pl.BlockSpec(memory_space=pl.ANY)],
        out_specs=pl.BlockSpec(memory_space=pl.ANY),
        out_shape=jax.ShapeDtypeStruct((n_slots, d), F32),
        scratch_shapes=[pltpu.SemaphoreType.DMA(())],
        input_output_aliases={2: 0},
        compiler_params=_cparams(("arbitrary",)),
        name="dispatch",
    )(slots.reshape(nt, 1, 2 * tm), h, jnp.zeros((n_slots, d), F32))


def _expert_kernel(te_ref, na_ref, xs_ref, wgu_ref, wd_ref, ys_ref, xb_ref):
    t = pl.program_id(0)
    c = pl.program_id(1)
    active = t < na_ref[0]

    @pl.when(jnp.logical_and(active, c == 0))
    def _():
        xb_ref[...] = xs_ref[...].astype(BF16)

    @pl.when(active)
    def _():
        xb = xb_ref[...]
        acts = []
        for j in range(wd_ref.shape[0] // FF_SUB):
            gu = _bdot(xb, wgu_ref[:, 2 * FF_SUB * j:2 * FF_SUB * (j + 1)])
            acts.append((jax.nn.silu(gu[:, 0:FF_SUB]) * gu[:, FF_SUB:2 * FF_SUB]).astype(BF16))
        part = _bdot(jnp.concatenate(acts, axis=1), wd_ref[...])

        @pl.when(c == 0)
        def _():
            ys_ref[...] = part

        @pl.when(c > 0)
        def _():
            ys_ref[...] += part

    @pl.when(jnp.logical_and(jnp.logical_not(active), c == 0))
    def _():
        ys_ref[...] = jnp.zeros_like(ys_ref)


def _gate_up_kernel(wg_ref, wu_ref, o_ref):
    for j in range(wg_ref.shape[1] // FF_SUB):
        src = slice(FF_SUB * j, FF_SUB * (j + 1))
        o_ref[:, 2 * FF_SUB * j:2 * FF_SUB * j + FF_SUB] = wg_ref[:, src].astype(BF16)
        o_ref[:, 2 * FF_SUB * j + FF_SUB:2 * FF_SUB * (j + 1)] = wu_ref[:, src].astype(BF16)


def _interleave_gate_up(w_gate, w_up):
    ne, d, ff = w_gate.shape
    cols = 2 * FF_SUB
    spec = pl.BlockSpec((None, d, cols), lambda e, j: (e, 0, j))
    return pl.pallas_call(
        _gate_up_kernel,
        grid=(ne, ff // cols),
        in_specs=[spec, spec],
        out_specs=pl.BlockSpec((None, d, 2 * cols), lambda e, j: (e, 0, j)),
        out_shape=jax.ShapeDtypeStruct((ne, d, 2 * ff), BF16),
        compiler_params=_cparams(("parallel", "parallel")),
        name="gate_up_weights",
    )(w_gate, w_up)


def _experts(tile_expert, n_active, xs, w_gu, w_down, tm, chunk):
    n_slots, d = xs.shape
    ff = w_down.shape[1]
    n_tiles = n_slots // tm
    last = ff // chunk - 1

    def live(t, na):
        return jnp.minimum(t, na[0] - 1)

    def chunk_of(t, c, na):
        return jnp.where(t < na[0], c, last)

    grid_spec = pltpu.PrefetchScalarGridSpec(
        num_scalar_prefetch=2,
        grid=(n_tiles, ff // chunk),
        in_specs=[pl.BlockSpec((tm, d), lambda t, c, te, na: (live(t, na), 0)),
                  pl.BlockSpec((None, d, 2 * chunk), lambda t, c, te, na: (te[live(t, na)], 0, chunk_of(t, c, na))),
                  pl.BlockSpec((None, chunk, d), lambda t, c, te, na: (te[live(t, na)], chunk_of(t, c, na), 0))],
        out_specs=pl.BlockSpec((tm, d), lambda t, c, te, na: (t, 0)),
        scratch_shapes=[pltpu.VMEM((tm, d), BF16)],
    )
    return pl.pallas_call(
        _expert_kernel,
        grid_spec=grid_spec,
        out_shape=jax.ShapeDtypeStruct((n_slots, d), F32),
        compiler_params=_cparams(("arbitrary", "arbitrary")),
        name="experts",
    )(tile_expert, n_active, xs, w_gu, w_down)


def _combine_kernel(slot_ref, x_ref, mod_ref, route_ref, gpost_ref, ys_ref, o_ref, buf_ref, sem):
    i = pl.program_id(0)
    nt = pl.num_programs(0) - 1
    tm = x_ref.shape[0]

    @pl.when(i < nt)
    def _():
        b = i % 2
        for r in range(tm):
            for k in range(2):
                src = slot_ref[0, 0, 2 * r + k]
                pltpu.make_async_copy(ys_ref.at[pl.ds(src, 1), :], buf_ref.at[b, k, pl.ds(r, 1), :],
                                      sem.at[b]).start()

    @pl.when(i > 0)
    def _():
        b = (i - 1) % 2
        for k in range(2):
            pltpu.make_async_copy(ys_ref.at[pl.ds(0, tm), :], buf_ref.at[b, k], sem.at[b]).wait()
        route = route_ref[...]
        f = route[:, 4:5] * buf_ref[b, 0] + route[:, 5:6] * buf_ref[b, 1]
        mod = mod_ref[...]
        o_ref[...] = x_ref[...] + mod[5:6] * _rms(f, gpost_ref[...])


def _combine(x, mods, route, g_post, ys, slots, seq, tm):
    n, d = x.shape
    nt = n // tm
    per_batch = seq // tm
    done = lambda i: jnp.maximum(i - 1, 0)
    return pl.pallas_call(
        _combine_kernel,
        grid=(nt + 1,),
        in_specs=[pl.BlockSpec((1, 1, 2 * tm), lambda i: (jnp.minimum(i, nt - 1), 0, 0), memory_space=pltpu.SMEM),
                  pl.BlockSpec((tm, d), lambda i: (done(i), 0)),
                  pl.BlockSpec((None, 6, d), lambda i: (done(i) // per_batch, 0, 0)),
                  pl.BlockSpec((tm, LANES), lambda i: (done(i), 0)),
                  _const_spec((1, d)),
                  pl.BlockSpec(memory_space=pl.ANY)],
        out_specs=pl.BlockSpec((tm, d), lambda i: (done(i), 0)),
        out_shape=jax.ShapeDtypeStruct((n, d), F32),
        scratch_shapes=[pltpu.VMEM((2, 2, tm, d), F32), pltpu.SemaphoreType.DMA((2,))],
        compiler_params=_cparams(("arbitrary",)),
        name="combine",
    )(slots.reshape(nt, 1, 2 * tm), x, mods, route, g_post, ys)


def _rope_tables(n):
    axis_dim = HEAD_DIM // 2
    pos = jnp.arange(n, dtype=jnp.int32)
    r = (pos // GRID_W).astype(F32)[:, None]
    col = (pos % GRID_W).astype(F32)[:, None]
    inv = 1.0 / (ROPE_THETA ** (jnp.arange(0, axis_dim, 2, dtype=F32) / axis_dim))
    ang = jnp.concatenate([r * inv, col * inv], axis=-1)
    cos, sin = jnp.cos(ang), jnp.sin(ang)
    zero = jnp.zeros_like(sin)
    reps = LANES // HEAD_DIM
    return (jnp.tile(jnp.concatenate([cos, cos], -1), (1, reps)),
            jnp.tile(jnp.concatenate([-sin, zero], -1), (1, reps)),
            jnp.tile(jnp.concatenate([zero, sin], -1), (1, reps)))


def _head_mean_matrix(width):
    idx = jnp.arange(width) // HEAD_DIM
    return jnp.where(idx[:, None] == idx[None, :], 1.0 / HEAD_DIM, 0.0).astype(BF16)


def kernel(x, c, ctx, c_ctx, e_w_mod, e_b_mod, e_g_pre_mix, e_g_post_mix, e_w_in, e_g_q, e_g_k, e_w_conv, e_w_out, e_g_pre_ffn, e_g_post_ffn, e_w_gate, e_w_up, e_w_down, o_w_mod, o_b_mod, o_g_pre_mix, o_g_post_mix, o_w_in, o_g_v, o_b_v, o_w_s, o_b_s, o_w_out, o_g_pre_ffn, o_g_post_ffn, o_w_router, o_b_router, o_w_gate, o_w_up, o_w_down):
    b, s, d = x.shape
    n_ctx = ctx.shape[1]
    n = b * s
    tm = min(TOKEN_TILE, s)
    tq = min(ATTN_Q_TILE, s)
    assert b + 1 <= 8 and s % tm == 0 and s % tq == 0 and tm % CHUNK == 0 and n_ctx % 16 == 0
    assert (s // tm) % ATTN_BLOCKS_PER_ITER == 0 and FF_CHUNK_EXPERT % FF_SUB == 0
    assert e_w_mod.shape[0] == 1 and o_w_mod.shape[0] == 1
    row = lambda g: g.reshape(1, -1)

    cond8 = jnp.zeros((8, d), F32).at[:b].set(c).at[b].set(c_ctx)
    mods_e = _modulation(cond8, e_w_mod[0], e_b_mod[0])
    mods_o = _modulation(cond8, o_w_mod[0], o_b_mod[0])

    w_in = e_w_in[0].astype(BF16)
    gq = jnp.tile(e_g_q[0], N_HEADS).reshape(1, ATTN_W)
    gk = jnp.tile(e_g_k[0], N_KV_HEADS).reshape(1, KV_W)
    eq, ek = _head_mean_matrix(ATTN_W), _head_mean_matrix(KV_W)
    cos, sa, sb = _rope_tables(s)
    qt, k, vt4, bg, z = _inproj(x, mods_e, None, row(e_g_pre_mix[0]), w_in, gq, gk, eq, ek, cos, sa, sb, tm)
    ones = jnp.ones((n_ctx, LANES), F32)
    zeros = jnp.zeros((n_ctx, LANES), F32)
    _, kc, vct4, _, _ = _inproj(ctx, mods_e, b, row(e_g_pre_mix[0]), w_in, gq, gk, eq, ek, ones, zeros, zeros,
                                n_ctx)
    score_bound = (ATTN_BOUND_MARGIN * HEAD_DIM ** 0.5 * LOG2E * jnp.max(jnp.abs(e_g_q[0]))
                   * jnp.max(jnp.abs(e_g_k[0]))).reshape(1).astype(F32)
    attn = _attention(score_bound, qt, k.reshape(b, s // tm, tm, KV_W), vt4, kc.reshape(b, 1, n_ctx, KV_W), vct4, tq)
    x1 = _outproj(x, mods_e, attn, bg, z, e_w_conv[0], e_w_out[0].astype(BF16), row(e_g_post_mix[0]), tm)

    x2 = _swiglu(x1, mods_e, row(e_g_pre_ffn[0]), e_w_gate[0].astype(BF16), e_w_up[0].astype(BF16),
                 e_w_down[0].astype(BF16), row(e_g_post_ffn[0]), tm)

    b_s_b = jnp.broadcast_to(o_b_s[0][:, :, None], (GM_GROUPS, CHUNK, d // GM_GROUPS))
    w_r32 = jnp.zeros((d, LANES), F32).at[:, :N_EXPERTS].set(o_w_router[0])
    w_r_hi = w_r32.astype(BF16)
    w_r = jnp.stack([w_r_hi, (w_r32 - w_r_hi.astype(F32)).astype(BF16)])
    b_r = jnp.full((1, LANES), NEG_BIG, F32).at[0, :N_EXPERTS].set(o_b_router[0])
    x3, h3, route, counts = _gmlp_router(
        x2, mods_o, row(o_g_pre_mix[0]), o_w_in[0].astype(BF16), row(o_g_v[0]), row(o_b_v[0]),
        o_w_s[0].astype(BF16), b_s_b, o_w_out[0].astype(BF16), row(o_g_post_mix[0]), row(o_g_pre_ffn[0]),
        w_r, b_r, tm)

    te_rows = EXPERT_TILE
    n_tiles = -(-(2 * n + N_EXPERTS * (te_rows - 1)) // te_rows)
    n_slots = n_tiles * te_rows
    cnt = counts[0, :N_EXPERTS].astype(jnp.int32)
    tiles_per = (cnt + te_rows - 1) // te_rows
    tile_end = jnp.cumsum(tiles_per)
    base = (tile_end - tiles_per) * te_rows
    route2 = route.reshape(n, LANES)
    e12 = route2[:, 0:2].astype(jnp.int32)
    r12 = route2[:, 2:4].astype(jnp.int32)
    slots = (base[e12] + r12).reshape(-1)
    tile_expert = jnp.minimum(
        jnp.sum(jnp.arange(n_tiles, dtype=jnp.int32)[:, None] >= tile_end[None, :], axis=1),
        N_EXPERTS - 1).astype(jnp.int32)
    n_active = tile_end[-1:].astype(jnp.int32)

    rt = min(ROUTE_TILE, s)
    xs = _dispatch(h3.reshape(n, d), slots, n_slots, rt)
    w_gu = _interleave_gate_up(o_w_gate[0], o_w_up[0])
    ys = _experts(tile_expert, n_active, xs, w_gu, o_w_down[0].astype(BF16), te_rows, FF_CHUNK_EXPERT)
    out = _combine(x3.reshape(n, d), mods_o, route2, row(o_g_post_ffn[0]), ys, slots, s, rt)
    return out.reshape(b, s, d)
```

```python
import functools
import math

import jax
import jax.numpy as jnp
from jax import lax
from jax.experimental import pallas as pl
from jax.experimental.pallas import tpu as pltpu

F32 = jnp.float32
BF16 = jnp.bfloat16

EPS = 1e-6
GRID_W = 64
N_HEADS = 8
N_KV_HEADS = 2
HEAD_DIM = 64
ATTN_W = N_HEADS * HEAD_DIM
KV_W = N_KV_HEADS * HEAD_DIM
ROPE_THETA = 10000.0
CHUNK = 128
GM_GROUPS = 8
N_EXPERTS = 8
LANES = 128
LOG2E = 1.4426950408889634
NEG_BIG = -1e30

VMEM_LIMIT_BYTES = 56 * 1024 * 1024

TOKEN_TILE = 512
ATTN_Q_TILE = 512
ATTN_BLOCKS_PER_ITER = 2
ATTN_LOOKAHEAD = 2
ATTN_HEADROOM = 40.0
ATTN_BOUND_MARGIN = 1.05
ATTN_MIN_ROW_SUM = 2.0 ** -20
ATTN_MAX_ROW_SUM = 2.0 ** 100
FF_CHUNK_DENSE = 1408
GMLP_ROW_PARTS = 2
FF_CHUNK_EXPERT = 1792
FF_SUB = 256
EXPERT_TILE = 1024
ROUTE_TILE = 256


def _cparams(sem):
    return pltpu.CompilerParams(dimension_semantics=sem, vmem_limit_bytes=VMEM_LIMIT_BYTES)


def _const_spec(shape):
    n = len(shape)
    return pl.BlockSpec(shape, lambda *_: (0,) * n, pipeline_mode=pl.Buffered(1))


def _rms(x, g):
    return x * lax.rsqrt(jnp.mean(x * x, axis=-1, keepdims=True) + EPS) * g


def _pre(x, g, shift, scale):
    return _rms(x, g) * (1.0 + scale) + shift


def _bdot(a, b):
    return jnp.dot(a, b, preferred_element_type=F32)


def _mod_kernel(c_ref, w_ref, b_ref, o_ref):
    a = jax.nn.silu(c_ref[...])
    o_ref[...] = jnp.dot(a, w_ref[...], precision=lax.Precision.HIGHEST,
                         preferred_element_type=F32) + b_ref[...]


def _modulation(cond8, w_mod, b_mod):
    d = cond8.shape[1]
    out = pl.pallas_call(
        _mod_kernel,
        grid=(6,),
        in_specs=[pl.BlockSpec((8, d), lambda j: (0, 0)),
                  pl.BlockSpec((d, d), lambda j: (0, j)),
                  pl.BlockSpec((1, d), lambda j: (0, j))],
        out_specs=pl.BlockSpec((8, d), lambda j: (0, j)),
        out_shape=jax.ShapeDtypeStruct((8, 6 * d), F32),
        compiler_params=_cparams(("arbitrary",)),
        name="modulation",
    )(cond8, w_mod, b_mod.reshape(1, 6 * d))
    return out.reshape(8, 6, d)


def _rope128(t, cos, sa, sb):
    return t * cos + pltpu.roll(t, 96, 1) * sa + pltpu.roll(t, 32, 1) * sb


def _inproj_kernel(x_ref, mod_ref, gpre_ref, w_ref, gq_ref, gk_ref, eq_ref, ek_ref,
                   cos_ref, sa_ref, sb_ref, qt_ref, k_ref, vt_ref, bg_ref, z_ref, *, q_scale):
    mod = mod_ref[...]
    h = _pre(x_ref[...], gpre_ref[...], mod[0:1], mod[1:2]).astype(BF16)
    cos, sa, sb = cos_ref[...], sa_ref[...], sb_ref[...]

    q = _bdot(h, w_ref[:, 0:ATTN_W])
    ms = _bdot((q * q).astype(BF16), eq_ref[...])
    qn = q * lax.rsqrt(ms + EPS) * gq_ref[...]
    qr = jnp.concatenate(
        [_rope128(qn[:, LANES * j:LANES * (j + 1)], cos, sa, sb) for j in range(ATTN_W // LANES)], axis=1)
    qt_ref[...] = (qr * q_scale).T.astype(BF16)

    k = _bdot(h, w_ref[:, ATTN_W:ATTN_W + KV_W])
    msk = _bdot((k * k).astype(BF16), ek_ref[...])
    kn = k * lax.rsqrt(msk + EPS) * gk_ref[...]
    k_ref[...] = _rope128(kn, cos, sa, sb).astype(BF16)

    v = _bdot(h, w_ref[:, ATTN_W + KV_W:ATTN_W + 2 * KV_W])
    vt_ref[...] = v.T.astype(BF16)

    o = ATTN_W + 2 * KV_W
    cw = (w_ref.shape[1] - o) // 3
    bg_ref[...] = _bdot(h, w_ref[:, o:o + cw]).astype(BF16)
    cg = _bdot(h, w_ref[:, o + cw:o + 2 * cw])
    hv = _bdot(h, w_ref[:, o + 2 * cw:o + 3 * cw])
    z_ref[...] = (cg * hv).astype(BF16)


def _inproj(x, mods, mod_row, g_pre, w_in, gq, gk, eq, ek, cos, sa, sb, tm):
    b, s, d = x.shape
    nt = s // tm
    cw = (w_in.shape[1] - ATTN_W - 2 * KV_W) // 3
    row = (lambda bi: bi) if mod_row is None else (lambda bi: mod_row)
    tok = lambda bi, i: (bi, i, 0)
    return pl.pallas_call(
        functools.partial(_inproj_kernel, q_scale=HEAD_DIM ** -0.5 * LOG2E),
        grid=(b, nt),
        in_specs=[pl.BlockSpec((None, tm, d), tok),
                  pl.BlockSpec((None, 6, d), lambda bi, i: (row(bi), 0, 0)),
                  _const_spec((1, d)),
                  _const_spec(w_in.shape),
                  _const_spec((1, ATTN_W)),
                  _const_spec((1, KV_W)),
                  _const_spec((ATTN_W, ATTN_W)),
                  _const_spec((KV_W, KV_W)),
                  pl.BlockSpec((tm, LANES), lambda bi, i: (i, 0)),
                  pl.BlockSpec((tm, LANES), lambda bi, i: (i, 0)),
                  pl.BlockSpec((tm, LANES), lambda bi, i: (i, 0))],
        out_specs=[pl.BlockSpec((None, ATTN_W, tm), lambda bi, i: (bi, 0, i)),
                   pl.BlockSpec((None, tm, KV_W), tok),
                   pl.BlockSpec((None, None, KV_W, tm), lambda bi, i: (bi, i, 0, 0)),
                   pl.BlockSpec((None, tm, cw), tok),
                   pl.BlockSpec((None, tm, cw), tok)],
        out_shape=[jax.ShapeDtypeStruct((b, ATTN_W, s), BF16),
                   jax.ShapeDtypeStruct((b, s, KV_W), BF16),
                   jax.ShapeDtypeStruct((b, nt, KV_W, tm), BF16),
                   jax.ShapeDtypeStruct((b, s, cw), BF16),
                   jax.ShapeDtypeStruct((b, s, cw), BF16)],
        compiler_params=_cparams(("parallel", "parallel")),
        name="inproj",
    )(x, mods, g_pre, w_in, gq, gk, eq, ek, cos, sa, sb)


def _attn_kernel(bound_ref, qt_ref, k_ref, vt_ref, kc_ref, vct_ref, o_ref, qp_ref, m_ref, acc_ref, ot_ref,
                 s_ref):
    n_kb = k_ref.shape[0]
    group = N_HEADS // N_KV_HEADS

    for h in range(N_HEADS):
        qh = qt_ref[HEAD_DIM * h:HEAD_DIM * (h + 1), :]
        zq = jnp.zeros_like(qh)
        qp_ref[h] = jnp.concatenate([qh, zq] if h // group == 0 else [zq, qh], axis=0)

    look = ATTN_LOOKAHEAD

    def values_aug(vtb, h):
        g = h // group
        return jnp.concatenate([vtb[HEAD_DIM * g:HEAD_DIM * (g + 1), :], jnp.ones((16, vtb.shape[1]), BF16)],
                               axis=0)

    def attend_exact(s, vtb, h, first):
        mb = jnp.max(s, axis=0, keepdims=True)
        m_old = None if first else m_ref[h:h + 1, :]
        m_new = mb if first else jnp.maximum(m_old, mb)
        p = jnp.exp2(s - m_new).astype(BF16)
        pv = _bdot(values_aug(vtb, h), p)
        acc_ref[h] = pv if first else acc_ref[h] * jnp.exp2(m_old - m_new) + pv
        m_ref[h:h + 1, :] = m_new

    def attend_stream(s, vtb, h, first):
        del first
        m_cur = m_ref[h:h + 1, :]
        p = jnp.exp2(s - m_cur).astype(BF16)
        pv = _bdot(values_aug(vtb, h), p)
        m_new = jnp.maximum(m_cur, jnp.max(s, axis=0, keepdims=True))
        acc_ref[h] = (acc_ref[h] + pv) * jnp.exp2(m_cur - m_new)
        m_ref[h:h + 1, :] = m_new

    def run(pending, blocks, kb_after, attend, first):
        n_steps = len(blocks) * N_HEADS
        for n in range(n_steps):
            s = pending.pop(0)
            ahead = n + look
            kb = blocks[ahead // N_HEADS][0] if ahead < n_steps else kb_after
            pending.append(_bdot(kb, qp_ref[ahead % N_HEADS]))
            attend(s, blocks[n // N_HEADS][1], n % N_HEADS, first)
        return pending

    def all_keys(attend, floor):
        kc = kc_ref[0]
        pending = run([_bdot(kc, qp_ref[h]) for h in range(look)], [(kc, vct_ref[0])], k_ref[0], attend_exact,
                      True)
        for l in range(look):
            s_ref[l] = pending[l]
        if floor is not None:
            for h in range(N_HEADS):
                m_cur = m_ref[h:h + 1, :]
                m_new = jnp.maximum(m_cur, floor)
                acc_ref[h] = acc_ref[h] * jnp.exp2(m_cur - m_new)
                m_ref[h:h + 1, :] = m_new

        def body(i, carry):
            first_blk = i * ATTN_BLOCKS_PER_ITER
            blocks = [(k_ref[first_blk + j], vt_ref[first_blk + j]) for j in range(ATTN_BLOCKS_PER_ITER)]
            kb_after = k_ref[jnp.minimum(first_blk + ATTN_BLOCKS_PER_ITER, n_kb - 1)]
            pending = run([s_ref[l] for l in range(look)], blocks, kb_after, attend, False)
            for l in range(look):
                s_ref[l] = pending[l]
            return carry

        lax.fori_loop(0, n_kb // ATTN_BLOCKS_PER_ITER, body, 0)

    all_keys(attend_stream, bound_ref[0] - ATTN_HEADROOM)
    sums = jnp.concatenate([acc_ref[h, HEAD_DIM:HEAD_DIM + 1, :] for h in range(N_HEADS)], axis=0)
    sound = jnp.logical_and(sums >= ATTN_MIN_ROW_SUM, sums <= ATTN_MAX_ROW_SUM)
    n_unsound = jnp.sum(jnp.where(sound, 0.0, 1.0))

    @pl.when(n_unsound > 0.0)
    def _():
        all_keys(attend_exact, None)

    for h in range(N_HEADS):
        acc = acc_ref[h]
        ot_ref[HEAD_DIM * h:HEAD_DIM * (h + 1), :] = acc[0:HEAD_DIM] / acc[HEAD_DIM:HEAD_DIM + 1]
    o_ref[...] = ot_ref[...].T.astype(BF16)


def _attention(score_bound, qt, k4, vt4, kc4, vct4, tq):
    b, _, s = qt.shape
    _, n_kb, tk, _ = k4.shape
    ctx = kc4.shape[2]
    return pl.pallas_call(
        _attn_kernel,
        grid=(b, s // tq),
        in_specs=[pl.BlockSpec(memory_space=pltpu.SMEM),
                  pl.BlockSpec((None, ATTN_W, tq), lambda bi, i: (bi, 0, i)),
                  pl.BlockSpec((None, n_kb, tk, KV_W), lambda bi, i: (bi, 0, 0, 0)),
                  pl.BlockSpec((None, n_kb, KV_W, tk), lambda bi, i: (bi, 0, 0, 0)),
                  pl.BlockSpec((None, 1, ctx, KV_W), lambda bi, i: (bi, 0, 0, 0)),
                  pl.BlockSpec((None, 1, KV_W, ctx), lambda bi, i: (bi, 0, 0, 0))],
        out_specs=pl.BlockSpec((None, tq, ATTN_W), lambda bi, i: (bi, i, 0)),
        out_shape=jax.ShapeDtypeStruct((b, s, ATTN_W), BF16),
        scratch_shapes=[pltpu.VMEM((N_HEADS, KV_W, tq), BF16),
                        pltpu.VMEM((N_HEADS, tq), F32),
                        pltpu.VMEM((N_HEADS, HEAD_DIM + 16, tq), F32),
                        pltpu.VMEM((ATTN_W, tq), F32),
                        pltpu.VMEM((ATTN_LOOKAHEAD, tk, tq), F32)],
        compiler_params=_cparams(("parallel", "parallel")),
        name="attention",
    )(score_bound, qt, k4, vt4, kc4, vct4)


def _outproj_kernel(x_ref, mod_ref, attn_ref, bg_ref, z_ref, zp_ref, zn_ref, wc_ref, wo_ref, gpost_ref,
                    o_ref):
    i = pl.program_id(1)
    nt = pl.num_programs(1)
    tm = z_ref.shape[0]
    z = z_ref[...].astype(F32)
    halo = zp_ref.shape[0]
    zprev = zp_ref[...].astype(F32)[halo - 1:halo, :] * (i > 0).astype(F32)
    znext = zn_ref[...].astype(F32)[0:1, :] * (i < nt - 1).astype(F32)
    row = lax.broadcasted_iota(jnp.int32, z.shape, 0)
    zm1 = jnp.where(row == 0, zprev, pltpu.roll(z, 1, 0))
    zp1 = jnp.where(row == tm - 1, znext, pltpu.roll(z, tm - 1, 0))
    wc = wc_ref[...]
    conv = bg_ref[...].astype(F32) * (wc[0:1] * zm1 + wc[1:2] * z + wc[2:3] * zp1)
    y = _bdot(attn_ref[...], wo_ref[0:ATTN_W, :]) + _bdot(conv.astype(BF16), wo_ref[ATTN_W:, :])
    mod = mod_ref[...]
    o_ref[...] = x_ref[...] + mod[2:3] * _rms(y, gpost_ref[...])


def _outproj(x, mods, attn, bg, z, w_conv, w_out, g_post, tm):
    b, s, d = x.shape
    cw = z.shape[2]
    halo = 16
    r = tm // halo
    last = s // halo - 1
    tok = lambda bi, i: (bi, i, 0)
    return pl.pallas_call(
        _outproj_kernel,
        grid=(b, s // tm),
        in_specs=[pl.BlockSpec((None, tm, d), tok),
                  pl.BlockSpec((None, 6, d), lambda bi, i: (bi, 0, 0)),
                  pl.BlockSpec((None, tm, ATTN_W), tok),
                  pl.BlockSpec((None, tm, cw), tok),
                  pl.BlockSpec((None, tm, cw), tok),
                  pl.BlockSpec((None, halo, cw), lambda bi, i: (bi, jnp.maximum(i * r - 1, 0), 0)),
                  pl.BlockSpec((None, halo, cw), lambda bi, i: (bi, jnp.minimum((i + 1) * r, last), 0)),
                  _const_spec(w_conv.shape),
                  _const_spec(w_out.shape),
                  _const_spec((1, d))],
        out_specs=pl.BlockSpec((None, tm, d), tok),
        out_shape=jax.ShapeDtypeStruct((b, s, d), F32),
        compiler_params=_cparams(("parallel", "parallel")),
        name="outproj",
    )(x, mods, attn, bg, z, z, z, w_conv, w_out, g_post)


def _swiglu_kernel(x_ref, mod_ref, gpre_ref, wg_ref, wu_ref, wd_ref, gpost_ref, o_ref, *, chunk):
    mod = mod_ref[...]
    x = x_ref[...]
    h = _pre(x, gpre_ref[...], mod[3:4], mod[4:5]).astype(BF16)
    acc = None
    for c in range(wg_ref.shape[1] // chunk):
        sl = slice(c * chunk, (c + 1) * chunk)
        a = (jax.nn.silu(_bdot(h, wg_ref[:, sl])) * _bdot(h, wu_ref[:, sl])).astype(BF16)
        part = _bdot(a, wd_ref[sl, :])
        acc = part if acc is None else acc + part
    o_ref[...] = x + mod[5:6] * _rms(acc, gpost_ref[...])


def _swiglu(x, mods, g_pre, w_gate, w_up, w_down, g_post, tm):
    b, s, d = x.shape
    tok = lambda bi, i: (bi, i, 0)
    return pl.pallas_call(
        functools.partial(_swiglu_kernel, chunk=FF_CHUNK_DENSE),
        grid=(b, s // tm),
        in_specs=[pl.BlockSpec((None, tm, d), tok),
                  pl.BlockSpec((None, 6, d), lambda bi, i: (bi, 0, 0)),
                  _const_spec((1, d)),
                  _const_spec(w_gate.shape),
                  _const_spec(w_up.shape),
                  _const_spec(w_down.shape),
                  _const_spec((1, d))],
        out_specs=pl.BlockSpec((None, tm, d), tok),
        out_shape=jax.ShapeDtypeStruct((b, s, d), F32),
        compiler_params=_cparams(("parallel", "parallel")),
        name="swiglu",
    )(x, mods, g_pre, w_gate, w_up, w_down, g_post)


def _gmlp_kernel(x_ref, mod_ref, gpre_ref, win_ref, gv_ref, bv_ref, ws_ref, bs_ref, wout_ref, gpost_ref,
                 gffn_ref, wr_ref, br_ref, tri_ref,
                 x3_ref, h3_ref, route_ref, cnt_ref, run_ref):
    first = jnp.logical_and(pl.program_id(0) == 0, pl.program_id(1) == 0)

    @pl.when(first)
    def _():
        run_ref[...] = jnp.zeros_like(run_ref)

    mod = mod_ref[...]
    tm, d = x_ref.shape
    gc = d // GM_GROUPS
    n_parts = GMLP_ROW_PARTS if tm % (GMLP_ROW_PARTS * CHUNK) == 0 else 1
    rows = tm // n_parts
    n_chunks = rows // CHUNK
    parts = range(n_parts)
    xs = [x_ref[rows * i:rows * (i + 1), :] for i in parts]
    hs = [_pre(xs[i], gpre_ref[...], mod[0:1], mod[1:2]).astype(BF16) for i in parts]
    us, vns = [], []
    for i in parts:
        us.append(jax.nn.gelu(_bdot(hs[i], win_ref[:, 0:d])))
        v = jax.nn.gelu(_bdot(hs[i], win_ref[:, d:2 * d]))
        vc = v - jnp.mean(v, axis=-1, keepdims=True)
        var = jnp.mean(vc * vc, axis=-1, keepdims=True)
        vns.append((vc * lax.rsqrt(var + EPS) * gv_ref[...] + bv_ref[...]).astype(BF16))
    gated = []
    for i in parts:
        mixed = []
        for g in range(GM_GROUPS):
            rhs = jnp.concatenate(
                [vns[i][CHUNK * c:CHUNK * (c + 1), gc * g:gc * (g + 1)] for c in range(n_chunks)],
                axis=1)
            mixed.append(_bdot(ws_ref[g], rhs) + jnp.concatenate([bs_ref[g]] * n_chunks, axis=1))
        s = jnp.concatenate(
            [jnp.concatenate([mixed[g][:, gc * c:gc * (c + 1)] for g in range(GM_GROUPS)], axis=1)
             for c in range(n_chunks)], axis=0)
        gated.append((us[i] * s).astype(BF16))
    h3s = []
    for i in parts:
        y = _bdot(gated[i], wout_ref[...])
        x3 = xs[i] + mod[2:3] * _rms(y, gpost_ref[...])
        x3_ref[rows * i:rows * (i + 1), :] = x3
        h3 = _pre(x3, gffn_ref[...], mod[3:4], mod[4:5])
        h3_ref[rows * i:rows * (i + 1), :] = h3
        h3s.append(h3)
    logit_parts = []
    for i in parts:
        h_hi = h3s[i].astype(BF16)
        h_lo = (h3s[i] - h_hi.astype(F32)).astype(BF16)
        logit_parts.append(_bdot(h_hi, wr_ref[0]) + _bdot(h_lo, wr_ref[0]) + _bdot(h_hi, wr_ref[1]))
    logits = jnp.concatenate(logit_parts, axis=0) + br_ref[...]
    lane = lax.broadcasted_iota(jnp.int32, logits.shape, 1)
    m1 = jnp.max(logits, axis=1, keepdims=True)
    i1 = jnp.min(jnp.where(logits == m1, lane, LANES), axis=1, keepdims=True)
    rest = jnp.where(lane == i1, 2.0 * NEG_BIG, logits)
    m2 = jnp.max(rest, axis=1, keepdims=True)
    i2 = jnp.min(jnp.where(rest == m2, lane, LANES), axis=1, keepdims=True)
    e21 = jnp.exp(m2 - m1)
    w1 = 1.0 / (1.0 + e21)
    w2 = e21 / (1.0 + e21)
    hot1 = lane == i1
    hot2 = lane == i2
    onehot = jnp.where(jnp.logical_or(hot1, hot2), 1.0, 0.0)
    before = _bdot(tri_ref[...], onehot.astype(BF16)) + run_ref[...]
    r1 = jnp.sum(jnp.where(hot1, before, 0.0), axis=1, keepdims=True)
    r2 = jnp.sum(jnp.where(hot2, before, 0.0), axis=1, keepdims=True)
    run = run_ref[...] + jnp.sum(onehot, axis=0, keepdims=True)
    run_ref[...] = run
    cnt_ref[...] = jnp.broadcast_to(run, cnt_ref.shape)
    fields = (i1.astype(F32), i2.astype(F32), r1, r2, w1, w2)
    route = jnp.zeros(logits.shape, F32)
    for j, f in enumerate(fields):
        route = jnp.where(lane == j, f, route)
    route_ref[...] = route


def _gmlp_router(x, mods, g_pre, w_in, g_v, b_v, w_s, b_s_b, w_out, g_post, g_ffn, w_r, b_r, tm):
    b, s, d = x.shape
    tok = lambda bi, i: (bi, i, 0)
    tri = jnp.tril(jnp.ones((tm, tm), F32), -1).astype(BF16)
    return pl.pallas_call(
        _gmlp_kernel,
        grid=(b, s // tm),
        in_specs=[pl.BlockSpec((None, tm, d), tok),
                  pl.BlockSpec((None, 6, d), lambda bi, i: (bi, 0, 0)),
                  _const_spec((1, d)),
                  _const_spec(w_in.shape),
                  _const_spec((1, d)),
                  _const_spec((1, d)),
                  _const_spec(w_s.shape),
                  _const_spec(b_s_b.shape),
                  _const_spec(w_out.shape),
                  _const_spec((1, d)),
                  _const_spec((1, d)),
                  _const_spec(w_r.shape),
                  _const_spec((1, LANES)),
                  _const_spec((tm, tm))],
        out_specs=[pl.BlockSpec((None, tm, d), tok),
                   pl.BlockSpec((None, tm, d), tok),
                   pl.BlockSpec((None, tm, LANES), tok),
                   pl.BlockSpec((8, LANES), lambda bi, i: (0, 0))],
        out_shape=[jax.ShapeDtypeStruct((b, s, d), F32),
                   jax.ShapeDtypeStruct((b, s, d), F32),
                   jax.ShapeDtypeStruct((b, s, LANES), F32),
                   jax.ShapeDtypeStruct((8, LANES), F32)],
        scratch_shapes=[pltpu.VMEM((1, LANES), F32)],
        compiler_params=_cparams(("arbitrary", "arbitrary")),
        name="gmlp_router",
    )(x, mods, g_pre, w_in, g_v, b_v, w_s, b_s_b, w_out, g_post, g_ffn, w_r, b_r, tri)


def _dispatch_kernel(slot_ref, h_ref, xs_in_ref, xs_ref, sem):
    del xs_in_ref
    tm = h_ref.shape[0]
    for r in range(tm):
        for k in range(2):
            dst = slot_ref[0, 0, 2 * r + k]
            pltpu.make_async_copy(h_ref.at[pl.ds(r, 1), :], xs_ref.at[pl.ds(dst, 1), :], sem).start(priority=k)
    for _ in range(2):
        pltpu.make_async_copy(h_ref, xs_ref.at[pl.ds(0, tm), :], sem).wait()


def _dispatch(h, slots, n_slots, tm):
    n, d = h.shape
    nt = n // tm
    return pl.pallas_call(
        _dispatch_kernel,
        grid=(nt,),
        in_specs=[pl.BlockSpec((1, 1, 2 * tm), lambda i: (i, 0, 0), memory_space=pltpu.SMEM),
                  pl.BlockSpec((tm, d), lambda i: (i, 0)),
                  pl.BlockSpec(memory_space=pl.ANY)],
        out_specs=pl.BlockSpec(memory_space=pl.ANY),
        out_shape=jax.ShapeDtypeStruct((n_slots, d), F32),
        scratch_shapes=[pltpu.SemaphoreType.DMA(())],
        input_output_aliases={2: 0},
        compiler_params=_cparams(("arbitrary",)),
        name="dispatch",
    )(slots.reshape(nt, 1, 2 * tm), h, jnp.zeros((n_slots, d), F32))


def _expert_kernel(te_ref, na_ref, xs_ref, wgu_ref, wd_ref, ys_ref, xb_ref):
    t = pl.program_id(0)
    c = pl.program_id(1)
    active = t < na_ref[0]

    @pl.when(jnp.logical_and(active, c == 0))
    def _():
        xb_ref[...] = xs_ref[...].astype(BF16)

    @pl.when(active)
    def _():
        xb = xb_ref[...]
        acts = []
        for j in range(wd_ref.shape[0] // FF_SUB):
            gu = _bdot(xb, wgu_ref[:, 2 * FF_SUB * j:2 * FF_SUB * (j + 1)])
            acts.append((jax.nn.silu(gu[:, 0:FF_SUB]) * gu[:, FF_SUB:2 * FF_SUB]).astype(BF16))
        part = _bdot(jnp.concatenate(acts, axis=1), wd_ref[...])

        @pl.when(c == 0)
        def _():
            ys_ref[...] = part

        @pl.when(c > 0)
        def _():
            ys_ref[...] += part

    @pl.when(jnp.logical_and(jnp.logical_not(active), c == 0))
    def _():
        ys_ref[...] = jnp.zeros_like(ys_ref)


def _gate_up_kernel(wg_ref, wu_ref, o_ref):
    for j in range(wg_ref.shape[1] // FF_SUB):
        src = slice(FF_SUB * j, FF_SUB * (j + 1))
        o_ref[:, 2 * FF_SUB * j:2 * FF_SUB * j + FF_SUB] = wg_ref[:, src].astype(BF16)
        o_ref[:, 2 * FF_SUB * j + FF_SUB:2 * FF_SUB * (j + 1)] = wu_ref[:, src].astype(BF16)


def _interleave_gate_up(w_gate, w_up):
    ne, d, ff = w_gate.shape
    cols = 2 * FF_SUB
    spec = pl.BlockSpec((None, d, cols), lambda e, j: (e, 0, j))
    return pl.pallas_call(
        _gate_up_kernel,
        grid=(ne, ff // cols),
        in_specs=[spec, spec],
        out_specs=pl.BlockSpec((None, d, 2 * cols), lambda e, j: (e, 0, j)),
        out_shape=jax.ShapeDtypeStruct((ne, d, 2 * ff), BF16),
        compiler_params=_cparams(("parallel", "parallel")),
        name="gate_up_weights",
    )(w_gate, w_up)


def _experts(tile_expert, n_active, xs, w_gu, w_down, tm, chunk):
    n_slots, d = xs.shape
    ff = w_down.shape[1]
    n_tiles = n_slots // tm
    last = ff // chunk - 1

    def live(t, na):
        return jnp.minimum(t, na[0] - 1)

    def chunk_of(t, c, na):
        return jnp.where(t < na[0], c, last)

    grid_spec = pltpu.PrefetchScalarGridSpec(
        num_scalar_prefetch=2,
        grid=(n_tiles, ff // chunk),
        in_specs=[pl.BlockSpec((tm, d), lambda t, c, te, na: (live(t, na), 0)),
                  pl.BlockSpec((None, d, 2 * chunk), lambda t, c, te, na: (te[live(t, na)], 0, chunk_of(t, c, na))),
                  pl.BlockSpec((None, chunk, d), lambda t, c, te, na: (te[live(t, na)], chunk_of(t, c, na), 0))],
        out_specs=pl.BlockSpec((tm, d), lambda t, c, te, na: (t, 0)),
        scratch_shapes=[pltpu.VMEM((tm, d), BF16)],
    )
    return pl.pallas_call(
        _expert_kernel,
        grid_spec=grid_spec,
        out_shape=jax.ShapeDtypeStruct((n_slots, d), F32),
        compiler_params=_cparams(("arbitrary", "arbitrary")),
        name="experts",
    )(tile_expert, n_active, xs, w_gu, w_down)


def _combine_kernel(slot_ref, x_ref, mod_ref, route_ref, gpost_ref, ys_ref, o_ref, buf_ref, sem):
    i = pl.program_id(0)
    nt = pl.num_programs(0) - 1
    tm = x_ref.shape[0]

    @pl.when(i < nt)
    def _():
        b = i % 2
        for r in range(tm):
            for k in range(2):
                src = slot_ref[0, 0, 2 * r + k]
                pltpu.make_async_copy(ys_ref.at[pl.ds(src, 1), :], buf_ref.at[b, k, pl.ds(r, 1), :],
                                      sem.at[b]).start(priority=k)

    @pl.when(i > 0)
    def _():
        b = (i - 1) % 2
        for k in range(2):
            pltpu.make_async_copy(ys_ref.at[pl.ds(0, tm), :], buf_ref.at[b, k], sem.at[b]).wait()
        route = route_ref[...]
        f = route[:, 4:5] * buf_ref[b, 0] + route[:, 5:6] * buf_ref[b, 1]
        mod = mod_ref[...]
        o_ref[...] = x_ref[...] + mod[5:6] * _rms(f, gpost_ref[...])


def _combine(x, mods, route, g_post, ys, slots, seq, tm):
    n, d = x.shape
    nt = n // tm
    per_batch = seq // tm
    done = lambda i: jnp.maximum(i - 1, 0)
    return pl.pallas_call(
        _combine_kernel,
        grid=(nt + 1,),
        in_specs=[pl.BlockSpec((1, 1, 2 * tm), lambda i: (jnp.minimum(i, nt - 1), 0, 0), memory_space=pltpu.SMEM),
                  pl.BlockSpec((tm, d), lambda i: (done(i), 0)),
                  pl.BlockSpec((None, 6, d), lambda i: (done(i) // per_batch, 0, 0)),
                  pl.BlockSpec((tm, LANES), lambda i: (done(i), 0)),
                  _const_spec((1, d)),
                  pl.BlockSpec(memory_space=pl.ANY)],
        out_specs=pl.BlockSpec((tm, d), lambda i: (done(i), 0)),
        out_shape=jax.ShapeDtypeStruct((n, d), F32),
        scratch_shapes=[pltpu.VMEM((2, 2, tm, d), F32), pltpu.SemaphoreType.DMA((2,))],
        compiler_params=_cparams(("arbitrary",)),
        name="combine",
    )(slots.reshape(nt, 1, 2 * tm), x, mods, route, g_post, ys)


def _rope_tables(n):
    axis_dim = HEAD_DIM // 2
    pos = jnp.arange(n, dtype=jnp.int32)
    r = (pos // GRID_W).astype(F32)[:, None]
    col = (pos % GRID_W).astype(F32)[:, None]
    inv = 1.0 / (ROPE_THETA ** (jnp.arange(0, axis_dim, 2, dtype=F32) / axis_dim))
    ang = jnp.concatenate([r * inv, col * inv], axis=-1)
    cos, sin = jnp.cos(ang), jnp.sin(ang)
    zero = jnp.zeros_like(sin)
    reps = LANES // HEAD_DIM
    return (jnp.tile(jnp.concatenate([cos, cos], -1), (1, reps)),
            jnp.tile(jnp.concatenate([-sin, zero], -1), (1, reps)),
            jnp.tile(jnp.concatenate([zero, sin], -1), (1, reps)))


def _head_mean_matrix(width):
    idx = jnp.arange(width) // HEAD_DIM
    return jnp.where(idx[:, None] == idx[None, :], 1.0 / HEAD_DIM, 0.0).astype(BF16)


def kernel(x, c, ctx, c_ctx, e_w_mod, e_b_mod, e_g_pre_mix, e_g_post_mix, e_w_in, e_g_q, e_g_k, e_w_conv, e_w_out, e_g_pre_ffn, e_g_post_ffn, e_w_gate, e_w_up, e_w_down, o_w_mod, o_b_mod, o_g_pre_mix, o_g_post_mix, o_w_in, o_g_v, o_b_v, o_w_s, o_b_s, o_w_out, o_g_pre_ffn, o_g_post_ffn, o_w_router, o_b_router, o_w_gate, o_w_up, o_w_down):
    b, s, d = x.shape
    n_ctx = ctx.shape[1]
    n = b * s
    tm = min(TOKEN_TILE, s)
    tq = min(ATTN_Q_TILE, s)
    assert b + 1 <= 8 and s % tm == 0 and s % tq == 0 and tm % CHUNK == 0 and n_ctx % 16 == 0
    assert (s // tm) % ATTN_BLOCKS_PER_ITER == 0 and FF_CHUNK_EXPERT % FF_SUB == 0
    assert e_w_mod.shape[0] == 1 and o_w_mod.shape[0] == 1
    row = lambda g: g.reshape(1, -1)

    cond8 = jnp.zeros((8, d), F32).at[:b].set(c).at[b].set(c_ctx)
    mods_e = _modulation(cond8, e_w_mod[0], e_b_mod[0])
    mods_o = _modulation(cond8, o_w_mod[0], o_b_mod[0])

    w_in = e_w_in[0].astype(BF16)
    gq = jnp.tile(e_g_q[0], N_HEADS).reshape(1, ATTN_W)
    gk = jnp.tile(e_g_k[0], N_KV_HEADS).reshape(1, KV_W)
    eq, ek = _head_mean_matrix(ATTN_W), _head_mean_matrix(KV_W)
    cos, sa, sb = _rope_tables(s)
    qt, k, vt4, bg, z = _inproj(x, mods_e, None, row(e_g_pre_mix[0]), w_in, gq, gk, eq, ek, cos, sa, sb, tm)
    ones = jnp.ones((n_ctx, LANES), F32)
    zeros = jnp.zeros((n_ctx, LANES), F32)
    _, kc, vct4, _, _ = _inproj(ctx, mods_e, b, row(e_g_pre_mix[0]), w_in, gq, gk, eq, ek, ones, zeros, zeros,
                                n_ctx)
    score_bound = (ATTN_BOUND_MARGIN * HEAD_DIM ** 0.5 * LOG2E * jnp.max(jnp.abs(e_g_q[0]))
                   * jnp.max(jnp.abs(e_g_k[0]))).reshape(1).astype(F32)
    attn = _attention(score_bound, qt, k.reshape(b, s // tm, tm, KV_W), vt4, kc.reshape(b, 1, n_ctx, KV_W), vct4, tq)
    x1 = _outproj(x, mods_e, attn, bg, z, e_w_conv[0], e_w_out[0].astype(BF16), row(e_g_post_mix[0]), tm)

    x2 = _swiglu(x1, mods_e, row(e_g_pre_ffn[0]), e_w_gate[0].astype(BF16), e_w_up[0].astype(BF16),
                 e_w_down[0].astype(BF16), row(e_g_post_ffn[0]), tm)

    b_s_b = jnp.broadcast_to(o_b_s[0][:, :, None], (GM_GROUPS, CHUNK, d // GM_GROUPS))
    w_r32 = jnp.zeros((d, LANES), F32).at[:, :N_EXPERTS].set(o_w_router[0])
    w_r_hi = w_r32.astype(BF16)
    w_r = jnp.stack([w_r_hi, (w_r32 - w_r_hi.astype(F32)).astype(BF16)])
    b_r = jnp.full((1, LANES), NEG_BIG, F32).at[0, :N_EXPERTS].set(o_b_router[0])
    x3, h3, route, counts = _gmlp_router(
        x2, mods_o, row(o_g_pre_mix[0]), o_w_in[0].astype(BF16), row(o_g_v[0]), row(o_b_v[0]),
        o_w_s[0].astype(BF16), b_s_b, o_w_out[0].astype(BF16), row(o_g_post_mix[0]), row(o_g_pre_ffn[0]),
        w_r, b_r, tm)

    te_rows = EXPERT_TILE
    n_tiles = -(-(2 * n + N_EXPERTS * (te_rows - 1)) // te_rows)
    n_slots = n_tiles * te_rows
    cnt = counts[0, :N_EXPERTS].astype(jnp.int32)
    tiles_per = (cnt + te_rows - 1) // te_rows
    tile_end = jnp.cumsum(tiles_per)
    base = (tile_end - tiles_per) * te_rows
    route2 = route.reshape(n, LANES)
    e12 = route2[:, 0:2].astype(jnp.int32)
    r12 = route2[:, 2:4].astype(jnp.int32)
    slots = (base[e12] + r12).reshape(-1)
    tile_expert = jnp.minimum(
        jnp.sum(jnp.arange(n_tiles, dtype=jnp.int32)[:, None] >= tile_end[None, :], axis=1),
        N_EXPERTS - 1).astype(jnp.int32)
    n_active = tile_end[-1:].astype(jnp.int32)

    rt = min(ROUTE_TILE, s)
    xs = _dispatch(h3.reshape(n, d), slots, n_slots, rt)
    w_gu = _interleave_gate_up(o_w_gate[0], o_w_up[0])
    ys = _experts(tile_expert, n_active, xs, w_gu, o_w_down[0].astype(BF16), te_rows, FF_CHUNK_EXPERT)
    out = _combine(x3.reshape(n, d), mods_o, route2, row(o_g_post_ffn[0]), ys, slots, s, rt)
    return out.reshape(b, s, d)
```

```python
import functools
import math

import jax
import jax.numpy as jnp
from jax import lax
from jax.experimental import pallas as pl
from jax.experimental.pallas import tpu as pltpu

F32 = jnp.float32
BF16 = jnp.bfloat16

EPS = 1e-6
GRID_W = 64
N_HEADS = 8
N_KV_HEADS = 2
HEAD_DIM = 64
ATTN_W = N_HEADS * HEAD_DIM
KV_W = N_KV_HEADS * HEAD_DIM
ROPE_THETA = 10000.0
CHUNK = 128
GM_GROUPS = 8
N_EXPERTS = 8
LANES = 128
LOG2E = 1.4426950408889634
NEG_BIG = -1e30

VMEM_LIMIT_BYTES = 56 * 1024 * 1024

TOKEN_TILE = 512
ATTN_Q_TILE = 512
ATTN_BLOCKS_PER_ITER = 2
ATTN_LOOKAHEAD = 2
ATTN_HEADROOM = 40.0
ATTN_BOUND_MARGIN = 1.05
ATTN_MIN_ROW_SUM = 2.0 ** -20
ATTN_MAX_ROW_SUM = 2.0 ** 100
FF_CHUNK_DENSE = 1408
GMLP_ROW_PARTS = 2
ROW_PARTS = 2
FF_CHUNK_EXPERT = 1792
FF_SUB = 256
EXPERT_TILE = 1024
ROUTE_TILE = 256


def _cparams(sem):
    return pltpu.CompilerParams(dimension_semantics=sem, vmem_limit_bytes=VMEM_LIMIT_BYTES)


def _const_spec(shape):
    n = len(shape)
    return pl.BlockSpec(shape, lambda *_: (0,) * n, pipeline_mode=pl.Buffered(1))


def _rms(x, g):
    return x * lax.rsqrt(jnp.mean(x * x, axis=-1, keepdims=True) + EPS) * g


def _pre(x, g, shift, scale):
    return _rms(x, g) * (1.0 + scale) + shift


def _bdot(a, b):
    return jnp.dot(a, b, preferred_element_type=F32)


def _mod_kernel(c_ref, w_ref, b_ref, o_ref):
    a = jax.nn.silu(c_ref[...])
    o_ref[...] = jnp.dot(a, w_ref[...], precision=lax.Precision.HIGHEST,
                         preferred_element_type=F32) + b_ref[...]


def _modulation(cond8, w_mod, b_mod):
    d = cond8.shape[1]
    out = pl.pallas_call(
        _mod_kernel,
        grid=(6,),
        in_specs=[pl.BlockSpec((8, d), lambda j: (0, 0)),
                  pl.BlockSpec((d, d), lambda j: (0, j)),
                  pl.BlockSpec((1, d), lambda j: (0, j))],
        out_specs=pl.BlockSpec((8, d), lambda j: (0, j)),
        out_shape=jax.ShapeDtypeStruct((8, 6 * d), F32),
        compiler_params=_cparams(("arbitrary",)),
        name="modulation",
    )(cond8, w_mod, b_mod.reshape(1, 6 * d))
    return out.reshape(8, 6, d)


def _rope128(t, cos, sa, sb):
    return t * cos + pltpu.roll(t, 96, 1) * sa + pltpu.roll(t, 32, 1) * sb


def _inproj_kernel(x_ref, mod_ref, gpre_ref, w_ref, gq_ref, gk_ref, eq_ref, ek_ref,
                   cos_ref, sa_ref, sb_ref, qt_ref, k_ref, vt_ref, bg_ref, z_ref, *, q_scale):
    mod = mod_ref[...]
    h = _pre(x_ref[...], gpre_ref[...], mod[0:1], mod[1:2]).astype(BF16)
    cos, sa, sb = cos_ref[...], sa_ref[...], sb_ref[...]

    q = _bdot(h, w_ref[:, 0:ATTN_W])
    ms = _bdot((q * q).astype(BF16), eq_ref[...])
    qn = q * lax.rsqrt(ms + EPS) * gq_ref[...]
    qr = jnp.concatenate(
        [_rope128(qn[:, LANES * j:LANES * (j + 1)], cos, sa, sb) for j in range(ATTN_W // LANES)], axis=1)
    qt_ref[...] = (qr * q_scale).T.astype(BF16)

    k = _bdot(h, w_ref[:, ATTN_W:ATTN_W + KV_W])
    msk = _bdot((k * k).astype(BF16), ek_ref[...])
    kn = k * lax.rsqrt(msk + EPS) * gk_ref[...]
    k_ref[...] = _rope128(kn, cos, sa, sb).astype(BF16)

    v = _bdot(h, w_ref[:, ATTN_W + KV_W:ATTN_W + 2 * KV_W])
    vt_ref[...] = v.T.astype(BF16)

    o = ATTN_W + 2 * KV_W
    cw = (w_ref.shape[1] - o) // 3
    bg_ref[...] = _bdot(h, w_ref[:, o:o + cw]).astype(BF16)
    cg = _bdot(h, w_ref[:, o + cw:o + 2 * cw])
    hv = _bdot(h, w_ref[:, o + 2 * cw:o + 3 * cw])
    z_ref[...] = (cg * hv).astype(BF16)


def _inproj(x, mods, mod_row, g_pre, w_in, gq, gk, eq, ek, cos, sa, sb, tm):
    b, s, d = x.shape
    nt = s // tm
    cw = (w_in.shape[1] - ATTN_W - 2 * KV_W) // 3
    row = (lambda bi: bi) if mod_row is None else (lambda bi: mod_row)
    tok = lambda bi, i: (bi, i, 0)
    return pl.pallas_call(
        functools.partial(_inproj_kernel, q_scale=HEAD_DIM ** -0.5 * LOG2E),
        grid=(b, nt),
        in_specs=[pl.BlockSpec((None, tm, d), tok),
                  pl.BlockSpec((None, 6, d), lambda bi, i: (row(bi), 0, 0)),
                  _const_spec((1, d)),
                  _const_spec(w_in.shape),
                  _const_spec((1, ATTN_W)),
                  _const_spec((1, KV_W)),
                  _const_spec((ATTN_W, ATTN_W)),
                  _const_spec((KV_W, KV_W)),
                  pl.BlockSpec((tm, LANES), lambda bi, i: (i, 0)),
                  pl.BlockSpec((tm, LANES), lambda bi, i: (i, 0)),
                  pl.BlockSpec((tm, LANES), lambda bi, i: (i, 0))],
        out_specs=[pl.BlockSpec((None, ATTN_W, tm), lambda bi, i: (bi, 0, i)),
                   pl.BlockSpec((None, tm, KV_W), tok),
                   pl.BlockSpec((None, None, KV_W, tm), lambda bi, i: (bi, i, 0, 0)),
                   pl.BlockSpec((None, tm, cw), tok),
                   pl.BlockSpec((None, tm, cw), tok)],
        out_shape=[jax.ShapeDtypeStruct((b, ATTN_W, s), BF16),
                   jax.ShapeDtypeStruct((b, s, KV_W), BF16),
                   jax.ShapeDtypeStruct((b, nt, KV_W, tm), BF16),
                   jax.ShapeDtypeStruct((b, s, cw), BF16),
                   jax.ShapeDtypeStruct((b, s, cw), BF16)],
        compiler_params=_cparams(("parallel", "parallel")),
        name="inproj",
    )(x, mods, g_pre, w_in, gq, gk, eq, ek, cos, sa, sb)


def _attn_kernel(bound_ref, qt_ref, k_ref, vt_ref, kc_ref, vct_ref, o_ref, qp_ref, m_ref, acc_ref, ot_ref,
                 s_ref):
    n_kb = k_ref.shape[0]
    group = N_HEADS // N_KV_HEADS

    for h in range(N_HEADS):
        qh = qt_ref[HEAD_DIM * h:HEAD_DIM * (h + 1), :]
        zq = jnp.zeros_like(qh)
        qp_ref[h] = jnp.concatenate([qh, zq] if h // group == 0 else [zq, qh], axis=0)

    look = ATTN_LOOKAHEAD

    def values_aug(vtb, h):
        g = h // group
        return jnp.concatenate([vtb[HEAD_DIM * g:HEAD_DIM * (g + 1), :], jnp.ones((16, vtb.shape[1]), BF16)],
                               axis=0)

    def attend_exact(s, vtb, h, first):
        mb = jnp.max(s, axis=0, keepdims=True)
        m_old = None if first else m_ref[h:h + 1, :]
        m_new = mb if first else jnp.maximum(m_old, mb)
        p = jnp.exp2(s - m_new).astype(BF16)
        pv = _bdot(values_aug(vtb, h), p)
        acc_ref[h] = pv if first else acc_ref[h] * jnp.exp2(m_old - m_new) + pv
        m_ref[h:h + 1, :] = m_new

    def attend_stream(s, vtb, h, first):
        del first
        m_cur = m_ref[h:h + 1, :]
        p = jnp.exp2(s - m_cur).astype(BF16)
        pv = _bdot(values_aug(vtb, h), p)
        m_new = jnp.maximum(m_cur, jnp.max(s, axis=0, keepdims=True))
        acc_ref[h] = (acc_ref[h] + pv) * jnp.exp2(m_cur - m_new)
        m_ref[h:h + 1, :] = m_new

    def run(pending, blocks, kb_after, attend, first):
        n_steps = len(blocks) * N_HEADS
        for n in range(n_steps):
            s = pending.pop(0)
            ahead = n + look
            kb = blocks[ahead // N_HEADS][0] if ahead < n_steps else kb_after
            pending.append(_bdot(kb, qp_ref[ahead % N_HEADS]))
            attend(s, blocks[n // N_HEADS][1], n % N_HEADS, first)
        return pending

    def all_keys(attend, floor):
        kc = kc_ref[0]
        pending = run([_bdot(kc, qp_ref[h]) for h in range(look)], [(kc, vct_ref[0])], k_ref[0], attend_exact,
                      True)
        for l in range(look):
            s_ref[l] = pending[l]
        if floor is not None:
            for h in range(N_HEADS):
                m_cur = m_ref[h:h + 1, :]
                m_new = jnp.maximum(m_cur, floor)
                acc_ref[h] = acc_ref[h] * jnp.exp2(m_cur - m_new)
                m_ref[h:h + 1, :] = m_new

        def body(i, carry):
            first_blk = i * ATTN_BLOCKS_PER_ITER
            blocks = [(k_ref[first_blk + j], vt_ref[first_blk + j]) for j in range(ATTN_BLOCKS_PER_ITER)]
            kb_after = k_ref[jnp.minimum(first_blk + ATTN_BLOCKS_PER_ITER, n_kb - 1)]
            pending = run([s_ref[l] for l in range(look)], blocks, kb_after, attend, False)
            for l in range(look):
                s_ref[l] = pending[l]
            return carry

        lax.fori_loop(0, n_kb // ATTN_BLOCKS_PER_ITER, body, 0)

    all_keys(attend_stream, bound_ref[0] - ATTN_HEADROOM)
    sums = jnp.concatenate([acc_ref[h, HEAD_DIM:HEAD_DIM + 1, :] for h in range(N_HEADS)], axis=0)
    sound = jnp.logical_and(sums >= ATTN_MIN_ROW_SUM, sums <= ATTN_MAX_ROW_SUM)
    n_unsound = jnp.sum(jnp.where(sound, 0.0, 1.0))

    @pl.when(n_unsound > 0.0)
    def _():
        all_keys(attend_exact, None)

    for h in range(N_HEADS):
        acc = acc_ref[h]
        ot_ref[HEAD_DIM * h:HEAD_DIM * (h + 1), :] = acc[0:HEAD_DIM] / acc[HEAD_DIM:HEAD_DIM + 1]
    o_ref[...] = ot_ref[...].T.astype(BF16)


def _attention(score_bound, qt, k4, vt4, kc4, vct4, tq):
    b, _, s = qt.shape
    _, n_kb, tk, _ = k4.shape
    ctx = kc4.shape[2]
    return pl.pallas_call(
        _attn_kernel,
        grid=(b, s // tq),
        in_specs=[pl.BlockSpec(memory_space=pltpu.SMEM),
                  pl.BlockSpec((None, ATTN_W, tq), lambda bi, i: (bi, 0, i)),
                  pl.BlockSpec((None, n_kb, tk, KV_W), lambda bi, i: (bi, 0, 0, 0)),
                  pl.BlockSpec((None, n_kb, KV_W, tk), lambda bi, i: (bi, 0, 0, 0)),
                  pl.BlockSpec((None, 1, ctx, KV_W), lambda bi, i: (bi, 0, 0, 0)),
                  pl.BlockSpec((None, 1, KV_W, ctx), lambda bi, i: (bi, 0, 0, 0))],
        out_specs=pl.BlockSpec((None, tq, ATTN_W), lambda bi, i: (bi, i, 0)),
        out_shape=jax.ShapeDtypeStruct((b, s, ATTN_W), BF16),
        scratch_shapes=[pltpu.VMEM((N_HEADS, KV_W, tq), BF16),
                        pltpu.VMEM((N_HEADS, tq), F32),
                        pltpu.VMEM((N_HEADS, HEAD_DIM + 16, tq), F32),
                        pltpu.VMEM((ATTN_W, tq), F32),
                        pltpu.VMEM((ATTN_LOOKAHEAD, tk, tq), F32)],
        compiler_params=_cparams(("parallel", "parallel")),
        name="attention",
    )(score_bound, qt, k4, vt4, kc4, vct4)


def _outproj_kernel(x_ref, mod_ref, attn_ref, bg_ref, z_ref, zp_ref, zn_ref, wc_ref, wo_ref, gpost_ref,
                    o_ref):
    i = pl.program_id(1)
    nt = pl.num_programs(1)
    tm = z_ref.shape[0]
    z = z_ref[...].astype(F32)
    halo = zp_ref.shape[0]
    zprev = zp_ref[...].astype(F32)[halo - 1:halo, :] * (i > 0).astype(F32)
    znext = zn_ref[...].astype(F32)[0:1, :] * (i < nt - 1).astype(F32)
    row = lax.broadcasted_iota(jnp.int32, z.shape, 0)
    zm1 = jnp.where(row == 0, zprev, pltpu.roll(z, 1, 0))
    zp1 = jnp.where(row == tm - 1, znext, pltpu.roll(z, tm - 1, 0))
    wc = wc_ref[...]
    taps = wc[0:1] * zm1 + wc[1:2] * z + wc[2:3] * zp1
    mod = mod_ref[...]
    n_parts = ROW_PARTS if tm % (ROW_PARTS * 16) == 0 else 1
    rows = tm // n_parts
    ys = []
    for p in range(n_parts):
        sl = slice(rows * p, rows * (p + 1))
        conv = (bg_ref[sl, :].astype(F32) * taps[sl, :]).astype(BF16)
        ys.append(_bdot(attn_ref[sl, :], wo_ref[0:ATTN_W, :]) + _bdot(conv, wo_ref[ATTN_W:, :]))
    for p in range(n_parts):
        sl = slice(rows * p, rows * (p + 1))
        o_ref[sl, :] = x_ref[sl, :] + mod[2:3] * _rms(ys[p], gpost_ref[...])


def _outproj(x, mods, attn, bg, z, w_conv, w_out, g_post, tm):
    b, s, d = x.shape
    cw = z.shape[2]
    halo = 16
    r = tm // halo
    last = s // halo - 1
    tok = lambda bi, i: (bi, i, 0)
    return pl.pallas_call(
        _outproj_kernel,
        grid=(b, s // tm),
        in_specs=[pl.BlockSpec((None, tm, d), tok),
                  pl.BlockSpec((None, 6, d), lambda bi, i: (bi, 0, 0)),
                  pl.BlockSpec((None, tm, ATTN_W), tok),
                  pl.BlockSpec((None, tm, cw), tok),
                  pl.BlockSpec((None, tm, cw), tok),
                  pl.BlockSpec((None, halo, cw), lambda bi, i: (bi, jnp.maximum(i * r - 1, 0), 0)),
                  pl.BlockSpec((None, halo, cw), lambda bi, i: (bi, jnp.minimum((i + 1) * r, last), 0)),
                  _const_spec(w_conv.shape),
                  _const_spec(w_out.shape),
                  _const_spec((1, d))],
        out_specs=pl.BlockSpec((None, tm, d), tok),
        out_shape=jax.ShapeDtypeStruct((b, s, d), F32),
        compiler_params=_cparams(("parallel", "parallel")),
        name="outproj",
    )(x, mods, attn, bg, z, z, z, w_conv, w_out, g_post)


def _swiglu_kernel(x_ref, mod_ref, gpre_ref, wg_ref, wu_ref, wd_ref, gpost_ref, o_ref, *, chunk):
    mod = mod_ref[...]
    tm = x_ref.shape[0]
    n_parts = ROW_PARTS if tm % (ROW_PARTS * 16) == 0 else 1
    rows = tm // n_parts
    xs = [x_ref[rows * p:rows * (p + 1), :] for p in range(n_parts)]
    hs = [_pre(xs[p], gpre_ref[...], mod[3:4], mod[4:5]).astype(BF16) for p in range(n_parts)]
    accs = [None] * n_parts
    for c in range(wg_ref.shape[1] // chunk):
        sl = slice(c * chunk, (c + 1) * chunk)
        for p in range(n_parts):
            a = (jax.nn.silu(_bdot(hs[p], wg_ref[:, sl])) * _bdot(hs[p], wu_ref[:, sl])).astype(BF16)
            part = _bdot(a, wd_ref[sl, :])
            accs[p] = part if accs[p] is None else accs[p] + part
    for p in range(n_parts):
        o_ref[rows * p:rows * (p + 1), :] = xs[p] + mod[5:6] * _rms(accs[p], gpost_ref[...])


def _swiglu(x, mods, g_pre, w_gate, w_up, w_down, g_post, tm):
    b, s, d = x.shape
    tok = lambda bi, i: (bi, i, 0)
    return pl.pallas_call(
        functools.partial(_swiglu_kernel, chunk=FF_CHUNK_DENSE),
        grid=(b, s // tm),
        in_specs=[pl.BlockSpec((None, tm, d), tok),
                  pl.BlockSpec((None, 6, d), lambda bi, i: (bi, 0, 0)),
                  _const_spec((1, d)),
                  _const_spec(w_gate.shape),
                  _const_spec(w_up.shape),
                  _const_spec(w_down.shape),
                  _const_spec((1, d))],
        out_specs=pl.BlockSpec((None, tm, d), tok),
        out_shape=jax.ShapeDtypeStruct((b, s, d), F32),
        compiler_params=_cparams(("parallel", "parallel")),
        name="swiglu",
    )(x, mods, g_pre, w_gate, w_up, w_down, g_post)


def _gmlp_kernel(x_ref, mod_ref, gpre_ref, win_ref, gv_ref, bv_ref, ws_ref, bs_ref, wout_ref, gpost_ref,
                 gffn_ref, wr_ref, br_ref, tri_ref,
                 x3_ref, h3_ref, route_ref, cnt_ref, run_ref):
    first = jnp.logical_and(pl.program_id(0) == 0, pl.program_id(1) == 0)

    @pl.when(first)
    def _():
        run_ref[...] = jnp.zeros_like(run_ref)

    mod = mod_ref[...]
    tm, d = x_ref.shape
    gc = d // GM_GROUPS
    n_parts = GMLP_ROW_PARTS if tm % (GMLP_ROW_PARTS * CHUNK) == 0 else 1
    rows = tm // n_parts
    n_chunks = rows // CHUNK
    parts = range(n_parts)
    xs = [x_ref[rows * i:rows * (i + 1), :] for i in parts]
    hs = [_pre(xs[i], gpre_ref[...], mod[0:1], mod[1:2]).astype(BF16) for i in parts]
    us, vns = [], []
    for i in parts:
        us.append(jax.nn.gelu(_bdot(hs[i], win_ref[:, 0:d])))
        v = jax.nn.gelu(_bdot(hs[i], win_ref[:, d:2 * d]))
        vc = v - jnp.mean(v, axis=-1, keepdims=True)
        var = jnp.mean(vc * vc, axis=-1, keepdims=True)
        vns.append((vc * lax.rsqrt(var + EPS) * gv_ref[...] + bv_ref[...]).astype(BF16))
    gated = []
    for i in parts:
        mixed = []
        for g in range(GM_GROUPS):
            rhs = jnp.concatenate(
                [vns[i][CHUNK * c:CHUNK * (c + 1), gc * g:gc * (g + 1)] for c in range(n_chunks)],
                axis=1)
            mixed.append(_bdot(ws_ref[g], rhs) + jnp.concatenate([bs_ref[g]] * n_chunks, axis=1))
        s = jnp.concatenate(
            [jnp.concatenate([mixed[g][:, gc * c:gc * (c + 1)] for g in range(GM_GROUPS)], axis=1)
             for c in range(n_chunks)], axis=0)
        gated.append((us[i] * s).astype(BF16))
    h3s = []
    for i in parts:
        y = _bdot(gated[i], wout_ref[...])
        x3 = xs[i] + mod[2:3] * _rms(y, gpost_ref[...])
        x3_ref[rows * i:rows * (i + 1), :] = x3
        h3 = _pre(x3, gffn_ref[...], mod[3:4], mod[4:5])
        h3_ref[rows * i:rows * (i + 1), :] = h3
        h3s.append(h3)
    logit_parts = []
    for i in parts:
        h_hi = h3s[i].astype(BF16)
        h_lo = (h3s[i] - h_hi.astype(F32)).astype(BF16)
        logit_parts.append(_bdot(h_hi, wr_ref[0]) + _bdot(h_lo, wr_ref[0]) + _bdot(h_hi, wr_ref[1]))
    logits = jnp.concatenate(logit_parts, axis=0) + br_ref[...]
    lane = lax.broadcasted_iota(jnp.int32, logits.shape, 1)
    m1 = jnp.max(logits, axis=1, keepdims=True)
    i1 = jnp.min(jnp.where(logits == m1, lane, LANES), axis=1, keepdims=True)
    rest = jnp.where(lane == i1, 2.0 * NEG_BIG, logits)
    m2 = jnp.max(rest, axis=1, keepdims=True)
    i2 = jnp.min(jnp.where(rest == m2, lane, LANES), axis=1, keepdims=True)
    e21 = jnp.exp(m2 - m1)
    w1 = 1.0 / (1.0 + e21)
    w2 = e21 / (1.0 + e21)
    hot1 = lane == i1
    hot2 = lane == i2
    onehot = jnp.where(jnp.logical_or(hot1, hot2), 1.0, 0.0)
    before = _bdot(tri_ref[...], onehot.astype(BF16)) + run_ref[...]
    r1 = jnp.sum(jnp.where(hot1, before, 0.0), axis=1, keepdims=True)
    r2 = jnp.sum(jnp.where(hot2, before, 0.0), axis=1, keepdims=True)
    run = run_ref[...] + jnp.sum(onehot, axis=0, keepdims=True)
    run_ref[...] = run
    cnt_ref[...] = jnp.broadcast_to(run, cnt_ref.shape)
    fields = (i1.astype(F32), i2.astype(F32), r1, r2, w1, w2)
    route = jnp.zeros(logits.shape, F32)
    for j, f in enumerate(fields):
        route = jnp.where(lane == j, f, route)
    route_ref[...] = route


def _gmlp_router(x, mods, g_pre, w_in, g_v, b_v, w_s, b_s_b, w_out, g_post, g_ffn, w_r, b_r, tm):
    b, s, d = x.shape
    tok = lambda bi, i: (bi, i, 0)
    tri = jnp.tril(jnp.ones((tm, tm), F32), -1).astype(BF16)
    return pl.pallas_call(
        _gmlp_kernel,
        grid=(b, s // tm),
        in_specs=[pl.BlockSpec((None, tm, d), tok),
                  pl.BlockSpec((None, 6, d), lambda bi, i: (bi, 0, 0)),
                  _const_spec((1, d)),
                  _const_spec(w_in.shape),
                  _const_spec((1, d)),
                  _const_spec((1, d)),
                  _const_spec(w_s.shape),
                  _const_spec(b_s_b.shape),
                  _const_spec(w_out.shape),
                  _const_spec((1, d)),
                  _const_spec((1, d)),
                  _const_spec(w_r.shape),
                  _const_spec((1, LANES)),
                  _const_spec((tm, tm))],
        out_specs=[pl.BlockSpec((None, tm, d), tok),
                   pl.BlockSpec((None, tm, d), tok),
                   pl.BlockSpec((None, tm, LANES), tok),
                   pl.BlockSpec((8, LANES), lambda bi, i: (0, 0))],
        out_shape=[jax.ShapeDtypeStruct((b, s, d), F32),
                   jax.ShapeDtypeStruct((b, s, d), F32),
                   jax.ShapeDtypeStruct((b, s, LANES), F32),
                   jax.ShapeDtypeStruct((8, LANES), F32)],
        scratch_shapes=[pltpu.VMEM((1, LANES), F32)],
        compiler_params=_cparams(("arbitrary", "arbitrary")),
        name="gmlp_router",
    )(x, mods, g_pre, w_in, g_v, b_v, w_s, b_s_b, w_out, g_post, g_ffn, w_r, b_r, tri)


def _dispatch_kernel(slot_ref, h_ref, xs_in_ref, xs_ref, sem):
    del xs_in_ref
    tm = h_ref.shape[0]
    for r in range(tm):
        for k in range(2):
            dst = slot_ref[0, 0, 2 * r + k]
            pltpu.make_async_copy(h_ref.at[pl.ds(r, 1), :], xs_ref.at[pl.ds(dst, 1), :], sem).start(priority=k)
    for _ in range(2):
        pltpu.make_async_copy(h_ref, xs_ref.at[pl.ds(0, tm), :], sem).wait()


def _dispatch(h, slots, n_slots, tm):
    n, d = h.shape
    nt = n // tm
    return pl.pallas_call(
        _dispatch_kernel,
        grid=(nt,),
        in_specs=[pl.BlockSpec((1, 1, 2 * tm), lambda i: (i, 0, 0), memory_space=pltpu.SMEM),
                  pl.BlockSpec((tm, d), lambda i: (i, 0)),
                  pl.BlockSpec(memory_space=pl.ANY)],
        out_specs=pl.BlockSpec(memory_space=pl.ANY),
        out_shape=jax.ShapeDtypeStruct((n_slots, d), F32),
        scratch_shapes=[pltpu.SemaphoreType.DMA(())],
        input_output_aliases={2: 0},
        compiler_params=_cparams(("arbitrary",)),
        name="dispatch",
    )(slots.reshape(nt, 1, 2 * tm), h, jnp.zeros((n_slots, d), F32))


def _expert_kernel(te_ref, na_ref, xs_ref, wgu_ref, wd_ref, ys_ref, xb_ref):
    t = pl.program_id(0)
    c = pl.program_id(1)
    active = t < na_ref[0]

    @pl.when(jnp.logical_and(active, c == 0))
    def _():
        xb_ref[...] = xs_ref[...].astype(BF16)

    @pl.when(active)
    def _():
        xb = xb_ref[...]
        acts = []
        for j in range(wd_ref.shape[0] // FF_SUB):
            gu = _bdot(xb, wgu_ref[:, 2 * FF_SUB * j:2 * FF_SUB * (j + 1)])
            acts.append((jax.nn.silu(gu[:, 0:FF_SUB]) * gu[:, FF_SUB:2 * FF_SUB]).astype(BF16))
        part = _bdot(jnp.concatenate(acts, axis=1), wd_ref[...])

        @pl.when(c == 0)
        def _():
            ys_ref[...] = part

        @pl.when(c > 0)
        def _():
            ys_ref[...] += part

    @pl.when(jnp.logical_and(jnp.logical_not(active), c == 0))
    def _():
        ys_ref[...] = jnp.zeros_like(ys_ref)


def _gate_up_kernel(wg_ref, wu_ref, o_ref):
    for j in range(wg_ref.shape[1] // FF_SUB):
        src = slice(FF_SUB * j, FF_SUB * (j + 1))
        o_ref[:, 2 * FF_SUB * j:2 * FF_SUB * j + FF_SUB] = wg_ref[:, src].astype(BF16)
        o_ref[:, 2 * FF_SUB * j + FF_SUB:2 * FF_SUB * (j + 1)] = wu_ref[:, src].astype(BF16)


def _interleave_gate_up(w_gate, w_up):
    ne, d, ff = w_gate.shape
    cols = 2 * FF_SUB
    spec = pl.BlockSpec((None, d, cols), lambda e, j: (e, 0, j))
    return pl.pallas_call(
        _gate_up_kernel,
        grid=(ne, ff // cols),
        in_specs=[spec, spec],
        out_specs=pl.BlockSpec((None, d, 2 * cols), lambda e, j: (e, 0, j)),
        out_shape=jax.ShapeDtypeStruct((ne, d, 2 * ff), BF16),
        compiler_params=_cparams(("parallel", "parallel")),
        name="gate_up_weights",
    )(w_gate, w_up)


def _experts(tile_expert, n_active, xs, w_gu, w_down, tm, chunk):
    n_slots, d = xs.shape
    ff = w_down.shape[1]
    n_tiles = n_slots // tm
    last = ff // chunk - 1

    def live(t, na):
        return jnp.minimum(t, na[0] - 1)

    def chunk_of(t, c, na):
        return jnp.where(t < na[0], c, last)

    grid_spec = pltpu.PrefetchScalarGridSpec(
        num_scalar_prefetch=2,
        grid=(n_tiles, ff // chunk),
        in_specs=[pl.BlockSpec((tm, d), lambda t, c, te, na: (live(t, na), 0)),
                  pl.BlockSpec((None, d, 2 * chunk), lambda t, c, te, na: (te[live(t, na)], 0, chunk_of(t, c, na))),
                  pl.BlockSpec((None, chunk, d), lambda t, c, te, na: (te[live(t, na)], chunk_of(t, c, na), 0))],
        out_specs=pl.BlockSpec((tm, d), lambda t, c, te, na: (t, 0)),
        scratch_shapes=[pltpu.VMEM((tm, d), BF16)],
    )
    return pl.pallas_call(
        _expert_kernel,
        grid_spec=grid_spec,
        out_shape=jax.ShapeDtypeStruct((n_slots, d), F32),
        compiler_params=_cparams(("arbitrary", "arbitrary")),
        name="experts",
    )(tile_expert, n_active, xs, w_gu, w_down)


def _combine_kernel(slot_ref, x_ref, mod_ref, route_ref, gpost_ref, ys_ref, o_ref, buf_ref, sem, *, nt):
    i = pl.program_id(0)
    tm = x_ref.shape[0]

    def issue(b):
        for r in range(tm):
            for k in range(2):
                src = slot_ref[0, 0, 2 * r + k]
                pltpu.make_async_copy(ys_ref.at[pl.ds(src, 1), :], buf_ref.at[b, k, pl.ds(r, 1), :],
                                      sem.at[b]).start(priority=k)

    def drain(b):
        for k in range(2):
            pltpu.make_async_copy(ys_ref.at[pl.ds(0, tm), :], buf_ref.at[b, k], sem.at[b]).wait()

    def finish(b):
        route = route_ref[...]
        f = route[:, 4:5] * buf_ref[b, 0] + route[:, 5:6] * buf_ref[b, 1]
        mod = mod_ref[...]
        o_ref[...] = x_ref[...] + mod[5:6] * _rms(f, gpost_ref[...])

    @pl.when(i == 0)
    def _():
        issue(0)

    for parity in range(2):
        @pl.when(jnp.logical_and(jnp.logical_and(i > 0, i < nt), i % 2 == parity))
        def _(parity=parity):
            drain(1 - parity)
            issue(parity)
            finish(1 - parity)

    @pl.when(i == nt)
    def _():
        drain((nt - 1) % 2)
        finish((nt - 1) % 2)


def _combine(x, mods, route, g_post, ys, slots, seq, tm):
    n, d = x.shape
    nt = n // tm
    per_batch = seq // tm
    done = lambda i: jnp.maximum(i - 1, 0)
    return pl.pallas_call(
        functools.partial(_combine_kernel, nt=nt),
        grid=(nt + 1,),
        in_specs=[pl.BlockSpec((1, 1, 2 * tm), lambda i: (jnp.minimum(i, nt - 1), 0, 0), memory_space=pltpu.SMEM),
                  pl.BlockSpec((tm, d), lambda i: (done(i), 0)),
                  pl.BlockSpec((None, 6, d), lambda i: (done(i) // per_batch, 0, 0)),
                  pl.BlockSpec((tm, LANES), lambda i: (done(i), 0)),
                  _const_spec((1, d)),
                  pl.BlockSpec(memory_space=pl.ANY)],
        out_specs=pl.BlockSpec((tm, d), lambda i: (done(i), 0)),
        out_shape=jax.ShapeDtypeStruct((n, d), F32),
        scratch_shapes=[pltpu.VMEM((2, 2, tm, d), F32), pltpu.SemaphoreType.DMA((2,))],
        compiler_params=_cparams(("arbitrary",)),
        name="combine",
    )(slots.reshape(nt, 1, 2 * tm), x, mods, route, g_post, ys)


def _rope_tables(n):
    axis_dim = HEAD_DIM // 2
    pos = jnp.arange(n, dtype=jnp.int32)
    r = (pos // GRID_W).astype(F32)[:, None]
    col = (pos % GRID_W).astype(F32)[:, None]
    inv = 1.0 / (ROPE_THETA ** (jnp.arange(0, axis_dim, 2, dtype=F32) / axis_dim))
    ang = jnp.concatenate([r * inv, col * inv], axis=-1)
    cos, sin = jnp.cos(ang), jnp.sin(ang)
    zero = jnp.zeros_like(sin)
    reps = LANES // HEAD_DIM
    return (jnp.tile(jnp.concatenate([cos, cos], -1), (1, reps)),
            jnp.tile(jnp.concatenate([-sin, zero], -1), (1, reps)),
            jnp.tile(jnp.concatenate([zero, sin], -1), (1, reps)))


def _head_mean_matrix(width):
    idx = jnp.arange(width) // HEAD_DIM
    return jnp.where(idx[:, None] == idx[None, :], 1.0 / HEAD_DIM, 0.0).astype(BF16)


def kernel(x, c, ctx, c_ctx, e_w_mod, e_b_mod, e_g_pre_mix, e_g_post_mix, e_w_in, e_g_q, e_g_k, e_w_conv, e_w_out, e_g_pre_ffn, e_g_post_ffn, e_w_gate, e_w_up, e_w_down, o_w_mod, o_b_mod, o_g_pre_mix, o_g_post_mix, o_w_in, o_g_v, o_b_v, o_w_s, o_b_s, o_w_out, o_g_pre_ffn, o_g_post_ffn, o_w_router, o_b_router, o_w_gate, o_w_up, o_w_down):
    b, s, d = x.shape
    n_ctx = ctx.shape[1]
    n = b * s
    tm = min(TOKEN_TILE, s)
    tq = min(ATTN_Q_TILE, s)
    assert b + 1 <= 8 and s % tm == 0 and s % tq == 0 and tm % CHUNK == 0 and n_ctx % 16 == 0
    assert (s // tm) % ATTN_BLOCKS_PER_ITER == 0 and FF_CHUNK_EXPERT % FF_SUB == 0
    assert e_w_mod.shape[0] == 1 and o_w_mod.shape[0] == 1
    row = lambda g: g.reshape(1, -1)

    cond8 = jnp.zeros((8, d), F32).at[:b].set(c).at[b].set(c_ctx)
    mods_e = _modulation(cond8, e_w_mod[0], e_b_mod[0])
    mods_o = _modulation(cond8, o_w_mod[0], o_b_mod[0])

    w_in = e_w_in[0].astype(BF16)
    gq = jnp.tile(e_g_q[0], N_HEADS).reshape(1, ATTN_W)
    gk = jnp.tile(e_g_k[0], N_KV_HEADS).reshape(1, KV_W)
    eq, ek = _head_mean_matrix(ATTN_W), _head_mean_matrix(KV_W)
    cos, sa, sb = _rope_tables(s)
    qt, k, vt4, bg, z = _inproj(x, mods_e, None, row(e_g_pre_mix[0]), w_in, gq, gk, eq, ek, cos, sa, sb, tm)
    ones = jnp.ones((n_ctx, LANES), F32)
    zeros = jnp.zeros((n_ctx, LANES), F32)
    _, kc, vct4, _, _ = _inproj(ctx, mods_e, b, row(e_g_pre_mix[0]), w_in, gq, gk, eq, ek, ones, zeros, zeros,
                                n_ctx)
    score_bound = (ATTN_BOUND_MARGIN * HEAD_DIM ** 0.5 * LOG2E * jnp.max(jnp.abs(e_g_q[0]))
                   * jnp.max(jnp.abs(e_g_k[0]))).reshape(1).astype(F32)
    attn = _attention(score_bound, qt, k.reshape(b, s // tm, tm, KV_W), vt4, kc.reshape(b, 1, n_ctx, KV_W), vct4, tq)
    x1 = _outproj(x, mods_e, attn, bg, z, e_w_conv[0], e_w_out[0].astype(BF16), row(e_g_post_mix[0]), tm)

    x2 = _swiglu(x1, mods_e, row(e_g_pre_ffn[0]), e_w_gate[0].astype(BF16), e_w_up[0].astype(BF16),
                 e_w_down[0].astype(BF16), row(e_g_post_ffn[0]), tm)

    b_s_b = jnp.broadcast_to(o_b_s[0][:, :, None], (GM_GROUPS, CHUNK, d // GM_GROUPS))
    w_r32 = jnp.zeros((d, LANES), F32).at[:, :N_EXPERTS].set(o_w_router[0])
    w_r_hi = w_r32.astype(BF16)
    w_r = jnp.stack([w_r_hi, (w_r32 - w_r_hi.astype(F32)).astype(BF16)])
    b_r = jnp.full((1, LANES), NEG_BIG, F32).at[0, :N_EXPERTS].set(o_b_router[0])
    x3, h3, route, counts = _gmlp_router(
        x2, mods_o, row(o_g_pre_mix[0]), o_w_in[0].astype(BF16), row(o_g_v[0]), row(o_b_v[0]),
        o_w_s[0].astype(BF16), b_s_b, o_w_out[0].astype(BF16), row(o_g_post_mix[0]), row(o_g_pre_ffn[0]),
        w_r, b_r, tm)

    te_rows = EXPERT_TILE
    n_tiles = -(-(2 * n + N_EXPERTS * (te_rows - 1)) // te_rows)
    n_slots = n_tiles * te_rows
    cnt = counts[0, :N_EXPERTS].astype(jnp.int32)
    tiles_per = (cnt + te_rows - 1) // te_rows
    tile_end = jnp.cumsum(tiles_per)
    base = (tile_end - tiles_per) * te_rows
    route2 = route.reshape(n, LANES)
    e12 = route2[:, 0:2].astype(jnp.int32)
    r12 = route2[:, 2:4].astype(jnp.int32)
    slots = (base[e12] + r12).reshape(-1)
    tile_expert = jnp.minimum(
        jnp.sum(jnp.arange(n_tiles, dtype=jnp.int32)[:, None] >= tile_end[None, :], axis=1),
        N_EXPERTS - 1).astype(jnp.int32)
    n_active = tile_end[-1:].astype(jnp.int32)

    rt = min(ROUTE_TILE, s)
    xs = _dispatch(h3.reshape(n, d), slots, n_slots, rt)
    w_gu = _interleave_gate_up(o_w_gate[0], o_w_up[0])
    ys = _experts(tile_expert, n_active, xs, w_gu, o_w_down[0].astype(BF16), te_rows, FF_CHUNK_EXPERT)
    out = _combine(x3.reshape(n, d), mods_o, route2, row(o_g_post_ffn[0]), ys, slots, s, rt)
    return out.reshape(b, s, d)
```

```python
import functools
import math

import jax
import jax.numpy as jnp
from jax import lax
from jax.experimental import pallas as pl
from jax.experimental.pallas import tpu as pltpu

F32 = jnp.float32
BF16 = jnp.bfloat16

EPS = 1e-6
GRID_W = 64
N_HEADS = 8
N_KV_HEADS = 2
HEAD_DIM = 64
ATTN_W = N_HEADS * HEAD_DIM
KV_W = N_KV_HEADS * HEAD_DIM
ROPE_THETA = 10000.0
CHUNK = 128
GM_GROUPS = 8
N_EXPERTS = 8
LANES = 128
LOG2E = 1.4426950408889634
NEG_BIG = -1e30

VMEM_LIMIT_BYTES = 56 * 1024 * 1024

TOKEN_TILE = 512
ATTN_Q_TILE = 512
ATTN_BLOCKS_PER_ITER = 4
ATTN_LOOKAHEAD = 2
ATTN_HEADROOM = 40.0
ATTN_BOUND_MARGIN = 1.05
ATTN_MIN_ROW_SUM = 2.0 ** -20
ATTN_MAX_ROW_SUM = 2.0 ** 100
FF_CHUNK_DENSE = 2816
GMLP_ROW_PARTS = 2
ROW_PARTS = 2
FF_CHUNK_EXPERT = 1792
FF_SUB = 256
EXPERT_TILE = 1024
ROUTE_TILE = 256


def _cparams(sem):
    return pltpu.CompilerParams(dimension_semantics=sem, vmem_limit_bytes=VMEM_LIMIT_BYTES)


def _const_spec(shape):
    n = len(shape)
    return pl.BlockSpec(shape, lambda *_: (0,) * n, pipeline_mode=pl.Buffered(1))


def _rms(x, g):
    return x * lax.rsqrt(jnp.mean(x * x, axis=-1, keepdims=True) + EPS) * g


def _pre(x, g, shift, scale):
    return _rms(x, g) * (1.0 + scale) + shift


def _bdot(a, b):
    return jnp.dot(a, b, preferred_element_type=F32)


def _mod_kernel(c_ref, w_ref, b_ref, o_ref):
    a = jax.nn.silu(c_ref[...])
    o_ref[...] = jnp.dot(a, w_ref[...], precision=lax.Precision.HIGHEST,
                         preferred_element_type=F32) + b_ref[...]


def _modulation(cond8, w_mod, b_mod):
    d = cond8.shape[1]
    out = pl.pallas_call(
        _mod_kernel,
        grid=(6,),
        in_specs=[pl.BlockSpec((8, d), lambda j: (0, 0)),
                  pl.BlockSpec((d, d), lambda j: (0, j)),
                  pl.BlockSpec((1, d), lambda j: (0, j))],
        out_specs=pl.BlockSpec((8, d), lambda j: (0, j)),
        out_shape=jax.ShapeDtypeStruct((8, 6 * d), F32),
        compiler_params=_cparams(("arbitrary",)),
        name="modulation",
    )(cond8, w_mod, b_mod.reshape(1, 6 * d))
    return out.reshape(8, 6, d)


def _rope128(t, cos, sa, sb):
    return t * cos + pltpu.roll(t, 96, 1) * sa + pltpu.roll(t, 32, 1) * sb


def _inproj_kernel(x_ref, mod_ref, gpre_ref, w_ref, gq_ref, gk_ref, eq_ref, ek_ref,
                   cos_ref, sa_ref, sb_ref, qt_ref, k_ref, vt_ref, bg_ref, z_ref, *, q_scale):
    mod = mod_ref[...]
    h = _pre(x_ref[...], gpre_ref[...], mod[0:1], mod[1:2]).astype(BF16)
    cos, sa, sb = cos_ref[...], sa_ref[...], sb_ref[...]

    q = _bdot(h, w_ref[:, 0:ATTN_W])
    ms = _bdot((q * q).astype(BF16), eq_ref[...])
    qn = q * lax.rsqrt(ms + EPS) * gq_ref[...]
    qr = jnp.concatenate(
        [_rope128(qn[:, LANES * j:LANES * (j + 1)], cos, sa, sb) for j in range(ATTN_W // LANES)], axis=1)
    qt_ref[...] = (qr * q_scale).T.astype(BF16)

    kv = _bdot(h, w_ref[:, ATTN_W:ATTN_W + 2 * KV_W])
    k = kv[:, 0:KV_W]
    msk = _bdot((k * k).astype(BF16), ek_ref[...])
    kn = k * lax.rsqrt(msk + EPS) * gk_ref[...]
    k_ref[...] = _rope128(kn, cos, sa, sb).astype(BF16)
    vt_ref[...] = kv[:, KV_W:2 * KV_W].T.astype(BF16)

    o = ATTN_W + 2 * KV_W
    cw = (w_ref.shape[1] - o) // 3
    bg_ref[...] = _bdot(h, w_ref[:, o:o + cw]).astype(BF16)
    cg = _bdot(h, w_ref[:, o + cw:o + 2 * cw])
    hv = _bdot(h, w_ref[:, o + 2 * cw:o + 3 * cw])
    z_ref[...] = (cg * hv).astype(BF16)


def _inproj(x, mods, mod_row, g_pre, w_in, gq, gk, eq, ek, cos, sa, sb, tm):
    b, s, d = x.shape
    nt = s // tm
    cw = (w_in.shape[1] - ATTN_W - 2 * KV_W) // 3
    row = (lambda bi: bi) if mod_row is None else (lambda bi: mod_row)
    tok = lambda bi, i: (bi, i, 0)
    return pl.pallas_call(
        functools.partial(_inproj_kernel, q_scale=HEAD_DIM ** -0.5 * LOG2E),
        grid=(b, nt),
        in_specs=[pl.BlockSpec((None, tm, d), tok),
                  pl.BlockSpec((None, 6, d), lambda bi, i: (row(bi), 0, 0)),
                  _const_spec((1, d)),
                  _const_spec(w_in.shape),
                  _const_spec((1, ATTN_W)),
                  _const_spec((1, KV_W)),
                  _const_spec((ATTN_W, ATTN_W)),
                  _const_spec((KV_W, KV_W)),
                  pl.BlockSpec((tm, LANES), lambda bi, i: (i, 0)),
                  pl.BlockSpec((tm, LANES), lambda bi, i: (i, 0)),
                  pl.BlockSpec((tm, LANES), lambda bi, i: (i, 0))],
        out_specs=[pl.BlockSpec((None, ATTN_W, tm), lambda bi, i: (bi, 0, i)),
                   pl.BlockSpec((None, tm, KV_W), tok),
                   pl.BlockSpec((None, None, KV_W, tm), lambda bi, i: (bi, i, 0, 0)),
                   pl.BlockSpec((None, tm, cw), tok),
                   pl.BlockSpec((None, tm, cw), tok)],
        out_shape=[jax.ShapeDtypeStruct((b, ATTN_W, s), BF16),
                   jax.ShapeDtypeStruct((b, s, KV_W), BF16),
                   jax.ShapeDtypeStruct((b, nt, KV_W, tm), BF16),
                   jax.ShapeDtypeStruct((b, s, cw), BF16),
                   jax.ShapeDtypeStruct((b, s, cw), BF16)],
        compiler_params=_cparams(("parallel", "parallel")),
        name="inproj",
    )(x, mods, g_pre, w_in, gq, gk, eq, ek, cos, sa, sb)


def _attn_kernel(bound_ref, qt_ref, k_ref, vt_ref, kc_ref, vct_ref, o_ref, qp_ref, m_ref, acc_ref, ot_ref,
                 s_ref):
    n_kb = k_ref.shape[0]
    group = N_HEADS // N_KV_HEADS

    for h in range(N_HEADS):
        qh = qt_ref[HEAD_DIM * h:HEAD_DIM * (h + 1), :]
        zq = jnp.zeros_like(qh)
        qp_ref[h] = jnp.concatenate([qh, zq] if h // group == 0 else [zq, qh], axis=0)

    look = ATTN_LOOKAHEAD

    def values_aug(vtb, h):
        g = h // group
        return jnp.concatenate([vtb[HEAD_DIM * g:HEAD_DIM * (g + 1), :], jnp.ones((16, vtb.shape[1]), BF16)],
                               axis=0)

    def attend_exact(s, vtb, h, first):
        mb = jnp.max(s, axis=0, keepdims=True)
        m_old = None if first else m_ref[h:h + 1, :]
        m_new = mb if first else jnp.maximum(m_old, mb)
        p = jnp.exp2(s - m_new).astype(BF16)
        pv = _bdot(values_aug(vtb, h), p)
        acc_ref[h] = pv if first else acc_ref[h] * jnp.exp2(m_old - m_new) + pv
        m_ref[h:h + 1, :] = m_new

    def attend_stream(s, vtb, h, first):
        del first
        m_cur = m_ref[h:h + 1, :]
        p = jnp.exp2(s - m_cur).astype(BF16)
        pv = _bdot(values_aug(vtb, h), p)
        m_new = jnp.maximum(m_cur, jnp.max(s, axis=0, keepdims=True))
        acc_ref[h] = (acc_ref[h] + pv) * jnp.exp2(m_cur - m_new)
        m_ref[h:h + 1, :] = m_new

    def run(pending, blocks, kb_after, attend, first):
        n_steps = len(blocks) * N_HEADS
        for n in range(n_steps):
            s = pending.pop(0)
            ahead = n + look
            kb = blocks[ahead // N_HEADS][0] if ahead < n_steps else kb_after
            pending.append(_bdot(kb, qp_ref[ahead % N_HEADS]))
            attend(s, blocks[n // N_HEADS][1], n % N_HEADS, first)
        return pending

    def all_keys(attend, floor):
        kc = kc_ref[0]
        pending = run([_bdot(kc, qp_ref[h]) for h in range(look)], [(kc, vct_ref[0])], k_ref[0], attend_exact,
                      True)
        for l in range(look):
            s_ref[l] = pending[l]
        if floor is not None:
            for h in range(N_HEADS):
                m_cur = m_ref[h:h + 1, :]
                m_new = jnp.maximum(m_cur, floor)
                acc_ref[h] = acc_ref[h] * jnp.exp2(m_cur - m_new)
                m_ref[h:h + 1, :] = m_new

        per_iter = math.gcd(ATTN_BLOCKS_PER_ITER, n_kb)

        def body(i, carry):
            first_blk = i * per_iter
            blocks = [(k_ref[first_blk + j], vt_ref[first_blk + j]) for j in range(per_iter)]
            kb_after = k_ref[jnp.minimum(first_blk + per_iter, n_kb - 1)]
            pending = run([s_ref[l] for l in range(look)], blocks, kb_after, attend, False)
            for l in range(look):
                s_ref[l] = pending[l]
            return carry

        lax.fori_loop(0, n_kb // per_iter, body, 0)

    all_keys(attend_stream, bound_ref[0] - ATTN_HEADROOM)
    sums = jnp.concatenate([acc_ref[h, HEAD_DIM:HEAD_DIM + 1, :] for h in range(N_HEADS)], axis=0)
    sound = jnp.logical_and(sums >= ATTN_MIN_ROW_SUM, sums <= ATTN_MAX_ROW_SUM)
    n_unsound = jnp.sum(jnp.where(sound, 0.0, 1.0))

    @pl.when(n_unsound > 0.0)
    def _():
        all_keys(attend_exact, None)

    for h in range(N_HEADS):
        acc = acc_ref[h]
        ot_ref[HEAD_DIM * h:HEAD_DIM * (h + 1), :] = acc[0:HEAD_DIM] / acc[HEAD_DIM:HEAD_DIM + 1]
    o_ref[...] = ot_ref[...].T.astype(BF16)


def _attention(score_bound, qt, k4, vt4, kc4, vct4, tq):
    b, _, s = qt.shape
    _, n_kb, tk, _ = k4.shape
    ctx = kc4.shape[2]
    return pl.pallas_call(
        _attn_kernel,
        grid=(b, s // tq),
        in_specs=[pl.BlockSpec(memory_space=pltpu.SMEM),
                  pl.BlockSpec((None, ATTN_W, tq), lambda bi, i: (bi, 0, i)),
                  pl.BlockSpec((None, n_kb, tk, KV_W), lambda bi, i: (bi, 0, 0, 0)),
                  pl.BlockSpec((None, n_kb, KV_W, tk), lambda bi, i: (bi, 0, 0, 0)),
                  pl.BlockSpec((None, 1, ctx, KV_W), lambda bi, i: (bi, 0, 0, 0)),
                  pl.BlockSpec((None, 1, KV_W, ctx), lambda bi, i: (bi, 0, 0, 0))],
        out_specs=pl.BlockSpec((None, tq, ATTN_W), lambda bi, i: (bi, i, 0)),
        out_shape=jax.ShapeDtypeStruct((b, s, ATTN_W), BF16),
        scratch_shapes=[pltpu.VMEM((N_HEADS, KV_W, tq), BF16),
                        pltpu.VMEM((N_HEADS, tq), F32),
                        pltpu.VMEM((N_HEADS, HEAD_DIM + 16, tq), F32),
                        pltpu.VMEM((ATTN_W, tq), F32),
                        pltpu.VMEM((ATTN_LOOKAHEAD, tk, tq), F32)],
        compiler_params=_cparams(("parallel", "parallel")),
        name="attention",
    )(score_bound, qt, k4, vt4, kc4, vct4)


def _outproj_kernel(x_ref, mod_ref, attn_ref, bg_ref, z_ref, zp_ref, zn_ref, wc_ref, wo_ref, gpost_ref,
                    o_ref):
    i = pl.program_id(1)
    nt = pl.num_programs(1)
    tm = z_ref.shape[0]
    z = z_ref[...].astype(F32)
    halo = zp_ref.shape[0]
    zprev = zp_ref[...].astype(F32)[halo - 1:halo, :] * (i > 0).astype(F32)
    znext = zn_ref[...].astype(F32)[0:1, :] * (i < nt - 1).astype(F32)
    row = lax.broadcasted_iota(jnp.int32, z.shape, 0)
    zm1 = jnp.where(row == 0, zprev, pltpu.roll(z, 1, 0))
    zp1 = jnp.where(row == tm - 1, znext, pltpu.roll(z, tm - 1, 0))
    wc = wc_ref[...]
    taps = wc[0:1] * zm1 + wc[1:2] * z + wc[2:3] * zp1
    mod = mod_ref[...]
    n_parts = ROW_PARTS if tm % (ROW_PARTS * 16) == 0 else 1
    rows = tm // n_parts
    ys = []
    for p in range(n_parts):
        sl = slice(rows * p, rows * (p + 1))
        conv = (bg_ref[sl, :].astype(F32) * taps[sl, :]).astype(BF16)
        ys.append(_bdot(attn_ref[sl, :], wo_ref[0:ATTN_W, :]) + _bdot(conv, wo_ref[ATTN_W:, :]))
    for p in range(n_parts):
        sl = slice(rows * p, rows * (p + 1))
        o_ref[sl, :] = x_ref[sl, :] + mod[2:3] * _rms(ys[p], gpost_ref[...])


def _outproj(x, mods, attn, bg, z, w_conv, w_out, g_post, tm):
    b, s, d = x.shape
    cw = z.shape[2]
    halo = 16
    r = tm // halo
    last = s // halo - 1
    tok = lambda bi, i: (bi, i, 0)
    return pl.pallas_call(
        _outproj_kernel,
        grid=(b, s // tm),
        in_specs=[pl.BlockSpec((None, tm, d), tok),
                  pl.BlockSpec((None, 6, d), lambda bi, i: (bi, 0, 0)),
                  pl.BlockSpec((None, tm, ATTN_W), tok),
                  pl.BlockSpec((None, tm, cw), tok),
                  pl.BlockSpec((None, tm, cw), tok),
                  pl.BlockSpec((None, halo, cw), lambda bi, i: (bi, jnp.maximum(i * r - 1, 0), 0)),
                  pl.BlockSpec((None, halo, cw), lambda bi, i: (bi, jnp.minimum((i + 1) * r, last), 0)),
                  _const_spec(w_conv.shape),
                  _const_spec(w_out.shape),
                  _const_spec((1, d))],
        out_specs=pl.BlockSpec((None, tm, d), tok),
        out_shape=jax.ShapeDtypeStruct((b, s, d), F32),
        compiler_params=_cparams(("parallel", "parallel")),
        name="outproj",
    )(x, mods, attn, bg, z, z, z, w_conv, w_out, g_post)


def _swiglu_kernel(x_ref, mod_ref, gpre_ref, wg_ref, wu_ref, wd_ref, gpost_ref, o_ref, *, chunk):
    mod = mod_ref[...]
    tm = x_ref.shape[0]
    n_parts = ROW_PARTS if tm % (ROW_PARTS * 16) == 0 else 1
    rows = tm // n_parts
    xs = [x_ref[rows * p:rows * (p + 1), :] for p in range(n_parts)]
    hs = [_pre(xs[p], gpre_ref[...], mod[3:4], mod[4:5]).astype(BF16) for p in range(n_parts)]
    accs = [None] * n_parts
    for c in range(wg_ref.shape[1] // chunk):
        sl = slice(c * chunk, (c + 1) * chunk)
        for p in range(n_parts):
            a = (jax.nn.silu(_bdot(hs[p], wg_ref[:, sl])) * _bdot(hs[p], wu_ref[:, sl])).astype(BF16)
            part = _bdot(a, wd_ref[sl, :])
            accs[p] = part if accs[p] is None else accs[p] + part
    for p in range(n_parts):
        o_ref[rows * p:rows * (p + 1), :] = xs[p] + mod[5:6] * _rms(accs[p], gpost_ref[...])


def _swiglu(x, mods, g_pre, w_gate, w_up, w_down, g_post, tm):
    b, s, d = x.shape
    tok = lambda bi, i: (bi, i, 0)
    return pl.pallas_call(
        functools.partial(_swiglu_kernel, chunk=min(FF_CHUNK_DENSE, w_gate.shape[1])),
        grid=(b, s // tm),
        in_specs=[pl.BlockSpec((None, tm, d), tok),
                  pl.BlockSpec((None, 6, d), lambda bi, i: (bi, 0, 0)),
                  _const_spec((1, d)),
                  _const_spec(w_gate.shape),
                  _const_spec(w_up.shape),
                  _const_spec(w_down.shape),
                  _const_spec((1, d))],
        out_specs=pl.BlockSpec((None, tm, d), tok),
        out_shape=jax.ShapeDtypeStruct((b, s, d), F32),
        compiler_params=_cparams(("parallel", "parallel")),
        name="swiglu",
    )(x, mods, g_pre, w_gate, w_up, w_down, g_post)


def _gmlp_kernel(x_ref, mod_ref, gpre_ref, win_ref, gv_ref, bv_ref, ws_ref, bs_ref, wout_ref, gpost_ref,
                 gffn_ref, wr_ref, br_ref, tri_ref,
                 x3_ref, h3_ref, route_ref, cnt_ref, run_ref):
    first = jnp.logical_and(pl.program_id(0) == 0, pl.program_id(1) == 0)

    @pl.when(first)
    def _():
        run_ref[...] = jnp.zeros_like(run_ref)

    mod = mod_ref[...]
    tm, d = x_ref.shape
    gc = d // GM_GROUPS
    n_parts = GMLP_ROW_PARTS if tm % (GMLP_ROW_PARTS * CHUNK) == 0 else 1
    rows = tm // n_parts
    n_chunks = rows // CHUNK
    parts = range(n_parts)
    xs = [x_ref[rows * i:rows * (i + 1), :] for i in parts]
    hs = [_pre(xs[i], gpre_ref[...], mod[0:1], mod[1:2]).astype(BF16) for i in parts]
    us, vns = [], []
    for i in parts:
        us.append(jax.nn.gelu(_bdot(hs[i], win_ref[:, 0:d])))
        v = jax.nn.gelu(_bdot(hs[i], win_ref[:, d:2 * d]))
        vc = v - jnp.mean(v, axis=-1, keepdims=True)
        var = jnp.mean(vc * vc, axis=-1, keepdims=True)
        vns.append((vc * lax.rsqrt(var + EPS) * gv_ref[...] + bv_ref[...]).astype(BF16))
    gated = []
    for i in parts:
        mixed = []
        for g in range(GM_GROUPS):
            rhs = jnp.concatenate(
                [vns[i][CHUNK * c:CHUNK * (c + 1), gc * g:gc * (g + 1)] for c in range(n_chunks)],
                axis=1)
            mixed.append(_bdot(ws_ref[g], rhs) + jnp.concatenate([bs_ref[g]] * n_chunks, axis=1))
        s = jnp.concatenate(
            [jnp.concatenate([mixed[g][:, gc * c:gc * (c + 1)] for g in range(GM_GROUPS)], axis=1)
             for c in range(n_chunks)], axis=0)
        gated.append((us[i] * s).astype(BF16))
    h3s = []
    for i in parts:
        y = _bdot(gated[i], wout_ref[...])
        x3 = xs[i] + mod[2:3] * _rms(y, gpost_ref[...])
        x3_ref[rows * i:rows * (i + 1), :] = x3
        h3 = _pre(x3, gffn_ref[...], mod[3:4], mod[4:5])
        h3_ref[rows * i:rows * (i + 1), :] = h3
        h3s.append(h3)
    logit_parts = []
    for i in parts:
        h_hi = h3s[i].astype(BF16)
        h_lo = (h3s[i] - h_hi.astype(F32)).astype(BF16)
        logit_parts.append(_bdot(h_hi, wr_ref[0]) + _bdot(h_lo, wr_ref[0]) + _bdot(h_hi, wr_ref[1]))
    logits = jnp.concatenate(logit_parts, axis=0) + br_ref[...]
    lane = lax.broadcasted_iota(jnp.int32, logits.shape, 1)
    m1 = jnp.max(logits, axis=1, keepdims=True)
    i1 = jnp.min(jnp.where(logits == m1, lane, LANES), axis=1, keepdims=True)
    rest = jnp.where(lane == i1, 2.0 * NEG_BIG, logits)
    m2 = jnp.max(rest, axis=1, keepdims=True)
    i2 = jnp.min(jnp.where(rest == m2, lane, LANES), axis=1, keepdims=True)
    e21 = jnp.exp(m2 - m1)
    w1 = 1.0 / (1.0 + e21)
    w2 = e21 / (1.0 + e21)
    hot1 = lane == i1
    hot2 = lane == i2
    onehot = jnp.where(jnp.logical_or(hot1, hot2), 1.0, 0.0)
    before = _bdot(tri_ref[...], onehot.astype(BF16)) + run_ref[...]
    r1 = jnp.sum(jnp.where(hot1, before, 0.0), axis=1, keepdims=True)
    r2 = jnp.sum(jnp.where(hot2, before, 0.0), axis=1, keepdims=True)
    run = run_ref[...] + jnp.sum(onehot, axis=0, keepdims=True)
    run_ref[...] = run
    cnt_ref[...] = jnp.broadcast_to(run, cnt_ref.shape)
    fields = (i1.astype(F32), i2.astype(F32), r1, r2, w1, w2)
    route = jnp.zeros(logits.shape, F32)
    for j, f in enumerate(fields):
        route = jnp.where(lane == j, f, route)
    route_ref[...] = route


def _gmlp_router(x, mods, g_pre, w_in, g_v, b_v, w_s, b_s_b, w_out, g_post, g_ffn, w_r, b_r, tm):
    b, s, d = x.shape
    tok = lambda bi, i: (bi, i, 0)
    tri = jnp.tril(jnp.ones((tm, tm), F32), -1).astype(BF16)
    return pl.pallas_call(
        _gmlp_kernel,
        grid=(b, s // tm),
        in_specs=[pl.BlockSpec((None, tm, d), tok),
                  pl.BlockSpec((None, 6, d), lambda bi, i: (bi, 0, 0)),
                  _const_spec((1, d)),
                  _const_spec(w_in.shape),
                  _const_spec((1, d)),
                  _const_spec((1, d)),
                  _const_spec(w_s.shape),
                  _const_spec(b_s_b.shape),
                  _const_spec(w_out.shape),
                  _const_spec((1, d)),
                  _const_spec((1, d)),
                  _const_spec(w_r.shape),
                  _const_spec((1, LANES)),
                  _const_spec((tm, tm))],
        out_specs=[pl.BlockSpec((None, tm, d), tok),
                   pl.BlockSpec((None, tm, d), tok),
                   pl.BlockSpec((None, tm, LANES), tok),
                   pl.BlockSpec((8, LANES), lambda bi, i: (0, 0))],
        out_shape=[jax.ShapeDtypeStruct((b, s, d), F32),
                   jax.ShapeDtypeStruct((b, s, d), F32),
                   jax.ShapeDtypeStruct((b, s, LANES), F32),
                   jax.ShapeDtypeStruct((8, LANES), F32)],
        scratch_shapes=[pltpu.VMEM((1, LANES), F32)],
        compiler_params=_cparams(("arbitrary", "arbitrary")),
        name="gmlp_router",
    )(x, mods, g_pre, w_in, g_v, b_v, w_s, b_s_b, w_out, g_post, g_ffn, w_r, b_r, tri)


def _dispatch_kernel(slot_ref, h_ref, xs_in_ref, xs_ref, sem):
    del xs_in_ref
    tm = h_ref.shape[0]
    for r in range(tm):
        for k in range(2):
            dst = slot_ref[0, 0, 2 * r + k]
            pltpu.make_async_copy(h_ref.at[pl.ds(r, 1), :], xs_ref.at[pl.ds(dst, 1), :], sem).start(priority=k)
    for _ in range(2):
        pltpu.make_async_copy(h_ref, xs_ref.at[pl.ds(0, tm), :], sem).wait()


def _dispatch(h, slots, n_slots, tm):
    n, d = h.shape
    nt = n // tm
    return pl.pallas_call(
        _dispatch_kernel,
        grid=(nt,),
        in_specs=[pl.BlockSpec((1, 1, 2 * tm), lambda i: (i, 0, 0), memory_space=pltpu.SMEM),
                  pl.BlockSpec((tm, d), lambda i: (i, 0)),
                  pl.BlockSpec(memory_space=pl.ANY)],
        out_specs=pl.BlockSpec(memory_space=pl.ANY),
        out_shape=jax.ShapeDtypeStruct((n_slots, d), F32),
        scratch_shapes=[pltpu.SemaphoreType.DMA(())],
        input_output_aliases={2: 0},
        compiler_params=_cparams(("arbitrary",)),
        name="dispatch",
    )(slots.reshape(nt, 1, 2 * tm), h, jnp.zeros((n_slots, d), F32))


def _expert_kernel(te_ref, na_ref, xs_ref, wgu_ref, wd_ref, ys_ref, xb_ref):
    t = pl.program_id(0)
    c = pl.program_id(1)
    active = t < na_ref[0]

    @pl.when(jnp.logical_and(active, c == 0))
    def _():
        xb_ref[...] = xs_ref[...].astype(BF16)

    @pl.when(active)
    def _():
        xb = xb_ref[...]
        acts = []
        for j in range(wd_ref.shape[0] // FF_SUB):
            gu = _bdot(xb, wgu_ref[:, 2 * FF_SUB * j:2 * FF_SUB * (j + 1)])
            acts.append((jax.nn.silu(gu[:, 0:FF_SUB]) * gu[:, FF_SUB:2 * FF_SUB]).astype(BF16))
        part = _bdot(jnp.concatenate(acts, axis=1), wd_ref[...])

        @pl.when(c == 0)
        def _():
            ys_ref[...] = part

        @pl.when(c > 0)
        def _():
            ys_ref[...] += part

    @pl.when(jnp.logical_and(jnp.logical_not(active), c == 0))
    def _():
        ys_ref[...] = jnp.zeros_like(ys_ref)


def _gate_up_kernel(wg_ref, wu_ref, o_ref):
    for j in range(wg_ref.shape[1] // FF_SUB):
        src = slice(FF_SUB * j, FF_SUB * (j + 1))
        o_ref[:, 2 * FF_SUB * j:2 * FF_SUB * j + FF_SUB] = wg_ref[:, src].astype(BF16)
        o_ref[:, 2 * FF_SUB * j + FF_SUB:2 * FF_SUB * (j + 1)] = wu_ref[:, src].astype(BF16)


def _interleave_gate_up(w_gate, w_up):
    ne, d, ff = w_gate.shape
    cols = 2 * FF_SUB
    spec = pl.BlockSpec((None, d, cols), lambda e, j: (e, 0, j))
    return pl.pallas_call(
        _gate_up_kernel,
        grid=(ne, ff // cols),
        in_specs=[spec, spec],
        out_specs=pl.BlockSpec((None, d, 2 * cols), lambda e, j: (e, 0, j)),
        out_shape=jax.ShapeDtypeStruct((ne, d, 2 * ff), BF16),
        compiler_params=_cparams(("parallel", "parallel")),
        name="gate_up_weights",
    )(w_gate, w_up)


def _experts(tile_expert, n_active, xs, w_gu, w_down, tm, chunk):
    n_slots, d = xs.shape
    ff = w_down.shape[1]
    n_tiles = n_slots // tm
    last = ff // chunk - 1

    def live(t, na):
        return jnp.minimum(t, na[0] - 1)

    def chunk_of(t, c, na):
        return jnp.where(t < na[0], c, last)

    grid_spec = pltpu.PrefetchScalarGridSpec(
        num_scalar_prefetch=2,
        grid=(n_tiles, ff // chunk),
        in_specs=[pl.BlockSpec((tm, d), lambda t, c, te, na: (live(t, na), 0)),
                  pl.BlockSpec((None, d, 2 * chunk), lambda t, c, te, na: (te[live(t, na)], 0, chunk_of(t, c, na))),
                  pl.BlockSpec((None, chunk, d), lambda t, c, te, na: (te[live(t, na)], chunk_of(t, c, na), 0))],
        out_specs=pl.BlockSpec((tm, d), lambda t, c, te, na: (t, 0)),
        scratch_shapes=[pltpu.VMEM((tm, d), BF16)],
    )
    return pl.pallas_call(
        _expert_kernel,
        grid_spec=grid_spec,
        out_shape=jax.ShapeDtypeStruct((n_slots, d), F32),
        compiler_params=_cparams(("arbitrary", "arbitrary")),
        name="experts",
    )(tile_expert, n_active, xs, w_gu, w_down)


def _combine_kernel(slot_ref, x_ref, mod_ref, route_ref, gpost_ref, ys_ref, o_ref, buf_ref, sem, *, nt):
    i = pl.program_id(0)
    tm = x_ref.shape[0]

    def issue(b):
        for r in range(tm):
            for k in range(2):
                src = slot_ref[0, 0, 2 * r + k]
                pltpu.make_async_copy(ys_ref.at[pl.ds(src, 1), :], buf_ref.at[b, k, pl.ds(r, 1), :],
                                      sem.at[b]).start(priority=k)

    def drain(b):
        for k in range(2):
            pltpu.make_async_copy(ys_ref.at[pl.ds(0, tm), :], buf_ref.at[b, k], sem.at[b]).wait()

    def finish(b):
        route = route_ref[...]
        f = route[:, 4:5] * buf_ref[b, 0] + route[:, 5:6] * buf_ref[b, 1]
        mod = mod_ref[...]
        o_ref[...] = x_ref[...] + mod[5:6] * _rms(f, gpost_ref[...])

    @pl.when(i == 0)
    def _():
        issue(0)

    for parity in range(2):
        @pl.when(jnp.logical_and(jnp.logical_and(i > 0, i < nt), i % 2 == parity))
        def _(parity=parity):
            drain(1 - parity)
            issue(parity)
            finish(1 - parity)

    @pl.when(i == nt)
    def _():
        drain((nt - 1) % 2)
        finish((nt - 1) % 2)


def _combine(x, mods, route, g_post, ys, slots, seq, tm):
    n, d = x.shape
    nt = n // tm
    per_batch = seq // tm
    done = lambda i: jnp.maximum(i - 1, 0)
    return pl.pallas_call(
        functools.partial(_combine_kernel, nt=nt),
        grid=(nt + 1,),
        in_specs=[pl.BlockSpec((1, 1, 2 * tm), lambda i: (jnp.minimum(i, nt - 1), 0, 0), memory_space=pltpu.SMEM),
                  pl.BlockSpec((tm, d), lambda i: (done(i), 0)),
                  pl.BlockSpec((None, 6, d), lambda i: (done(i) // per_batch, 0, 0)),
                  pl.BlockSpec((tm, LANES), lambda i: (done(i), 0)),
                  _const_spec((1, d)),
                  pl.BlockSpec(memory_space=pl.ANY)],
        out_specs=pl.BlockSpec((tm, d), lambda i: (done(i), 0)),
        out_shape=jax.ShapeDtypeStruct((n, d), F32),
        scratch_shapes=[pltpu.VMEM((2, 2, tm, d), F32), pltpu.SemaphoreType.DMA((2,))],
        compiler_params=_cparams(("arbitrary",)),
        name="combine",
    )(slots.reshape(nt, 1, 2 * tm), x, mods, route, g_post, ys)


def _rope_tables(n):
    axis_dim = HEAD_DIM // 2
    pos = jnp.arange(n, dtype=jnp.int32)
    r = (pos // GRID_W).astype(F32)[:, None]
    col = (pos % GRID_W).astype(F32)[:, None]
    inv = 1.0 / (ROPE_THETA ** (jnp.arange(0, axis_dim, 2, dtype=F32) / axis_dim))
    ang = jnp.concatenate([r * inv, col * inv], axis=-1)
    cos, sin = jnp.cos(ang), jnp.sin(ang)
    zero = jnp.zeros_like(sin)
    reps = LANES // HEAD_DIM
    return (jnp.tile(jnp.concatenate([cos, cos], -1), (1, reps)),
            jnp.tile(jnp.concatenate([-sin, zero], -1), (1, reps)),
            jnp.tile(jnp.concatenate([zero, sin], -1), (1, reps)))


def _head_mean_matrix(width):
    idx = jnp.arange(width) // HEAD_DIM
    return jnp.where(idx[:, None] == idx[None, :], 1.0 / HEAD_DIM, 0.0).astype(BF16)


def kernel(x, c, ctx, c_ctx, e_w_mod, e_b_mod, e_g_pre_mix, e_g_post_mix, e_w_in, e_g_q, e_g_k, e_w_conv, e_w_out, e_g_pre_ffn, e_g_post_ffn, e_w_gate, e_w_up, e_w_down, o_w_mod, o_b_mod, o_g_pre_mix, o_g_post_mix, o_w_in, o_g_v, o_b_v, o_w_s, o_b_s, o_w_out, o_g_pre_ffn, o_g_post_ffn, o_w_router, o_b_router, o_w_gate, o_w_up, o_w_down):
    b, s, d = x.shape
    n_ctx = ctx.shape[1]
    n = b * s
    tm = min(TOKEN_TILE, s)
    tq = min(ATTN_Q_TILE, s)
    assert b + 1 <= 8 and s % tm == 0 and s % tq == 0 and tm % CHUNK == 0 and n_ctx % 16 == 0
    assert FF_CHUNK_EXPERT % FF_SUB == 0
    assert e_w_mod.shape[0] == 1 and o_w_mod.shape[0] == 1
    row = lambda g: g.reshape(1, -1)

    cond8 = jnp.zeros((8, d), F32).at[:b].set(c).at[b].set(c_ctx)
    mods_e = _modulation(cond8, e_w_mod[0], e_b_mod[0])
    mods_o = _modulation(cond8, o_w_mod[0], o_b_mod[0])

    w_in = e_w_in[0].astype(BF16)
    gq = jnp.tile(e_g_q[0], N_HEADS).reshape(1, ATTN_W)
    gk = jnp.tile(e_g_k[0], N_KV_HEADS).reshape(1, KV_W)
    eq, ek = _head_mean_matrix(ATTN_W), _head_mean_matrix(KV_W)
    cos, sa, sb = _rope_tables(s)
    qt, k, vt4, bg, z = _inproj(x, mods_e, None, row(e_g_pre_mix[0]), w_in, gq, gk, eq, ek, cos, sa, sb, tm)
    ones = jnp.ones((n_ctx, LANES), F32)
    zeros = jnp.zeros((n_ctx, LANES), F32)
    _, kc, vct4, _, _ = _inproj(ctx, mods_e, b, row(e_g_pre_mix[0]), w_in, gq, gk, eq, ek, ones, zeros, zeros,
                                n_ctx)
    score_bound = (ATTN_BOUND_MARGIN * HEAD_DIM ** 0.5 * LOG2E * jnp.max(jnp.abs(e_g_q[0]))
                   * jnp.max(jnp.abs(e_g_k[0]))).reshape(1).astype(F32)
    attn = _attention(score_bound, qt, k.reshape(b, s // tm, tm, KV_W), vt4, kc.reshape(b, 1, n_ctx, KV_W), vct4, tq)
    x1 = _outproj(x, mods_e, attn, bg, z, e_w_conv[0], e_w_out[0].astype(BF16), row(e_g_post_mix[0]), tm)

    x2 = _swiglu(x1, mods_e, row(e_g_pre_ffn[0]), e_w_gate[0].astype(BF16), e_w_up[0].astype(BF16),
                 e_w_down[0].astype(BF16), row(e_g_post_ffn[0]), tm)

    b_s_b = jnp.broadcast_to(o_b_s[0][:, :, None], (GM_GROUPS, CHUNK, d // GM_GROUPS))
    w_r32 = jnp.zeros((d, LANES), F32).at[:, :N_EXPERTS].set(o_w_router[0])
    w_r_hi = w_r32.astype(BF16)
    w_r = jnp.stack([w_r_hi, (w_r32 - w_r_hi.astype(F32)).astype(BF16)])
    b_r = jnp.full((1, LANES), NEG_BIG, F32).at[0, :N_EXPERTS].set(o_b_router[0])
    x3, h3, route, counts = _gmlp_router(
        x2, mods_o, row(o_g_pre_mix[0]), o_w_in[0].astype(BF16), row(o_g_v[0]), row(o_b_v[0]),
        o_w_s[0].astype(BF16), b_s_b, o_w_out[0].astype(BF16), row(o_g_post_mix[0]), row(o_g_pre_ffn[0]),
        w_r, b_r, tm)

    te_rows = EXPERT_TILE
    n_tiles = -(-(2 * n + N_EXPERTS * (te_rows - 1)) // te_rows)
    n_slots = n_tiles * te_rows
    cnt = counts[0, :N_EXPERTS].astype(jnp.int32)
    tiles_per = (cnt + te_rows - 1) // te_rows
    tile_end = jnp.cumsum(tiles_per)
    base = (tile_end - tiles_per) * te_rows
    route2 = route.reshape(n, LANES)
    e12 = route2[:, 0:2].astype(jnp.int32)
    r12 = route2[:, 2:4].astype(jnp.int32)
    slots = (base[e12] + r12).reshape(-1)
    tile_expert = jnp.minimum(
        jnp.sum(jnp.arange(n_tiles, dtype=jnp.int32)[:, None] >= tile_end[None, :], axis=1),
        N_EXPERTS - 1).astype(jnp.int32)
    n_active = tile_end[-1:].astype(jnp.int32)

    rt = min(ROUTE_TILE, s)
    xs = _dispatch(h3.reshape(n, d), slots, n_slots, rt)
    w_gu = _interleave_gate_up(o_w_gate[0], o_w_up[0])
    ys = _experts(tile_expert, n_active, xs, w_gu, o_w_down[0].astype(BF16), te_rows, FF_CHUNK_EXPERT)
    out = _combine(x3.reshape(n, d), mods_o, route2, row(o_g_post_ffn[0]), ys, slots, s, rt)
    return out.reshape(b, s, d)
```

```python
import functools
import math

import jax
import jax.numpy as jnp
from jax import lax
from jax.experimental import pallas as pl
from jax.experimental.pallas import tpu as pltpu

F32 = jnp.float32
BF16 = jnp.bfloat16

EPS = 1e-6
GRID_W = 64
N_HEADS = 8
N_KV_HEADS = 2
HEAD_DIM = 64
ATTN_W = N_HEADS * HEAD_DIM
KV_W = N_KV_HEADS * HEAD_DIM
ROPE_THETA = 10000.0
CHUNK = 128
GM_GROUPS = 8
N_EXPERTS = 8
LANES = 128
LOG2E = 1.4426950408889634
NEG_BIG = -1e30

VMEM_LIMIT_BYTES = 56 * 1024 * 1024

TOKEN_TILE = 512
ATTN_Q_TILE = 512
ATTN_BLOCKS_PER_ITER = 4
ATTN_LOOKAHEAD = 2
ATTN_HEADROOM = 40.0
ATTN_BOUND_MARGIN = 1.05
ATTN_MIN_ROW_SUM = 2.0 ** -20
ATTN_MAX_ROW_SUM = 2.0 ** 100
FF_CHUNK_DENSE = 2816
GMLP_ROW_PARTS = 2
ROW_PARTS = 2
FF_CHUNK_EXPERT = 1792
FF_SUB = 256
EXPERT_TILE = 1024
ROUTE_TILE = 512


def _cparams(sem):
    return pltpu.CompilerParams(dimension_semantics=sem, vmem_limit_bytes=VMEM_LIMIT_BYTES)


def _const_spec(shape):
    n = len(shape)
    return pl.BlockSpec(shape, lambda *_: (0,) * n, pipeline_mode=pl.Buffered(1))


def _rms(x, g):
    return x * lax.rsqrt(jnp.mean(x * x, axis=-1, keepdims=True) + EPS) * g


def _pre(x, g, shift, scale):
    return _rms(x, g) * (1.0 + scale) + shift


def _bdot(a, b):
    return jnp.dot(a, b, preferred_element_type=F32)


def _mod_kernel(c_ref, w_ref, b_ref, o_ref):
    a = jax.nn.silu(c_ref[...])
    o_ref[...] = jnp.dot(a, w_ref[...], precision=lax.Precision.HIGHEST,
                         preferred_element_type=F32) + b_ref[...]


def _modulation(cond8, w_mod, b_mod):
    d = cond8.shape[1]
    out = pl.pallas_call(
        _mod_kernel,
        grid=(6,),
        in_specs=[pl.BlockSpec((8, d), lambda j: (0, 0)),
                  pl.BlockSpec((d, d), lambda j: (0, j)),
                  pl.BlockSpec((1, d), lambda j: (0, j))],
        out_specs=pl.BlockSpec((8, d), lambda j: (0, j)),
        out_shape=jax.ShapeDtypeStruct((8, 6 * d), F32),
        compiler_params=_cparams(("arbitrary",)),
        name="modulation",
    )(cond8, w_mod, b_mod.reshape(1, 6 * d))
    return out.reshape(8, 6, d)


def _rope128(t, cos, sa, sb):
    return t * cos + pltpu.roll(t, 96, 1) * sa + pltpu.roll(t, 32, 1) * sb


def _inproj_kernel(x_ref, mod_ref, gpre_ref, w_ref, gq_ref, gk_ref, eq_ref, ek_ref,
                   cos_ref, sa_ref, sb_ref, qt_ref, k_ref, vt_ref, bg_ref, z_ref, *, q_scale):
    mod = mod_ref[...]
    h = _pre(x_ref[...], gpre_ref[...], mod[0:1], mod[1:2]).astype(BF16)
    cos, sa, sb = cos_ref[...], sa_ref[...], sb_ref[...]

    q = _bdot(h, w_ref[:, 0:ATTN_W])
    ms = _bdot((q * q).astype(BF16), eq_ref[...])
    qn = q * lax.rsqrt(ms + EPS) * gq_ref[...]
    qr = jnp.concatenate(
        [_rope128(qn[:, LANES * j:LANES * (j + 1)], cos, sa, sb) for j in range(ATTN_W // LANES)], axis=1)
    qt_ref[...] = (qr * q_scale).T.astype(BF16)

    kv = _bdot(h, w_ref[:, ATTN_W:ATTN_W + 2 * KV_W])
    k = kv[:, 0:KV_W]
    msk = _bdot((k * k).astype(BF16), ek_ref[...])
    kn = k * lax.rsqrt(msk + EPS) * gk_ref[...]
    k_ref[...] = _rope128(kn, cos, sa, sb).astype(BF16)
    vt_ref[...] = kv[:, KV_W:2 * KV_W].T.astype(BF16)

    o = ATTN_W + 2 * KV_W
    cw = (w_ref.shape[1] - o) // 3
    bg_ref[...] = _bdot(h, w_ref[:, o:o + cw]).astype(BF16)
    cg = _bdot(h, w_ref[:, o + cw:o + 2 * cw])
    hv = _bdot(h, w_ref[:, o + 2 * cw:o + 3 * cw])
    z_ref[...] = (cg * hv).astype(BF16)


def _inproj(x, mods, mod_row, g_pre, w_in, gq, gk, eq, ek, cos, sa, sb, tm):
    b, s, d = x.shape
    nt = s // tm
    cw = (w_in.shape[1] - ATTN_W - 2 * KV_W) // 3
    row = (lambda bi: bi) if mod_row is None else (lambda bi: mod_row)
    tok = lambda bi, i: (bi, i, 0)
    return pl.pallas_call(
        functools.partial(_inproj_kernel, q_scale=HEAD_DIM ** -0.5 * LOG2E),
        grid=(b, nt),
        in_specs=[pl.BlockSpec((None, tm, d), tok),
                  pl.BlockSpec((None, 6, d), lambda bi, i: (row(bi), 0, 0)),
                  _const_spec((1, d)),
                  _const_spec(w_in.shape),
                  _const_spec((1, ATTN_W)),
                  _const_spec((1, KV_W)),
                  _const_spec((ATTN_W, ATTN_W)),
                  _const_spec((KV_W, KV_W)),
                  pl.BlockSpec((tm, LANES), lambda bi, i: (i, 0)),
                  pl.BlockSpec((tm, LANES), lambda bi, i: (i, 0)),
                  pl.BlockSpec((tm, LANES), lambda bi, i: (i, 0))],
        out_specs=[pl.BlockSpec((None, ATTN_W, tm), lambda bi, i: (bi, 0, i)),
                   pl.BlockSpec((None, tm, KV_W), tok),
                   pl.BlockSpec((None, None, KV_W, tm), lambda bi, i: (bi, i, 0, 0)),
                   pl.BlockSpec((None, tm, cw), tok),
                   pl.BlockSpec((None, tm, cw), tok)],
        out_shape=[jax.ShapeDtypeStruct((b, ATTN_W, s), BF16),
                   jax.ShapeDtypeStruct((b, s, KV_W), BF16),
                   jax.ShapeDtypeStruct((b, nt, KV_W, tm), BF16),
                   jax.ShapeDtypeStruct((b, s, cw), BF16),
                   jax.ShapeDtypeStruct((b, s, cw), BF16)],
        compiler_params=_cparams(("parallel", "parallel")),
        name="inproj",
    )(x, mods, g_pre, w_in, gq, gk, eq, ek, cos, sa, sb)


def _attn_kernel(bound_ref, qt_ref, k_ref, vt_ref, kc_ref, vct_ref, o_ref, qp_ref, m_ref, acc_ref, ot_ref,
                 s_ref):
    n_kb = k_ref.shape[0]
    group = N_HEADS // N_KV_HEADS

    for h in range(N_HEADS):
        qh = qt_ref[HEAD_DIM * h:HEAD_DIM * (h + 1), :]
        zq = jnp.zeros_like(qh)
        qp_ref[h] = jnp.concatenate([qh, zq] if h // group == 0 else [zq, qh], axis=0)

    look = ATTN_LOOKAHEAD

    def values_aug(vtb, h):
        g = h // group
        return jnp.concatenate([vtb[HEAD_DIM * g:HEAD_DIM * (g + 1), :], jnp.ones((16, vtb.shape[1]), BF16)],
                               axis=0)

    def attend_exact(s, vtb, h, first):
        mb = jnp.max(s, axis=0, keepdims=True)
        m_old = None if first else m_ref[h:h + 1, :]
        m_new = mb if first else jnp.maximum(m_old, mb)
        p = jnp.exp2(s - m_new).astype(BF16)
        pv = _bdot(values_aug(vtb, h), p)
        acc_ref[h] = pv if first else acc_ref[h] * jnp.exp2(m_old - m_new) + pv
        m_ref[h:h + 1, :] = m_new

    def attend_stream(s, vtb, h, first):
        del first
        m_cur = m_ref[h:h + 1, :]
        p = jnp.exp2(s - m_cur).astype(BF16)
        pv = _bdot(values_aug(vtb, h), p)
        m_new = jnp.maximum(m_cur, jnp.max(s, axis=0, keepdims=True))
        acc_ref[h] = (acc_ref[h] + pv) * jnp.exp2(m_cur - m_new)
        m_ref[h:h + 1, :] = m_new

    def run(pending, blocks, kb_after, attend, first):
        n_steps = len(blocks) * N_HEADS
        for n in range(n_steps):
            s = pending.pop(0)
            ahead = n + look
            kb = blocks[ahead // N_HEADS][0] if ahead < n_steps else kb_after
            pending.append(_bdot(kb, qp_ref[ahead % N_HEADS]))
            attend(s, blocks[n // N_HEADS][1], n % N_HEADS, first)
        return pending

    def all_keys(attend, floor):
        kc = kc_ref[0]
        pending = run([_bdot(kc, qp_ref[h]) for h in range(look)], [(kc, vct_ref[0])], k_ref[0], attend_exact,
                      True)
        for l in range(look):
            s_ref[l] = pending[l]
        if floor is not None:
            for h in range(N_HEADS):
                m_cur = m_ref[h:h + 1, :]
                m_new = jnp.maximum(m_cur, floor)
                acc_ref[h] = acc_ref[h] * jnp.exp2(m_cur - m_new)
                m_ref[h:h + 1, :] = m_new

        per_iter = math.gcd(ATTN_BLOCKS_PER_ITER, n_kb)

        def body(i, carry):
            first_blk = i * per_iter
            blocks = [(k_ref[first_blk + j], vt_ref[first_blk + j]) for j in range(per_iter)]
            kb_after = k_ref[jnp.minimum(first_blk + per_iter, n_kb - 1)]
            pending = run([s_ref[l] for l in range(look)], blocks, kb_after, attend, False)
            for l in range(look):
                s_ref[l] = pending[l]
            return carry

        lax.fori_loop(0, n_kb // per_iter, body, 0)

    all_keys(attend_stream, bound_ref[0] - ATTN_HEADROOM)
    sums = jnp.concatenate([acc_ref[h, HEAD_DIM:HEAD_DIM + 1, :] for h in range(N_HEADS)], axis=0)
    sound = jnp.logical_and(sums >= ATTN_MIN_ROW_SUM, sums <= ATTN_MAX_ROW_SUM)
    n_unsound = jnp.sum(jnp.where(sound, 0.0, 1.0))

    @pl.when(n_unsound > 0.0)
    def _():
        all_keys(attend_exact, None)

    for h in range(N_HEADS):
        acc = acc_ref[h]
        ot_ref[HEAD_DIM * h:HEAD_DIM * (h + 1), :] = acc[0:HEAD_DIM] / acc[HEAD_DIM:HEAD_DIM + 1]
    o_ref[...] = ot_ref[...].T.astype(BF16)


def _attention(score_bound, qt, k4, vt4, kc4, vct4, tq):
    b, _, s = qt.shape
    _, n_kb, tk, _ = k4.shape
    ctx = kc4.shape[2]
    return pl.pallas_call(
        _attn_kernel,
        grid=(b, s // tq),
        in_specs=[pl.BlockSpec(memory_space=pltpu.SMEM),
                  pl.BlockSpec((None, ATTN_W, tq), lambda bi, i: (bi, 0, i)),
                  pl.BlockSpec((None, n_kb, tk, KV_W), lambda bi, i: (bi, 0, 0, 0)),
                  pl.BlockSpec((None, n_kb, KV_W, tk), lambda bi, i: (bi, 0, 0, 0)),
                  pl.BlockSpec((None, 1, ctx, KV_W), lambda bi, i: (bi, 0, 0, 0)),
                  pl.BlockSpec((None, 1, KV_W, ctx), lambda bi, i: (bi, 0, 0, 0))],
        out_specs=pl.BlockSpec((None, tq, ATTN_W), lambda bi, i: (bi, i, 0)),
        out_shape=jax.ShapeDtypeStruct((b, s, ATTN_W), BF16),
        scratch_shapes=[pltpu.VMEM((N_HEADS, KV_W, tq), BF16),
                        pltpu.VMEM((N_HEADS, tq), F32),
                        pltpu.VMEM((N_HEADS, HEAD_DIM + 16, tq), F32),
                        pltpu.VMEM((ATTN_W, tq), F32),
                        pltpu.VMEM((ATTN_LOOKAHEAD, tk, tq), F32)],
        compiler_params=_cparams(("parallel", "parallel")),
        name="attention",
    )(score_bound, qt, k4, vt4, kc4, vct4)


def _outproj_kernel(x_ref, mod_ref, attn_ref, bg_ref, z_ref, zp_ref, zn_ref, wc_ref, wo_ref, gpost_ref,
                    o_ref):
    i = pl.program_id(1)
    nt = pl.num_programs(1)
    tm = z_ref.shape[0]
    z = z_ref[...].astype(F32)
    halo = zp_ref.shape[0]
    zprev = zp_ref[...].astype(F32)[halo - 1:halo, :] * (i > 0).astype(F32)
    znext = zn_ref[...].astype(F32)[0:1, :] * (i < nt - 1).astype(F32)
    row = lax.broadcasted_iota(jnp.int32, z.shape, 0)
    zm1 = jnp.where(row == 0, zprev, pltpu.roll(z, 1, 0))
    zp1 = jnp.where(row == tm - 1, znext, pltpu.roll(z, tm - 1, 0))
    wc = wc_ref[...]
    taps = wc[0:1] * zm1 + wc[1:2] * z + wc[2:3] * zp1
    mod = mod_ref[...]
    n_parts = ROW_PARTS if tm % (ROW_PARTS * 16) == 0 else 1
    rows = tm // n_parts
    ys = []
    for p in range(n_parts):
        sl = slice(rows * p, rows * (p + 1))
        conv = (bg_ref[sl, :].astype(F32) * taps[sl, :]).astype(BF16)
        ys.append(_bdot(attn_ref[sl, :], wo_ref[0:ATTN_W, :]) + _bdot(conv, wo_ref[ATTN_W:, :]))
    for p in range(n_parts):
        sl = slice(rows * p, rows * (p + 1))
        o_ref[sl, :] = x_ref[sl, :] + mod[2:3] * _rms(ys[p], gpost_ref[...])


def _outproj(x, mods, attn, bg, z, w_conv, w_out, g_post, tm):
    b, s, d = x.shape
    cw = z.shape[2]
    halo = 16
    r = tm // halo
    last = s // halo - 1
    tok = lambda bi, i: (bi, i, 0)
    return pl.pallas_call(
        _outproj_kernel,
        grid=(b, s // tm),
        in_specs=[pl.BlockSpec((None, tm, d), tok),
                  pl.BlockSpec((None, 6, d), lambda bi, i: (bi, 0, 0)),
                  pl.BlockSpec((None, tm, ATTN_W), tok),
                  pl.BlockSpec((None, tm, cw), tok),
                  pl.BlockSpec((None, tm, cw), tok),
                  pl.BlockSpec((None, halo, cw), lambda bi, i: (bi, jnp.maximum(i * r - 1, 0), 0)),
                  pl.BlockSpec((None, halo, cw), lambda bi, i: (bi, jnp.minimum((i + 1) * r, last), 0)),
                  _const_spec(w_conv.shape),
                  _const_spec(w_out.shape),
                  _const_spec((1, d))],
        out_specs=pl.BlockSpec((None, tm, d), tok),
        out_shape=jax.ShapeDtypeStruct((b, s, d), F32),
        compiler_params=_cparams(("parallel", "parallel")),
        name="outproj",
    )(x, mods, attn, bg, z, z, z, w_conv, w_out, g_post)


def _swiglu_kernel(x_ref, mod_ref, gpre_ref, wg_ref, wu_ref, wd_ref, gpost_ref, o_ref, *, chunk):
    mod = mod_ref[...]
    tm = x_ref.shape[0]
    n_parts = ROW_PARTS if tm % (ROW_PARTS * 16) == 0 else 1
    rows = tm // n_parts
    xs = [x_ref[rows * p:rows * (p + 1), :] for p in range(n_parts)]
    hs = [_pre(xs[p], gpre_ref[...], mod[3:4], mod[4:5]).astype(BF16) for p in range(n_parts)]
    accs = [None] * n_parts
    for c in range(wg_ref.shape[1] // chunk):
        sl = slice(c * chunk, (c + 1) * chunk)
        for p in range(n_parts):
            a = (jax.nn.silu(_bdot(hs[p], wg_ref[:, sl])) * _bdot(hs[p], wu_ref[:, sl])).astype(BF16)
            part = _bdot(a, wd_ref[sl, :])
            accs[p] = part if accs[p] is None else accs[p] + part
    for p in range(n_parts):
        o_ref[rows * p:rows * (p + 1), :] = xs[p] + mod[5:6] * _rms(accs[p], gpost_ref[...])


def _swiglu(x, mods, g_pre, w_gate, w_up, w_down, g_post, tm):
    b, s, d = x.shape
    tok = lambda bi, i: (bi, i, 0)
    return pl.pallas_call(
        functools.partial(_swiglu_kernel, chunk=min(FF_CHUNK_DENSE, w_gate.shape[1])),
        grid=(b, s // tm),
        in_specs=[pl.BlockSpec((None, tm, d), tok),
                  pl.BlockSpec((None, 6, d), lambda bi, i: (bi, 0, 0)),
                  _const_spec((1, d)),
                  _const_spec(w_gate.shape),
                  _const_spec(w_up.shape),
                  _const_spec(w_down.shape),
                  _const_spec((1, d))],
        out_specs=pl.BlockSpec((None, tm, d), tok),
        out_shape=jax.ShapeDtypeStruct((b, s, d), F32),
        compiler_params=_cparams(("parallel", "parallel")),
        name="swiglu",
    )(x, mods, g_pre, w_gate, w_up, w_down, g_post)


def _gmlp_kernel(x_ref, mod_ref, gpre_ref, win_ref, gv_ref, bv_ref, ws_ref, bs_ref, wout_ref, gpost_ref,
                 gffn_ref, wr_ref, br_ref, tri_ref,
                 x3_ref, h3_ref, route_ref, cnt_ref, run_ref):
    first = jnp.logical_and(pl.program_id(0) == 0, pl.program_id(1) == 0)

    @pl.when(first)
    def _():
        run_ref[...] = jnp.zeros_like(run_ref)

    mod = mod_ref[...]
    tm, d = x_ref.shape
    gc = d // GM_GROUPS
    n_parts = GMLP_ROW_PARTS if tm % (GMLP_ROW_PARTS * CHUNK) == 0 else 1
    rows = tm // n_parts
    n_chunks = rows // CHUNK
    parts = range(n_parts)
    xs = [x_ref[rows * i:rows * (i + 1), :] for i in parts]
    hs = [_pre(xs[i], gpre_ref[...], mod[0:1], mod[1:2]).astype(BF16) for i in parts]
    us, vns = [], []
    for i in parts:
        us.append(jax.nn.gelu(_bdot(hs[i], win_ref[:, 0:d])))
        v = jax.nn.gelu(_bdot(hs[i], win_ref[:, d:2 * d]))
        vc = v - jnp.mean(v, axis=-1, keepdims=True)
        var = jnp.mean(vc * vc, axis=-1, keepdims=True)
        vns.append((vc * lax.rsqrt(var + EPS) * gv_ref[...] + bv_ref[...]).astype(BF16))
    gated = []
    for i in parts:
        mixed = []
        for g in range(GM_GROUPS):
            rhs = jnp.concatenate(
                [vns[i][CHUNK * c:CHUNK * (c + 1), gc * g:gc * (g + 1)] for c in range(n_chunks)],
                axis=1)
            mixed.append(_bdot(ws_ref[g], rhs) + jnp.concatenate([bs_ref[g]] * n_chunks, axis=1))
        s = jnp.concatenate(
            [jnp.concatenate([mixed[g][:, gc * c:gc * (c + 1)] for g in range(GM_GROUPS)], axis=1)
             for c in range(n_chunks)], axis=0)
        gated.append((us[i] * s).astype(BF16))
    h3s = []
    for i in parts:
        y = _bdot(gated[i], wout_ref[...])
        x3 = xs[i] + mod[2:3] * _rms(y, gpost_ref[...])
        x3_ref[rows * i:rows * (i + 1), :] = x3
        h3 = _pre(x3, gffn_ref[...], mod[3:4], mod[4:5])
        h3_ref[rows * i:rows * (i + 1), :] = h3
        h3s.append(h3)
    logit_parts = []
    for i in parts:
        h_hi = h3s[i].astype(BF16)
        h_lo = (h3s[i] - h_hi.astype(F32)).astype(BF16)
        prod = _bdot(jnp.concatenate([h_hi, h_lo], axis=0), wr_ref[...])
        logit_parts.append(prod[0:rows, 0:LANES] + prod[0:rows, LANES:2 * LANES] + prod[rows:2 * rows, 0:LANES])
    logits = jnp.concatenate(logit_parts, axis=0) + br_ref[...]
    lane = lax.broadcasted_iota(jnp.int32, logits.shape, 1)
    m1 = jnp.max(logits, axis=1, keepdims=True)
    i1 = jnp.min(jnp.where(logits == m1, lane, LANES), axis=1, keepdims=True)
    rest = jnp.where(lane == i1, 2.0 * NEG_BIG, logits)
    m2 = jnp.max(rest, axis=1, keepdims=True)
    i2 = jnp.min(jnp.where(rest == m2, lane, LANES), axis=1, keepdims=True)
    e21 = jnp.exp(m2 - m1)
    w1 = 1.0 / (1.0 + e21)
    w2 = e21 / (1.0 + e21)
    hot1 = lane == i1
    hot2 = lane == i2
    onehot = jnp.where(jnp.logical_or(hot1, hot2), 1.0, 0.0)
    before = _bdot(tri_ref[...], onehot.astype(BF16)) + run_ref[...]
    r1 = jnp.sum(jnp.where(hot1, before, 0.0), axis=1, keepdims=True)
    r2 = jnp.sum(jnp.where(hot2, before, 0.0), axis=1, keepdims=True)
    run = run_ref[...] + jnp.sum(onehot, axis=0, keepdims=True)
    run_ref[...] = run
    cnt_ref[...] = jnp.broadcast_to(run, cnt_ref.shape)
    fields = (i1.astype(F32), i2.astype(F32), r1, r2, w1, w2)
    route = jnp.zeros(logits.shape, F32)
    for j, f in enumerate(fields):
        route = jnp.where(lane == j, f, route)
    route_ref[...] = route


def _gmlp_router(x, mods, g_pre, w_in, g_v, b_v, w_s, b_s_b, w_out, g_post, g_ffn, w_r, b_r, tm):
    b, s, d = x.shape
    tok = lambda bi, i: (bi, i, 0)
    tri = jnp.tril(jnp.ones((tm, tm), F32), -1).astype(BF16)
    return pl.pallas_call(
        _gmlp_kernel,
        grid=(b, s // tm),
        in_specs=[pl.BlockSpec((None, tm, d), tok),
                  pl.BlockSpec((None, 6, d), lambda bi, i: (bi, 0, 0)),
                  _const_spec((1, d)),
                  _const_spec(w_in.shape),
                  _const_spec((1, d)),
                  _const_spec((1, d)),
                  _const_spec(w_s.shape),
                  _const_spec(b_s_b.shape),
                  _const_spec(w_out.shape),
                  _const_spec((1, d)),
                  _const_spec((1, d)),
                  _const_spec(w_r.shape),
                  _const_spec((1, LANES)),
                  _const_spec((tm, tm))],
        out_specs=[pl.BlockSpec((None, tm, d), tok),
                   pl.BlockSpec((None, tm, d), tok),
                   pl.BlockSpec((None, tm, LANES), tok),
                   pl.BlockSpec((8, LANES), lambda bi, i: (0, 0))],
        out_shape=[jax.ShapeDtypeStruct((b, s, d), F32),
                   jax.ShapeDtypeStruct((b, s, d), F32),
                   jax.ShapeDtypeStruct((b, s, LANES), F32),
                   jax.ShapeDtypeStruct((8, LANES), F32)],
        scratch_shapes=[pltpu.VMEM((1, LANES), F32)],
        compiler_params=_cparams(("arbitrary", "arbitrary")),
        name="gmlp_router",
    )(x, mods, g_pre, w_in, g_v, b_v, w_s, b_s_b, w_out, g_post, g_ffn, w_r, b_r, tri)


def _dispatch_kernel(slot_ref, h_ref, xs_in_ref, xs_ref, sem):
    del xs_in_ref
    tm = h_ref.shape[0]
    for r in range(tm):
        for k in range(2):
            dst = slot_ref[0, 0, 2 * r + k]
            pltpu.make_async_copy(h_ref.at[pl.ds(r, 1), :], xs_ref.at[pl.ds(dst, 1), :], sem).start(priority=k)
    for _ in range(2):
        pltpu.make_async_copy(h_ref, xs_ref.at[pl.ds(0, tm), :], sem).wait()


def _dispatch(h, slots, n_slots, tm):
    n, d = h.shape
    nt = n // tm
    return pl.pallas_call(
        _dispatch_kernel,
        grid=(nt,),
        in_specs=[pl.BlockSpec((1, 1, 2 * tm), lambda i: (i, 0, 0), memory_space=pltpu.SMEM),
                  pl.BlockSpec((tm, d), lambda i: (i, 0)),
                  pl.BlockSpec(memory_space=pl.ANY)],
        out_specs=pl.BlockSpec(memory_space=pl.ANY),
        out_shape=jax.ShapeDtypeStruct((n_slots, d), F32),
        scratch_shapes=[pltpu.SemaphoreType.DMA(())],
        input_output_aliases={2: 0},
        compiler_params=_cparams(("arbitrary",)),
        name="dispatch",
    )(slots.reshape(nt, 1, 2 * tm), h, jnp.zeros((n_slots, d), F32))


def _expert_kernel(te_ref, na_ref, xs_ref, wgu_ref, wd_ref, ys_ref, xb_ref):
    t = pl.program_id(0)
    c = pl.program_id(1)
    active = t < na_ref[0]

    @pl.when(jnp.logical_and(active, c == 0))
    def _():
        xb_ref[...] = xs_ref[...].astype(BF16)

    @pl.when(active)
    def _():
        xb = xb_ref[...]
        acts = []
        for j in range(wd_ref.shape[0] // FF_SUB):
            gu = _bdot(xb, wgu_ref[:, 2 * FF_SUB * j:2 * FF_SUB * (j + 1)])
            acts.append((jax.nn.silu(gu[:, 0:FF_SUB]) * gu[:, FF_SUB:2 * FF_SUB]).astype(BF16))
        part = _bdot(jnp.concatenate(acts, axis=1), wd_ref[...])

        @pl.when(c == 0)
        def _():
            ys_ref[...] = part

        @pl.when(c > 0)
        def _():
            ys_ref[...] += part

    @pl.when(jnp.logical_and(jnp.logical_not(active), c == 0))
    def _():
        ys_ref[...] = jnp.zeros_like(ys_ref)


def _gate_up_kernel(wg_ref, wu_ref, o_ref):
    for j in range(wg_ref.shape[1] // FF_SUB):
        src = slice(FF_SUB * j, FF_SUB * (j + 1))
        o_ref[:, 2 * FF_SUB * j:2 * FF_SUB * j + FF_SUB] = wg_ref[:, src].astype(BF16)
        o_ref[:, 2 * FF_SUB * j + FF_SUB:2 * FF_SUB * (j + 1)] = wu_ref[:, src].astype(BF16)


def _interleave_gate_up(w_gate, w_up):
    ne, d, ff = w_gate.shape
    cols = 2 * FF_SUB
    spec = pl.BlockSpec((None, d, cols), lambda e, j: (e, 0, j))
    return pl.pallas_call(
        _gate_up_kernel,
        grid=(ne, ff // cols),
        in_specs=[spec, spec],
        out_specs=pl.BlockSpec((None, d, 2 * cols), lambda e, j: (e, 0, j)),
        out_shape=jax.ShapeDtypeStruct((ne, d, 2 * ff), BF16),
        compiler_params=_cparams(("parallel", "parallel")),
        name="gate_up_weights",
    )(w_gate, w_up)


def _experts(tile_expert, n_active, xs, w_gu, w_down, tm, chunk):
    n_slots, d = xs.shape
    ff = w_down.shape[1]
    n_tiles = n_slots // tm
    last = ff // chunk - 1

    def live(t, na):
        return jnp.minimum(t, na[0] - 1)

    def chunk_of(t, c, na):
        return jnp.where(t < na[0], c, last)

    grid_spec = pltpu.PrefetchScalarGridSpec(
        num_scalar_prefetch=2,
        grid=(n_tiles, ff // chunk),
        in_specs=[pl.BlockSpec((tm, d), lambda t, c, te, na: (live(t, na), 0)),
                  pl.BlockSpec((None, d, 2 * chunk), lambda t, c, te, na: (te[live(t, na)], 0, chunk_of(t, c, na))),
                  pl.BlockSpec((None, chunk, d), lambda t, c, te, na: (te[live(t, na)], chunk_of(t, c, na), 0))],
        out_specs=pl.BlockSpec((tm, d), lambda t, c, te, na: (t, 0)),
        scratch_shapes=[pltpu.VMEM((tm, d), BF16)],
    )
    return pl.pallas_call(
        _expert_kernel,
        grid_spec=grid_spec,
        out_shape=jax.ShapeDtypeStruct((n_slots, d), F32),
        compiler_params=_cparams(("arbitrary", "arbitrary")),
        name="experts",
    )(tile_expert, n_active, xs, w_gu, w_down)


def _combine_kernel(slot_ref, x_ref, mod_ref, route_ref, gpost_ref, ys_ref, o_ref, buf_ref, sem, *, nt):
    i = pl.program_id(0)
    tm = x_ref.shape[0]

    def issue(b):
        for r in range(tm):
            for k in range(2):
                src = slot_ref[0, 0, 2 * r + k]
                pltpu.make_async_copy(ys_ref.at[pl.ds(src, 1), :], buf_ref.at[b, k, pl.ds(r, 1), :],
                                      sem.at[b]).start(priority=k)

    def drain(b):
        for k in range(2):
            pltpu.make_async_copy(ys_ref.at[pl.ds(0, tm), :], buf_ref.at[b, k], sem.at[b]).wait()

    def finish(b):
        route = route_ref[...]
        f = route[:, 4:5] * buf_ref[b, 0] + route[:, 5:6] * buf_ref[b, 1]
        mod = mod_ref[...]
        o_ref[...] = x_ref[...] + mod[5:6] * _rms(f, gpost_ref[...])

    @pl.when(i == 0)
    def _():
        issue(0)

    for parity in range(2):
        @pl.when(jnp.logical_and(jnp.logical_and(i > 0, i < nt), i % 2 == parity))
        def _(parity=parity):
            drain(1 - parity)
            issue(parity)
            finish(1 - parity)

    @pl.when(i == nt)
    def _():
        drain((nt - 1) % 2)
        finish((nt - 1) % 2)


def _combine(x, mods, route, g_post, ys, slots, seq, tm):
    n, d = x.shape
    nt = n // tm
    per_batch = seq // tm
    done = lambda i: jnp.maximum(i - 1, 0)
    return pl.pallas_call(
        functools.partial(_combine_kernel, nt=nt),
        grid=(nt + 1,),
        in_specs=[pl.BlockSpec((1, 1, 2 * tm), lambda i: (jnp.minimum(i, nt - 1), 0, 0), memory_space=pltpu.SMEM),
                  pl.BlockSpec((tm, d), lambda i: (done(i), 0)),
                  pl.BlockSpec((None, 6, d), lambda i: (done(i) // per_batch, 0, 0)),
                  pl.BlockSpec((tm, LANES), lambda i: (done(i), 0)),
                  _const_spec((1, d)),
                  pl.BlockSpec(memory_space=pl.ANY)],
        out_specs=pl.BlockSpec((tm, d), lambda i: (done(i), 0)),
        out_shape=jax.ShapeDtypeStruct((n, d), F32),
        scratch_shapes=[pltpu.VMEM((2, 2, tm, d), F32), pltpu.SemaphoreType.DMA((2,))],
        compiler_params=_cparams(("arbitrary",)),
        name="combine",
    )(slots.reshape(nt, 1, 2 * tm), x, mods, route, g_post, ys)


def _rope_tables(n):
    axis_dim = HEAD_DIM // 2
    pos = jnp.arange(n, dtype=jnp.int32)
    r = (pos // GRID_W).astype(F32)[:, None]
    col = (pos % GRID_W).astype(F32)[:, None]
    inv = 1.0 / (ROPE_THETA ** (jnp.arange(0, axis_dim, 2, dtype=F32) / axis_dim))
    ang = jnp.concatenate([r * inv, col * inv], axis=-1)
    cos, sin = jnp.cos(ang), jnp.sin(ang)
    zero = jnp.zeros_like(sin)
    reps = LANES // HEAD_DIM
    return (jnp.tile(jnp.concatenate([cos, cos], -1), (1, reps)),
            jnp.tile(jnp.concatenate([-sin, zero], -1), (1, reps)),
            jnp.tile(jnp.concatenate([zero, sin], -1), (1, reps)))


def _head_mean_matrix(width):
    idx = jnp.arange(width) // HEAD_DIM
    return jnp.where(idx[:, None] == idx[None, :], 1.0 / HEAD_DIM, 0.0).astype(BF16)


def kernel(x, c, ctx, c_ctx, e_w_mod, e_b_mod, e_g_pre_mix, e_g_post_mix, e_w_in, e_g_q, e_g_k, e_w_conv, e_w_out, e_g_pre_ffn, e_g_post_ffn, e_w_gate, e_w_up, e_w_down, o_w_mod, o_b_mod, o_g_pre_mix, o_g_post_mix, o_w_in, o_g_v, o_b_v, o_w_s, o_b_s, o_w_out, o_g_pre_ffn, o_g_post_ffn, o_w_router, o_b_router, o_w_gate, o_w_up, o_w_down):
    b, s, d = x.shape
    n_ctx = ctx.shape[1]
    n = b * s
    tm = min(TOKEN_TILE, s)
    tq = min(ATTN_Q_TILE, s)
    assert b + 1 <= 8 and s % tm == 0 and s % tq == 0 and tm % CHUNK == 0 and n_ctx % 16 == 0
    assert FF_CHUNK_EXPERT % FF_SUB == 0
    assert e_w_mod.shape[0] == 1 and o_w_mod.shape[0] == 1
    row = lambda g: g.reshape(1, -1)

    cond8 = jnp.zeros((8, d), F32).at[:b].set(c).at[b].set(c_ctx)
    mods_e = _modulation(cond8, e_w_mod[0], e_b_mod[0])
    mods_o = _modulation(cond8, o_w_mod[0], o_b_mod[0])

    w_in = e_w_in[0].astype(BF16)
    gq = jnp.tile(e_g_q[0], N_HEADS).reshape(1, ATTN_W)
    gk = jnp.tile(e_g_k[0], N_KV_HEADS).reshape(1, KV_W)
    eq, ek = _head_mean_matrix(ATTN_W), _head_mean_matrix(KV_W)
    cos, sa, sb = _rope_tables(s)
    qt, k, vt4, bg, z = _inproj(x, mods_e, None, row(e_g_pre_mix[0]), w_in, gq, gk, eq, ek, cos, sa, sb, tm)
    ones = jnp.ones((n_ctx, LANES), F32)
    zeros = jnp.zeros((n_ctx, LANES), F32)
    _, kc, vct4, _, _ = _inproj(ctx, mods_e, b, row(e_g_pre_mix[0]), w_in, gq, gk, eq, ek, ones, zeros, zeros,
                                n_ctx)
    score_bound = (ATTN_BOUND_MARGIN * HEAD_DIM ** 0.5 * LOG2E * jnp.max(jnp.abs(e_g_q[0]))
                   * jnp.max(jnp.abs(e_g_k[0]))).reshape(1).astype(F32)
    attn = _attention(score_bound, qt, k.reshape(b, s // tm, tm, KV_W), vt4, kc.reshape(b, 1, n_ctx, KV_W), vct4, tq)
    x1 = _outproj(x, mods_e, attn, bg, z, e_w_conv[0], e_w_out[0].astype(BF16), row(e_g_post_mix[0]), tm)

    x2 = _swiglu(x1, mods_e, row(e_g_pre_ffn[0]), e_w_gate[0].astype(BF16), e_w_up[0].astype(BF16),
                 e_w_down[0].astype(BF16), row(e_g_post_ffn[0]), tm)

    b_s_b = jnp.broadcast_to(o_b_s[0][:, :, None], (GM_GROUPS, CHUNK, d // GM_GROUPS))
    w_r32 = jnp.zeros((d, LANES), F32).at[:, :N_EXPERTS].set(o_w_router[0])
    w_r_hi = w_r32.astype(BF16)
    w_r = jnp.concatenate([w_r_hi, (w_r32 - w_r_hi.astype(F32)).astype(BF16)], axis=1)
    b_r = jnp.full((1, LANES), NEG_BIG, F32).at[0, :N_EXPERTS].set(o_b_router[0])
    x3, h3, route, counts = _gmlp_router(
        x2, mods_o, row(o_g_pre_mix[0]), o_w_in[0].astype(BF16), row(o_g_v[0]), row(o_b_v[0]),
        o_w_s[0].astype(BF16), b_s_b, o_w_out[0].astype(BF16), row(o_g_post_mix[0]), row(o_g_pre_ffn[0]),
        w_r, b_r, tm)

    te_rows = EXPERT_TILE
    n_tiles = -(-(2 * n + N_EXPERTS * (te_rows - 1)) // te_rows)
    n_slots = n_tiles * te_rows
    cnt = counts[0, :N_EXPERTS].astype(jnp.int32)
    tiles_per = (cnt + te_rows - 1) // te_rows
    tile_end = jnp.cumsum(tiles_per)
    base = (tile_end - tiles_per) * te_rows
    route2 = route.reshape(n, LANES)
    e12 = route2[:, 0:2].astype(jnp.int32)
    r12 = route2[:, 2:4].astype(jnp.int32)
    slots = (base[e12] + r12).reshape(-1)
    tile_expert = jnp.minimum(
        jnp.sum(jnp.arange(n_tiles, dtype=jnp.int32)[:, None] >= tile_end[None, :], axis=1),
        N_EXPERTS - 1).astype(jnp.int32)
    n_active = tile_end[-1:].astype(jnp.int32)

    rt = min(ROUTE_TILE, s)
    xs = _dispatch(h3.reshape(n, d), slots, n_slots, rt)
    w_gu = _interleave_gate_up(o_w_gate[0], o_w_up[0])
    ys = _experts(tile_expert, n_active, xs, w_gu, o_w_down[0].astype(BF16), te_rows, FF_CHUNK_EXPERT)
    out = _combine(x3.reshape(n, d), mods_o, route2, row(o_g_post_ffn[0]), ys, slots, s, rt)
    return out.reshape(b, s, d)
```

```python
import functools
import math

import jax
import jax.numpy as jnp
from jax import lax
from jax.experimental import pallas as pl
from jax.experimental.pallas import tpu as pltpu

F32 = jnp.float32
BF16 = jnp.bfloat16

EPS = 1e-6
GRID_W = 64
N_HEADS = 8
N_KV_HEADS = 2
HEAD_DIM = 64
ATTN_W = N_HEADS * HEAD_DIM
KV_W = N_KV_HEADS * HEAD_DIM
ROPE_THETA = 10000.0
CHUNK = 128
GM_GROUPS = 8
N_EXPERTS = 8
LANES = 128
LOG2E = 1.4426950408889634
NEG_BIG = -1e30

VMEM_LIMIT_BYTES = 56 * 1024 * 1024

TOKEN_TILE = 512
ATTN_Q_TILE = 512
ATTN_BLOCKS_PER_ITER = 4
ATTN_LOOKAHEAD = 2
ATTN_HEADROOM = 40.0
ATTN_BOUND_MARGIN = 1.05
ATTN_MIN_ROW_SUM = 2.0 ** -20
ATTN_MAX_ROW_SUM = 2.0 ** 100
FF_CHUNK_DENSE = 2816
GMLP_ROW_PARTS = 2
ROW_PARTS = 2
FF_CHUNK_EXPERT = 1792
FF_SUB = 256
EXPERT_TILE = 1024
ROUTE_TILE = 512


def _cparams(sem):
    return pltpu.CompilerParams(dimension_semantics=sem, vmem_limit_bytes=VMEM_LIMIT_BYTES)


def _const_spec(shape):
    n = len(shape)
    return pl.BlockSpec(shape, lambda *_: (0,) * n, pipeline_mode=pl.Buffered(1))


def _rms(x, g):
    return x * lax.rsqrt(jnp.mean(x * x, axis=-1, keepdims=True) + EPS) * g


def _pre(x, g, shift, scale):
    return _rms(x, g) * (1.0 + scale) + shift


def _bdot(a, b):
    return jnp.dot(a, b, preferred_element_type=F32)


def _mod_kernel(c_ref, w_ref, b_ref, o_ref):
    a = jax.nn.silu(c_ref[...])
    o_ref[...] = jnp.dot(a, w_ref[...], precision=lax.Precision.HIGHEST,
                         preferred_element_type=F32) + b_ref[...]


def _modulation(cond8, w_mod, b_mod):
    d = cond8.shape[1]
    out = pl.pallas_call(
        _mod_kernel,
        grid=(6,),
        in_specs=[pl.BlockSpec((8, d), lambda j: (0, 0)),
                  pl.BlockSpec((d, d), lambda j: (0, j)),
                  pl.BlockSpec((1, d), lambda j: (0, j))],
        out_specs=pl.BlockSpec((8, d), lambda j: (0, j)),
        out_shape=jax.ShapeDtypeStruct((8, 6 * d), F32),
        compiler_params=_cparams(("arbitrary",)),
        name="modulation",
    )(cond8, w_mod, b_mod.reshape(1, 6 * d))
    return out.reshape(8, 6, d)


def _rope128(t, cos, sa, sb):
    return t * cos + pltpu.roll(t, 96, 1) * sa + pltpu.roll(t, 32, 1) * sb


def _inproj_kernel(x_ref, mod_ref, gpre_ref, w_ref, gq_ref, gk_ref, eq_ref, ek_ref,
                   cos_ref, sa_ref, sb_ref, qt_ref, k_ref, vt_ref, bg_ref, z_ref, *, q_scale):
    mod = mod_ref[...]
    h = _pre(x_ref[...], gpre_ref[...], mod[0:1], mod[1:2]).astype(BF16)
    cos, sa, sb = cos_ref[...], sa_ref[...], sb_ref[...]

    q = _bdot(h, w_ref[:, 0:ATTN_W])
    ms = _bdot((q * q).astype(BF16), eq_ref[...])
    qn = q * lax.rsqrt(ms + EPS) * gq_ref[...]
    qr = jnp.concatenate(
        [_rope128(qn[:, LANES * j:LANES * (j + 1)], cos, sa, sb) for j in range(ATTN_W // LANES)], axis=1)
    qt_ref[...] = (qr * q_scale).T.astype(BF16)

    kv = _bdot(h, w_ref[:, ATTN_W:ATTN_W + 2 * KV_W])
    k = kv[:, 0:KV_W]
    msk = _bdot((k * k).astype(BF16), ek_ref[...])
    kn = k * lax.rsqrt(msk + EPS) * gk_ref[...]
    k_ref[...] = _rope128(kn, cos, sa, sb).astype(BF16)
    vt_ref[...] = kv[:, KV_W:2 * KV_W].T.astype(BF16)

    o = ATTN_W + 2 * KV_W
    cw = (w_ref.shape[1] - o) // 3
    bg_ref[...] = _bdot(h, w_ref[:, o:o + cw]).astype(BF16)
    cg = _bdot(h, w_ref[:, o + cw:o + 2 * cw])
    hv = _bdot(h, w_ref[:, o + 2 * cw:o + 3 * cw])
    z_ref[...] = (cg * hv).astype(BF16)


def _inproj(x, mods, mod_row, g_pre, w_in, gq, gk, eq, ek, cos, sa, sb, tm):
    b, s, d = x.shape
    nt = s // tm
    cw = (w_in.shape[1] - ATTN_W - 2 * KV_W) // 3
    row = (lambda bi: bi) if mod_row is None else (lambda bi: mod_row)
    tok = lambda bi, i: (bi, i, 0)
    return pl.pallas_call(
        functools.partial(_inproj_kernel, q_scale=HEAD_DIM ** -0.5 * LOG2E),
        grid=(b, nt),
        in_specs=[pl.BlockSpec((None, tm, d), tok),
                  pl.BlockSpec((None, 6, d), lambda bi, i: (row(bi), 0, 0)),
                  _const_spec((1, d)),
                  _const_spec(w_in.shape),
                  _const_spec((1, ATTN_W)),
                  _const_spec((1, KV_W)),
                  _const_spec((ATTN_W, ATTN_W)),
                  _const_spec((KV_W, KV_W)),
                  pl.BlockSpec((tm, LANES), lambda bi, i: (i, 0)),
                  pl.BlockSpec((tm, LANES), lambda bi, i: (i, 0)),
                  pl.BlockSpec((tm, LANES), lambda bi, i: (i, 0))],
        out_specs=[pl.BlockSpec((None, ATTN_W, tm), lambda bi, i: (bi, 0, i)),
                   pl.BlockSpec((None, tm, KV_W), tok),
                   pl.BlockSpec((None, None, KV_W, tm), lambda bi, i: (bi, i, 0, 0)),
                   pl.BlockSpec((None, tm, cw), tok),
                   pl.BlockSpec((None, tm, cw), tok)],
        out_shape=[jax.ShapeDtypeStruct((b, ATTN_W, s), BF16),
                   jax.ShapeDtypeStruct((b, s, KV_W), BF16),
                   jax.ShapeDtypeStruct((b, nt, KV_W, tm), BF16),
                   jax.ShapeDtypeStruct((b, s, cw), BF16),
                   jax.ShapeDtypeStruct((b, s, cw), BF16)],
        compiler_params=_cparams(("parallel", "parallel")),
        name="inproj",
    )(x, mods, g_pre, w_in, gq, gk, eq, ek, cos, sa, sb)


def _attn_kernel(bound_ref, qt_ref, k_ref, vt_ref, kc_ref, vct_ref, o_ref, qp_ref, m_ref, acc_ref, ot_ref,
                 s_ref):
    n_kb = k_ref.shape[0]
    group = N_HEADS // N_KV_HEADS

    for h in range(N_HEADS):
        qh = qt_ref[HEAD_DIM * h:HEAD_DIM * (h + 1), :]
        zq = jnp.zeros_like(qh)
        qp_ref[h] = jnp.concatenate([qh, zq] if h // group == 0 else [zq, qh], axis=0)

    look = ATTN_LOOKAHEAD

    def values_aug(vtb, h):
        g = h // group
        return jnp.concatenate([vtb[HEAD_DIM * g:HEAD_DIM * (g + 1), :], jnp.ones((16, vtb.shape[1]), BF16)],
                               axis=0)

    def attend_exact(s, vtb, h, first):
        mb = jnp.max(s, axis=0, keepdims=True)
        m_old = None if first else m_ref[h:h + 1, :]
        m_new = mb if first else jnp.maximum(m_old, mb)
        p = jnp.exp2(s - m_new).astype(BF16)
        pv = _bdot(values_aug(vtb, h), p)
        acc_ref[h] = pv if first else acc_ref[h] * jnp.exp2(m_old - m_new) + pv
        m_ref[h:h + 1, :] = m_new

    def attend_stream(s, vtb, h, first):
        del first
        m_cur = m_ref[h:h + 1, :]
        p = jnp.exp2(s - m_cur).astype(BF16)
        pv = _bdot(values_aug(vtb, h), p)
        m_new = jnp.maximum(m_cur, jnp.max(s, axis=0, keepdims=True))
        acc_ref[h] = (acc_ref[h] + pv) * jnp.exp2(m_cur - m_new)
        m_ref[h:h + 1, :] = m_new

    def run(pending, blocks, kb_after, attend, first):
        n_steps = len(blocks) * N_HEADS
        for n in range(n_steps):
            s = pending.pop(0)
            ahead = n + look
            kb = blocks[ahead // N_HEADS][0] if ahead < n_steps else kb_after
            pending.append(_bdot(kb, qp_ref[ahead % N_HEADS]))
            attend(s, blocks[n // N_HEADS][1], n % N_HEADS, first)
        return pending

    def all_keys(attend, floor):
        kc = kc_ref[0]
        pending = run([_bdot(kc, qp_ref[h]) for h in range(look)], [(kc, vct_ref[0])], k_ref[0], attend_exact,
                      True)
        for l in range(look):
            s_ref[l] = pending[l]
        if floor is not None:
            for h in range(N_HEADS):
                m_cur = m_ref[h:h + 1, :]
                m_new = jnp.maximum(m_cur, floor)
                acc_ref[h] = acc_ref[h] * jnp.exp2(m_cur - m_new)
                m_ref[h:h + 1, :] = m_new

        per_iter = math.gcd(ATTN_BLOCKS_PER_ITER, n_kb)

        def body(i, carry):
            first_blk = i * per_iter
            blocks = [(k_ref[first_blk + j], vt_ref[first_blk + j]) for j in range(per_iter)]
            kb_after = k_ref[jnp.minimum(first_blk + per_iter, n_kb - 1)]
            pending = run([s_ref[l] for l in range(look)], blocks, kb_after, attend, False)
            for l in range(look):
                s_ref[l] = pending[l]
            return carry

        lax.fori_loop(0, n_kb // per_iter, body, 0)

    all_keys(attend_stream, bound_ref[0] - ATTN_HEADROOM)
    sums = jnp.concatenate([acc_ref[h, HEAD_DIM:HEAD_DIM + 1, :] for h in range(N_HEADS)], axis=0)
    sound = jnp.logical_and(sums >= ATTN_MIN_ROW_SUM, sums <= ATTN_MAX_ROW_SUM)
    n_unsound = jnp.sum(jnp.where(sound, 0.0, 1.0))

    @pl.when(n_unsound > 0.0)
    def _():
        all_keys(attend_exact, None)

    for h in range(N_HEADS):
        acc = acc_ref[h]
        ot_ref[HEAD_DIM * h:HEAD_DIM * (h + 1), :] = acc[0:HEAD_DIM] / acc[HEAD_DIM:HEAD_DIM + 1]
    o_ref[...] = ot_ref[...].T.astype(BF16)


def _attention(score_bound, qt, k4, vt4, kc4, vct4, tq):
    b, _, s = qt.shape
    _, n_kb, tk, _ = k4.shape
    ctx = kc4.shape[2]
    return pl.pallas_call(
        _attn_kernel,
        grid=(b, s // tq),
        in_specs=[pl.BlockSpec(memory_space=pltpu.SMEM),
                  pl.BlockSpec((None, ATTN_W, tq), lambda bi, i: (bi, 0, i)),
                  pl.BlockSpec((None, n_kb, tk, KV_W), lambda bi, i: (bi, 0, 0, 0)),
                  pl.BlockSpec((None, n_kb, KV_W, tk), lambda bi, i: (bi, 0, 0, 0)),
                  pl.BlockSpec((None, 1, ctx, KV_W), lambda bi, i: (bi, 0, 0, 0)),
                  pl.BlockSpec((None, 1, KV_W, ctx), lambda bi, i: (bi, 0, 0, 0))],
        out_specs=pl.BlockSpec((None, tq, ATTN_W), lambda bi, i: (bi, i, 0)),
        out_shape=jax.ShapeDtypeStruct((b, s, ATTN_W), BF16),
        scratch_shapes=[pltpu.VMEM((N_HEADS, KV_W, tq), BF16),
                        pltpu.VMEM((N_HEADS, tq), F32),
                        pltpu.VMEM((N_HEADS, HEAD_DIM + 16, tq), F32),
                        pltpu.VMEM((ATTN_W, tq), F32),
                        pltpu.VMEM((ATTN_LOOKAHEAD, tk, tq), F32)],
        compiler_params=_cparams(("parallel", "parallel")),
        name="attention",
    )(score_bound, qt, k4, vt4, kc4, vct4)


def _outproj_kernel(x_ref, mod_ref, attn_ref, bg_ref, z_ref, zp_ref, zn_ref, wc_ref, wo_ref, gpost_ref,
                    o_ref):
    i = pl.program_id(1)
    nt = pl.num_programs(1)
    tm = z_ref.shape[0]
    z = z_ref[...].astype(F32)
    halo = zp_ref.shape[0]
    zprev = zp_ref[...].astype(F32)[halo - 1:halo, :] * (i > 0).astype(F32)
    znext = zn_ref[...].astype(F32)[0:1, :] * (i < nt - 1).astype(F32)
    row = lax.broadcasted_iota(jnp.int32, z.shape, 0)
    zm1 = jnp.where(row == 0, zprev, pltpu.roll(z, 1, 0))
    zp1 = jnp.where(row == tm - 1, znext, pltpu.roll(z, tm - 1, 0))
    wc = wc_ref[...]
    taps = wc[0:1] * zm1 + wc[1:2] * z + wc[2:3] * zp1
    mod = mod_ref[...]
    n_parts = ROW_PARTS if tm % (ROW_PARTS * 16) == 0 else 1
    rows = tm // n_parts
    ys = []
    for p in range(n_parts):
        sl = slice(rows * p, rows * (p + 1))
        conv = (bg_ref[sl, :].astype(F32) * taps[sl, :]).astype(BF16)
        ys.append(_bdot(attn_ref[sl, :], wo_ref[0:ATTN_W, :]) + _bdot(conv, wo_ref[ATTN_W:, :]))
    for p in range(n_parts):
        sl = slice(rows * p, rows * (p + 1))
        o_ref[sl, :] = x_ref[sl, :] + mod[2:3] * _rms(ys[p], gpost_ref[...])


def _outproj(x, mods, attn, bg, z, w_conv, w_out, g_post, tm):
    b, s, d = x.shape
    cw = z.shape[2]
    halo = 16
    r = tm // halo
    last = s // halo - 1
    tok = lambda bi, i: (bi, i, 0)
    return pl.pallas_call(
        _outproj_kernel,
        grid=(b, s // tm),
        in_specs=[pl.BlockSpec((None, tm, d), tok),
                  pl.BlockSpec((None, 6, d), lambda bi, i: (bi, 0, 0)),
                  pl.BlockSpec((None, tm, ATTN_W), tok),
                  pl.BlockSpec((None, tm, cw), tok),
                  pl.BlockSpec((None, tm, cw), tok),
                  pl.BlockSpec((None, halo, cw), lambda bi, i: (bi, jnp.maximum(i * r - 1, 0), 0)),
                  pl.BlockSpec((None, halo, cw), lambda bi, i: (bi, jnp.minimum((i + 1) * r, last), 0)),
                  _const_spec(w_conv.shape),
                  _const_spec(w_out.shape),
                  _const_spec((1, d))],
        out_specs=pl.BlockSpec((None, tm, d), tok),
        out_shape=jax.ShapeDtypeStruct((b, s, d), F32),
        compiler_params=_cparams(("parallel", "parallel")),
        name="outproj",
    )(x, mods, attn, bg, z, z, z, w_conv, w_out, g_post)


def _mix_ffn_kernel(x_ref, mod_ref, attn_ref, bg_ref, z_ref, zp_ref, zn_ref, wc_ref, wo_ref, gpost_ref,
                    gpre2_ref, wg_ref, wu_ref, wd_ref, gpost2_ref, o_ref):
    i = pl.program_id(1)
    nt = pl.num_programs(1)
    tm = z_ref.shape[0]
    z = z_ref[...].astype(F32)
    halo = zp_ref.shape[0]
    zprev = zp_ref[...].astype(F32)[halo - 1:halo, :] * (i > 0).astype(F32)
    znext = zn_ref[...].astype(F32)[0:1, :] * (i < nt - 1).astype(F32)
    row = lax.broadcasted_iota(jnp.int32, z.shape, 0)
    zm1 = jnp.where(row == 0, zprev, pltpu.roll(z, 1, 0))
    zp1 = jnp.where(row == tm - 1, znext, pltpu.roll(z, tm - 1, 0))
    wc = wc_ref[...]
    taps = wc[0:1] * zm1 + wc[1:2] * z + wc[2:3] * zp1
    mod = mod_ref[...]
    n_parts = ROW_PARTS if tm % (ROW_PARTS * 16) == 0 else 1
    rows = tm // n_parts
    parts = [slice(rows * p, rows * (p + 1)) for p in range(n_parts)]
    ys = []
    for sl in parts:
        conv = (bg_ref[sl, :].astype(F32) * taps[sl, :]).astype(BF16)
        ys.append(_bdot(attn_ref[sl, :], wo_ref[0:ATTN_W, :]) + _bdot(conv, wo_ref[ATTN_W:, :]))
    x1s = [x_ref[sl, :] + mod[2:3] * _rms(y, gpost_ref[...]) for sl, y in zip(parts, ys)]
    hs = [_pre(x1, gpre2_ref[...], mod[3:4], mod[4:5]).astype(BF16) for x1 in x1s]
    fs = []
    for h in hs:
        a = (jax.nn.silu(_bdot(h, wg_ref[...])) * _bdot(h, wu_ref[...])).astype(BF16)
        fs.append(_bdot(a, wd_ref[...]))
    for sl, x1, f in zip(parts, x1s, fs):
        o_ref[sl, :] = x1 + mod[5:6] * _rms(f, gpost2_ref[...])


def _outproj_swiglu(x, mods, attn, bg, z, w_conv, w_out, g_post, g_pre2, w_gate, w_up, w_down, g_post2, tm):
    b, s, d = x.shape
    cw = z.shape[2]
    halo = 16
    r = tm // halo
    last = s // halo - 1
    tok = lambda bi, i: (bi, i, 0)
    return pl.pallas_call(
        _mix_ffn_kernel,
        grid=(b, s // tm),
        in_specs=[pl.BlockSpec((None, tm, d), tok),
                  pl.BlockSpec((None, 6, d), lambda bi, i: (bi, 0, 0)),
                  pl.BlockSpec((None, tm, ATTN_W), tok),
                  pl.BlockSpec((None, tm, cw), tok),
                  pl.BlockSpec((None, tm, cw), tok),
                  pl.BlockSpec((None, halo, cw), lambda bi, i: (bi, jnp.maximum(i * r - 1, 0), 0)),
                  pl.BlockSpec((None, halo, cw), lambda bi, i: (bi, jnp.minimum((i + 1) * r, last), 0)),
                  _const_spec(w_conv.shape),
                  _const_spec(w_out.shape),
                  _const_spec((1, d)),
                  _const_spec((1, d)),
                  _const_spec(w_gate.shape),
                  _const_spec(w_up.shape),
                  _const_spec(w_down.shape),
                  _const_spec((1, d))],
        out_specs=pl.BlockSpec((None, tm, d), tok),
        out_shape=jax.ShapeDtypeStruct((b, s, d), F32),
        compiler_params=_cparams(("parallel", "parallel")),
        name="outproj_swiglu",
    )(x, mods, attn, bg, z, z, z, w_conv, w_out, g_post, g_pre2, w_gate, w_up, w_down, g_post2)


def _swiglu_kernel(x_ref, mod_ref, gpre_ref, wg_ref, wu_ref, wd_ref, gpost_ref, o_ref, *, chunk):
    mod = mod_ref[...]
    tm = x_ref.shape[0]
    n_parts = ROW_PARTS if tm % (ROW_PARTS * 16) == 0 else 1
    rows = tm // n_parts
    xs = [x_ref[rows * p:rows * (p + 1), :] for p in range(n_parts)]
    hs = [_pre(xs[p], gpre_ref[...], mod[3:4], mod[4:5]).astype(BF16) for p in range(n_parts)]
    accs = [None] * n_parts
    for c in range(wg_ref.shape[1] // chunk):
        sl = slice(c * chunk, (c + 1) * chunk)
        for p in range(n_parts):
            a = (jax.nn.silu(_bdot(hs[p], wg_ref[:, sl])) * _bdot(hs[p], wu_ref[:, sl])).astype(BF16)
            part = _bdot(a, wd_ref[sl, :])
            accs[p] = part if accs[p] is None else accs[p] + part
    for p in range(n_parts):
        o_ref[rows * p:rows * (p + 1), :] = xs[p] + mod[5:6] * _rms(accs[p], gpost_ref[...])


def _swiglu(x, mods, g_pre, w_gate, w_up, w_down, g_post, tm):
    b, s, d = x.shape
    tok = lambda bi, i: (bi, i, 0)
    return pl.pallas_call(
        functools.partial(_swiglu_kernel, chunk=min(FF_CHUNK_DENSE, w_gate.shape[1])),
        grid=(b, s // tm),
        in_specs=[pl.BlockSpec((None, tm, d), tok),
                  pl.BlockSpec((None, 6, d), lambda bi, i: (bi, 0, 0)),
                  _const_spec((1, d)),
                  _const_spec(w_gate.shape),
                  _const_spec(w_up.shape),
                  _const_spec(w_down.shape),
                  _const_spec((1, d))],
        out_specs=pl.BlockSpec((None, tm, d), tok),
        out_shape=jax.ShapeDtypeStruct((b, s, d), F32),
        compiler_params=_cparams(("parallel", "parallel")),
        name="swiglu",
    )(x, mods, g_pre, w_gate, w_up, w_down, g_post)


def _gmlp_kernel(x_ref, mod_ref, gpre_ref, win_ref, gv_ref, bv_ref, ws_ref, bs_ref, wout_ref, gpost_ref,
                 gffn_ref, wr_ref, br_ref, tri_ref,
                 x3_ref, h3_ref, route_ref, cnt_ref, run_ref):
    first = jnp.logical_and(pl.program_id(0) == 0, pl.program_id(1) == 0)

    @pl.when(first)
    def _():
        run_ref[...] = jnp.zeros_like(run_ref)

    mod = mod_ref[...]
    tm, d = x_ref.shape
    gc = d // GM_GROUPS
    n_parts = GMLP_ROW_PARTS if tm % (GMLP_ROW_PARTS * CHUNK) == 0 else 1
    rows = tm // n_parts
    n_chunks = rows // CHUNK
    parts = range(n_parts)
    xs = [x_ref[rows * i:rows * (i + 1), :] for i in parts]
    hs = [_pre(xs[i], gpre_ref[...], mod[0:1], mod[1:2]).astype(BF16) for i in parts]
    us, vns = [], []
    for i in parts:
        us.append(jax.nn.gelu(_bdot(hs[i], win_ref[:, 0:d])))
        v = jax.nn.gelu(_bdot(hs[i], win_ref[:, d:2 * d]))
        vc = v - jnp.mean(v, axis=-1, keepdims=True)
        var = jnp.mean(vc * vc, axis=-1, keepdims=True)
        vns.append((vc * lax.rsqrt(var + EPS) * gv_ref[...] + bv_ref[...]).astype(BF16))
    gated = []
    for i in parts:
        mixed = []
        for g in range(GM_GROUPS):
            rhs = jnp.concatenate(
                [vns[i][CHUNK * c:CHUNK * (c + 1), gc * g:gc * (g + 1)] for c in range(n_chunks)],
                axis=1)
            mixed.append(_bdot(ws_ref[g], rhs) + jnp.concatenate([bs_ref[g]] * n_chunks, axis=1))
        s = jnp.concatenate(
            [jnp.concatenate([mixed[g][:, gc * c:gc * (c + 1)] for g in range(GM_GROUPS)], axis=1)
             for c in range(n_chunks)], axis=0)
        gated.append((us[i] * s).astype(BF16))
    h3s = []
    for i in parts:
        y = _bdot(gated[i], wout_ref[...])
        x3 = xs[i] + mod[2:3] * _rms(y, gpost_ref[...])
        x3_ref[rows * i:rows * (i + 1), :] = x3
        h3 = _pre(x3, gffn_ref[...], mod[3:4], mod[4:5])
        h3_ref[rows * i:rows * (i + 1), :] = h3
        h3s.append(h3)
    logit_parts = []
    for i in parts:
        h_hi = h3s[i].astype(BF16)
        h_lo = (h3s[i] - h_hi.astype(F32)).astype(BF16)
        prod = _bdot(jnp.concatenate([h_hi, h_lo], axis=0), wr_ref[...])
        logit_parts.append(prod[0:rows, 0:LANES] + prod[0:rows, LANES:2 * LANES] + prod[rows:2 * rows, 0:LANES])
    logits = jnp.concatenate(logit_parts, axis=0) + br_ref[...]
    lane = lax.broadcasted_iota(jnp.int32, logits.shape, 1)
    m1 = jnp.max(logits, axis=1, keepdims=True)
    i1 = jnp.min(jnp.where(logits == m1, lane, LANES), axis=1, keepdims=True)
    rest = jnp.where(lane == i1, 2.0 * NEG_BIG, logits)
    m2 = jnp.max(rest, axis=1, keepdims=True)
    i2 = jnp.min(jnp.where(rest == m2, lane, LANES), axis=1, keepdims=True)
    e21 = jnp.exp(m2 - m1)
    w1 = 1.0 / (1.0 + e21)
    w2 = e21 / (1.0 + e21)
    hot1 = lane == i1
    hot2 = lane == i2
    onehot = jnp.where(jnp.logical_or(hot1, hot2), 1.0, 0.0)
    before = _bdot(tri_ref[...], onehot.astype(BF16)) + run_ref[...]
    r1 = jnp.sum(jnp.where(hot1, before, 0.0), axis=1, keepdims=True)
    r2 = jnp.sum(jnp.where(hot2, before, 0.0), axis=1, keepdims=True)
    run = run_ref[...] + jnp.sum(onehot, axis=0, keepdims=True)
    run_ref[...] = run
    cnt_ref[...] = jnp.broadcast_to(run, cnt_ref.shape)
    fields = (i1.astype(F32), i2.astype(F32), r1, r2, w1, w2)
    route = jnp.zeros(logits.shape, F32)
    for j, f in enumerate(fields):
        route = jnp.where(lane == j, f, route)
    route_ref[...] = route


def _gmlp_router(x, mods, g_pre, w_in, g_v, b_v, w_s, b_s_b, w_out, g_post, g_ffn, w_r, b_r, tm):
    b, s, d = x.shape
    tok = lambda bi, i: (bi, i, 0)
    tri = jnp.tril(jnp.ones((tm, tm), F32), -1).astype(BF16)
    return pl.pallas_call(
        _gmlp_kernel,
        grid=(b, s // tm),
        in_specs=[pl.BlockSpec((None, tm, d), tok),
                  pl.BlockSpec((None, 6, d), lambda bi, i: (bi, 0, 0)),
                  _const_spec((1, d)),
                  _const_spec(w_in.shape),
                  _const_spec((1, d)),
                  _const_spec((1, d)),
                  _const_spec(w_s.shape),
                  _const_spec(b_s_b.shape),
                  _const_spec(w_out.shape),
                  _const_spec((1, d)),
                  _const_spec((1, d)),
                  _const_spec(w_r.shape),
                  _const_spec((1, LANES)),
                  _const_spec((tm, tm))],
        out_specs=[pl.BlockSpec((None, tm, d), tok),
                   pl.BlockSpec((None, tm, d), tok),
                   pl.BlockSpec((None, tm, LANES), tok),
                   pl.BlockSpec((8, LANES), lambda bi, i: (0, 0))],
        out_shape=[jax.ShapeDtypeStruct((b, s, d), F32),
                   jax.ShapeDtypeStruct((b, s, d), F32),
                   jax.ShapeDtypeStruct((b, s, LANES), F32),
                   jax.ShapeDtypeStruct((8, LANES), F32)],
        scratch_shapes=[pltpu.VMEM((1, LANES), F32)],
        compiler_params=_cparams(("arbitrary", "arbitrary")),
        name="gmlp_router",
    )(x, mods, g_pre, w_in, g_v, b_v, w_s, b_s_b, w_out, g_post, g_ffn, w_r, b_r, tri)


def _dispatch_kernel(slot_ref, h_ref, xs_in_ref, xs_ref, sem):
    del xs_in_ref
    tm = h_ref.shape[0]
    for r in range(tm):
        for k in range(2):
            dst = slot_ref[0, 0, 2 * r + k]
            pltpu.make_async_copy(h_ref.at[pl.ds(r, 1), :], xs_ref.at[pl.ds(dst, 1), :], sem).start(priority=k)
    for _ in range(2):
        pltpu.make_async_copy(h_ref, xs_ref.at[pl.ds(0, tm), :], sem).wait()


def _dispatch(h, slots, n_slots, tm):
    n, d = h.shape
    nt = n // tm
    return pl.pallas_call(
        _dispatch_kernel,
        grid=(nt,),
        in_specs=[pl.BlockSpec((1, 1, 2 * tm), lambda i: (i, 0, 0), memory_space=pltpu.SMEM),
                  pl.BlockSpec((tm, d), lambda i: (i, 0)),
                  pl.BlockSpec(memory_space=pl.ANY)],
        out_specs=pl.BlockSpec(memory_space=pl.ANY),
        out_shape=jax.ShapeDtypeStruct((n_slots, d), F32),
        scratch_shapes=[pltpu.SemaphoreType.DMA(())],
        input_output_aliases={2: 0},
        compiler_params=_cparams(("arbitrary",)),
        name="dispatch",
    )(slots.reshape(nt, 1, 2 * tm), h, jnp.zeros((n_slots, d), F32))


def _expert_kernel(te_ref, na_ref, xs_ref, wgu_ref, wd_ref, ys_ref, xb_ref):
    t = pl.program_id(0)
    c = pl.program_id(1)
    active = t < na_ref[0]

    @pl.when(jnp.logical_and(active, c == 0))
    def _():
        xb_ref[...] = xs_ref[...].astype(BF16)

    @pl.when(active)
    def _():
        xb = xb_ref[...]
        acts = []
        for j in range(wd_ref.shape[0] // FF_SUB):
            gu = _bdot(xb, wgu_ref[:, 2 * FF_SUB * j:2 * FF_SUB * (j + 1)])
            acts.append((jax.nn.silu(gu[:, 0:FF_SUB]) * gu[:, FF_SUB:2 * FF_SUB]).astype(BF16))
        part = _bdot(jnp.concatenate(acts, axis=1), wd_ref[...])

        @pl.when(c == 0)
        def _():
            ys_ref[...] = part

        @pl.when(c > 0)
        def _():
            ys_ref[...] += part

    @pl.when(jnp.logical_and(jnp.logical_not(active), c == 0))
    def _():
        ys_ref[...] = jnp.zeros_like(ys_ref)


def _gate_up_kernel(wg_ref, wu_ref, o_ref):
    for j in range(wg_ref.shape[1] // FF_SUB):
        src = slice(FF_SUB * j, FF_SUB * (j + 1))
        o_ref[:, 2 * FF_SUB * j:2 * FF_SUB * j + FF_SUB] = wg_ref[:, src].astype(BF16)
        o_ref[:, 2 * FF_SUB * j + FF_SUB:2 * FF_SUB * (j + 1)] = wu_ref[:, src].astype(BF16)


def _interleave_gate_up(w_gate, w_up):
    ne, d, ff = w_gate.shape
    cols = 2 * FF_SUB
    spec = pl.BlockSpec((None, d, cols), lambda e, j: (e, 0, j))
    return pl.pallas_call(
        _gate_up_kernel,
        grid=(ne, ff // cols),
        in_specs=[spec, spec],
        out_specs=pl.BlockSpec((None, d, 2 * cols), lambda e, j: (e, 0, j)),
        out_shape=jax.ShapeDtypeStruct((ne, d, 2 * ff), BF16),
        compiler_params=_cparams(("parallel", "parallel")),
        name="gate_up_weights",
    )(w_gate, w_up)


def _experts(tile_expert, n_active, xs, w_gu, w_down, tm, chunk):
    n_slots, d = xs.shape
    ff = w_down.shape[1]
    n_tiles = n_slots // tm
    last = ff // chunk - 1

    def live(t, na):
        return jnp.minimum(t, na[0] - 1)

    def chunk_of(t, c, na):
        return jnp.where(t < na[0], c, last)

    grid_spec = pltpu.PrefetchScalarGridSpec(
        num_scalar_prefetch=2,
        grid=(n_tiles, ff // chunk),
        in_specs=[pl.BlockSpec((tm, d), lambda t, c, te, na: (live(t, na), 0)),
                  pl.BlockSpec((None, d, 2 * chunk), lambda t, c, te, na: (te[live(t, na)], 0, chunk_of(t, c, na))),
                  pl.BlockSpec((None, chunk, d), lambda t, c, te, na: (te[live(t, na)], chunk_of(t, c, na), 0))],
        out_specs=pl.BlockSpec((tm, d), lambda t, c, te, na: (t, 0)),
        scratch_shapes=[pltpu.VMEM((tm, d), BF16)],
    )
    return pl.pallas_call(
        _expert_kernel,
        grid_spec=grid_spec,
        out_shape=jax.ShapeDtypeStruct((n_slots, d), F32),
        compiler_params=_cparams(("arbitrary", "arbitrary")),
        name="experts",
    )(tile_expert, n_active, xs, w_gu, w_down)


def _combine_kernel(slot_ref, x_ref, mod_ref, route_ref, gpost_ref, ys_ref, o_ref, buf_ref, sem, *, nt):
    i = pl.program_id(0)
    tm = x_ref.shape[0]

    def issue(b):
        for r in range(tm):
            for k in range(2):
                src = slot_ref[0, 0, 2 * r + k]
                pltpu.make_async_copy(ys_ref.at[pl.ds(src, 1), :], buf_ref.at[b, k, pl.ds(r, 1), :],
                                      sem.at[b]).start(priority=k)

    def drain(b):
        for k in range(2):
            pltpu.make_async_copy(ys_ref.at[pl.ds(0, tm), :], buf_ref.at[b, k], sem.at[b]).wait()

    def finish(b):
        route = route_ref[...]
        f = route[:, 4:5] * buf_ref[b, 0] + route[:, 5:6] * buf_ref[b, 1]
        mod = mod_ref[...]
        o_ref[...] = x_ref[...] + mod[5:6] * _rms(f, gpost_ref[...])

    @pl.when(i == 0)
    def _():
        issue(0)

    for parity in range(2):
        @pl.when(jnp.logical_and(jnp.logical_and(i > 0, i < nt), i % 2 == parity))
        def _(parity=parity):
            drain(1 - parity)
            issue(parity)
            finish(1 - parity)

    @pl.when(i == nt)
    def _():
        drain((nt - 1) % 2)
        finish((nt - 1) % 2)


def _combine(x, mods, route, g_post, ys, slots, seq, tm):
    n, d = x.shape
    nt = n // tm
    per_batch = seq // tm
    done = lambda i: jnp.maximum(i - 1, 0)
    return pl.pallas_call(
        functools.partial(_combine_kernel, nt=nt),
        grid=(nt + 1,),
        in_specs=[pl.BlockSpec((1, 1, 2 * tm), lambda i: (jnp.minimum(i, nt - 1), 0, 0), memory_space=pltpu.SMEM),
                  pl.BlockSpec((tm, d), lambda i: (done(i), 0)),
                  pl.BlockSpec((None, 6, d), lambda i: (done(i) // per_batch, 0, 0)),
                  pl.BlockSpec((tm, LANES), lambda i: (done(i), 0)),
                  _const_spec((1, d)),
                  pl.BlockSpec(memory_space=pl.ANY)],
        out_specs=pl.BlockSpec((tm, d), lambda i: (done(i), 0)),
        out_shape=jax.ShapeDtypeStruct((n, d), F32),
        scratch_shapes=[pltpu.VMEM((2, 2, tm, d), F32), pltpu.SemaphoreType.DMA((2,))],
        compiler_params=_cparams(("arbitrary",)),
        name="combine",
    )(slots.reshape(nt, 1, 2 * tm), x, mods, route, g_post, ys)


def _rope_tables(n):
    axis_dim = HEAD_DIM // 2
    pos = jnp.arange(n, dtype=jnp.int32)
    r = (pos // GRID_W).astype(F32)[:, None]
    col = (pos % GRID_W).astype(F32)[:, None]
    inv = 1.0 / (ROPE_THETA ** (jnp.arange(0, axis_dim, 2, dtype=F32) / axis_dim))
    ang = jnp.concatenate([r * inv, col * inv], axis=-1)
    cos, sin = jnp.cos(ang), jnp.sin(ang)
    zero = jnp.zeros_like(sin)
    reps = LANES // HEAD_DIM
    return (jnp.tile(jnp.concatenate([cos, cos], -1), (1, reps)),
            jnp.tile(jnp.concatenate([-sin, zero], -1), (1, reps)),
            jnp.tile(jnp.concatenate([zero, sin], -1), (1, reps)))


def _head_mean_matrix(width):
    idx = jnp.arange(width) // HEAD_DIM
    return jnp.where(idx[:, None] == idx[None, :], 1.0 / HEAD_DIM, 0.0).astype(BF16)


def kernel(x, c, ctx, c_ctx, e_w_mod, e_b_mod, e_g_pre_mix, e_g_post_mix, e_w_in, e_g_q, e_g_k, e_w_conv, e_w_out, e_g_pre_ffn, e_g_post_ffn, e_w_gate, e_w_up, e_w_down, o_w_mod, o_b_mod, o_g_pre_mix, o_g_post_mix, o_w_in, o_g_v, o_b_v, o_w_s, o_b_s, o_w_out, o_g_pre_ffn, o_g_post_ffn, o_w_router, o_b_router, o_w_gate, o_w_up, o_w_down):
    b, s, d = x.shape
    n_ctx = ctx.shape[1]
    n = b * s
    tm = min(TOKEN_TILE, s)
    tq = min(ATTN_Q_TILE, s)
    assert b + 1 <= 8 and s % tm == 0 and s % tq == 0 and tm % CHUNK == 0 and n_ctx % 16 == 0
    assert FF_CHUNK_EXPERT % FF_SUB == 0
    assert e_w_mod.shape[0] == 1 and o_w_mod.shape[0] == 1
    row = lambda g: g.reshape(1, -1)

    cond8 = jnp.zeros((8, d), F32).at[:b].set(c).at[b].set(c_ctx)
    mods_e = _modulation(cond8, e_w_mod[0], e_b_mod[0])
    mods_o = _modulation(cond8, o_w_mod[0], o_b_mod[0])

    w_in = e_w_in[0].astype(BF16)
    gq = jnp.tile(e_g_q[0], N_HEADS).reshape(1, ATTN_W)
    gk = jnp.tile(e_g_k[0], N_KV_HEADS).reshape(1, KV_W)
    eq, ek = _head_mean_matrix(ATTN_W), _head_mean_matrix(KV_W)
    cos, sa, sb = _rope_tables(s)
    qt, k, vt4, bg, z = _inproj(x, mods_e, None, row(e_g_pre_mix[0]), w_in, gq, gk, eq, ek, cos, sa, sb, tm)
    ones = jnp.ones((n_ctx, LANES), F32)
    zeros = jnp.zeros((n_ctx, LANES), F32)
    _, kc, vct4, _, _ = _inproj(ctx, mods_e, b, row(e_g_pre_mix[0]), w_in, gq, gk, eq, ek, ones, zeros, zeros,
                                n_ctx)
    score_bound = (ATTN_BOUND_MARGIN * HEAD_DIM ** 0.5 * LOG2E * jnp.max(jnp.abs(e_g_q[0]))
                   * jnp.max(jnp.abs(e_g_k[0]))).reshape(1).astype(F32)
    attn = _attention(score_bound, qt, k.reshape(b, s // tm, tm, KV_W), vt4, kc.reshape(b, 1, n_ctx, KV_W), vct4, tq)
    x2 = _outproj_swiglu(x, mods_e, attn, bg, z, e_w_conv[0], e_w_out[0].astype(BF16), row(e_g_post_mix[0]),
                         row(e_g_pre_ffn[0]), e_w_gate[0].astype(BF16), e_w_up[0].astype(BF16),
                         e_w_down[0].astype(BF16), row(e_g_post_ffn[0]), tm)

    b_s_b = jnp.broadcast_to(o_b_s[0][:, :, None], (GM_GROUPS, CHUNK, d // GM_GROUPS))
    w_r32 = jnp.zeros((d, LANES), F32).at[:, :N_EXPERTS].set(o_w_router[0])
    w_r_hi = w_r32.astype(BF16)
    w_r = jnp.concatenate([w_r_hi, (w_r32 - w_r_hi.astype(F32)).astype(BF16)], axis=1)
    b_r = jnp.full((1, LANES), NEG_BIG, F32).at[0, :N_EXPERTS].set(o_b_router[0])
    x3, h3, route, counts = _gmlp_router(
        x2, mods_o, row(o_g_pre_mix[0]), o_w_in[0].astype(BF16), row(o_g_v[0]), row(o_b_v[0]),
        o_w_s[0].astype(BF16), b_s_b, o_w_out[0].astype(BF16), row(o_g_post_mix[0]), row(o_g_pre_ffn[0]),
        w_r, b_r, tm)

    te_rows = EXPERT_TILE
    n_tiles = -(-(2 * n + N_EXPERTS * (te_rows - 1)) // te_rows)
    n_slots = n_tiles * te_rows
    cnt = counts[0, :N_EXPERTS].astype(jnp.int32)
    tiles_per = (cnt + te_rows - 1) // te_rows
    tile_end = jnp.cumsum(tiles_per)
    base = (tile_end - tiles_per) * te_rows
    route2 = route.reshape(n, LANES)
    e12 = route2[:, 0:2].astype(jnp.int32)
    r12 = route2[:, 2:4].astype(jnp.int32)
    slots = (base[e12] + r12).reshape(-1)
    tile_expert = jnp.minimum(
        jnp.sum(jnp.arange(n_tiles, dtype=jnp.int32)[:, None] >= tile_end[None, :], axis=1),
        N_EXPERTS - 1).astype(jnp.int32)
    n_active = tile_end[-1:].astype(jnp.int32)

    rt = min(ROUTE_TILE, s)
    xs = _dispatch(h3.reshape(n, d), slots, n_slots, rt)
    w_gu = _interleave_gate_up(o_w_gate[0], o_w_up[0])
    ys = _experts(tile_expert, n_active, xs, w_gu, o_w_down[0].astype(BF16), te_rows, FF_CHUNK_EXPERT)
    out = _combine(x3.reshape(n, d), mods_o, route2, row(o_g_post_ffn[0]), ys, slots, s, rt)
    return out.reshape(b, s, d)
```
